```python
import math
import jax, jax.numpy as jnp
from jax import lax
import numpy as np

D_MODEL = 2048
BATCH = 4
SEQ = 4096
DEPTH = 4

GDN_HEADS = 8
GDN_HD = 128
GDN_W = GDN_HEADS * GDN_HD
GDN_CONV = 4
GDN_CHUNK = 64
SC_WIDTH = D_MODEL // 2
SC_CONV = 3
NSA_HEADS = 16
NSA_KV = 4
NSA_HD = 128
NSA_Q = NSA_HEADS * NSA_HD
NSA_KVW = NSA_KV * NSA_HD
CMP_LEN = 32
CMP_STRIDE = 16
CMP_HID = 256
SEL_BLOCK = 64
SEL_N = 16
WIN = 512
ATT_QBLK = 128
SEL_QBLK = 32
FORCE_BONUS = 1000.0
ROPE_THETA = 10000.0
D_FF = 5632
FFN_CONV = 3
EPS = 1e-6
NEG = -1e30

AB_COLS = 4 * GDN_W + 2 * GDN_HEADS + 3 * SC_WIDTH
NSA_COLS = NSA_Q + 6 * NSA_KVW + 3 * NSA_HEADS

kernel_name = "hybrid_deltanet_shortconv_nsa_convffn"


def split_cols(t, widths):
    offs = np.cumsum(widths)[:-1].tolist()
    return jnp.split(t, offs, axis=-1)


def rmsnorm(x, w):
    xf = x.astype(jnp.float32)
    y = xf * lax.rsqrt(jnp.mean(xf * xf, axis=-1, keepdims=True) + EPS)
    return (y * w.astype(jnp.float32)).astype(x.dtype)


def l2norm(x):
    return x * lax.rsqrt(jnp.sum(x * x, axis=-1, keepdims=True) + EPS)


def causal_dwconv(x, w):
    K = w.shape[0]
    S = x.shape[1]
    xp = jnp.pad(x, ((0, 0), (K - 1, 0), (0, 0)))
    return sum(xp[:, k:k + S] * w[k] for k in range(K))


def rope_tables(S, dim):
    inv = 1.0 / (ROPE_THETA ** (jnp.arange(0, dim, 2, dtype=jnp.float32) / dim))
    ang = jnp.arange(S, dtype=jnp.float32)[:, None] * inv[None, :]
    return jnp.cos(ang), jnp.sin(ang)


def apply_rope(x, cos, sin):
    x1, x2 = jnp.split(x, 2, axis=-1)
    c = cos[None, :, None, :]
    s = sin[None, :, None, :]
    return jnp.concatenate([x1 * c - x2 * s, x2 * c + x1 * s], axis=-1)


def gated_delta_chunked(q, k, v, g, beta):
    B, H, S, D = q.shape
    C = GDN_CHUNK
    N = S // C
    q = q * (D ** -0.5)
    rs = lambda t: t.reshape((B, H, N, C) + t.shape[3:])
    q, k, v, beta = rs(q), rs(k), rs(v), rs(beta)
    g = jnp.cumsum(rs(g), axis=-1)
    kb = k * beta[..., None]
    vb = v * beta[..., None]
    tril = jnp.tril(jnp.ones((C, C), dtype=bool))
    stril = jnp.tril(jnp.ones((C, C), dtype=bool), -1)
    gdiff = g[..., :, None] - g[..., None, :]
    decay = jnp.where(tril, jnp.exp(jnp.where(tril, gdiff, 0.0)), 0.0)
    Lm = jnp.where(stril, jnp.einsum('bhnid,bhnjd->bhnij', kb, k) * decay, 0.0)
    eye = jnp.eye(C, dtype=jnp.float32)
    T = lax.linalg.triangular_solve(eye + Lm, jnp.broadcast_to(eye, Lm.shape),
                                    left_side=True, lower=True)
    u = T @ vb
    w = T @ (kb * jnp.exp(g)[..., None])
    qk = jnp.where(tril, jnp.einsum('bhnid,bhnjd->bhnij', q, k) * decay, 0.0)

    def step(state, xs):
        qc, kc, uc, wc, gc, ac = xs
        vnew = uc - wc @ state
        o = (qc * jnp.exp(gc)[..., None]) @ state + ac @ vnew
        glast = gc[..., -1]
        kdec = kc * jnp.exp(glast[..., None] - gc)[..., None]
        state = state * jnp.exp(glast)[..., None, None] + jnp.einsum('bhcd,bhce->bhde', kdec, vnew)
        return state, o

    xs = tuple(jnp.moveaxis(t, 2, 0) for t in (q, k, u, w, g, qk))
    s0 = jnp.zeros((B, H, D, D), jnp.float32)
    _, o = lax.scan(step, s0, xs)
    return jnp.moveaxis(o, 0, 2).reshape(B, H, S, D)


def delta_shortconv_mixer(xn, w_in, w_out, conv_qkv, a_log, dt_bias, o_norm, conv_sc):
    B, S, _ = xn.shape
    H, D = GDN_HEADS, GDN_HD
    f32 = jnp.float32
    q, k, v, z, a, b, hb, gb, gc = split_cols(xn @ w_in, [H * D] * 4 + [H] * 2 + [SC_WIDTH] * 3)
    qkv = jax.nn.silu(causal_dwconv(jnp.concatenate([q, k, v], axis=-1), conv_qkv)).astype(f32)
    q, k, v = [t.reshape(B, S, H, D).transpose(0, 2, 1, 3) for t in jnp.split(qkv, 3, axis=-1)]
    g = -jnp.exp(a_log.astype(f32)) * jax.nn.softplus(a.astype(f32) + dt_bias.astype(f32))
    beta = jax.nn.sigmoid(b.astype(f32))
    o = gated_delta_chunked(l2norm(q), l2norm(k), v, g.transpose(0, 2, 1), beta.transpose(0, 2, 1))
    o = rmsnorm(o.transpose(0, 2, 1, 3), o_norm) * jax.nn.silu(z.astype(f32).reshape(B, S, H, D))
    y_a = o.reshape(B, S, H * D).astype(xn.dtype)
    y_b = gb * causal_dwconv(gc * hb, conv_sc)
    return jnp.concatenate([y_a, y_b], axis=-1) @ w_out


def nsa_compress(t, pos, w1, w2):
    B, G, S, D = t.shape
    r = CMP_LEN // CMP_STRIDE
    nc = S // CMP_STRIDE - r + 1
    tr = t.reshape(B, G, S // CMP_STRIDE, CMP_STRIDE, D)
    blocks = jnp.concatenate([tr[:, :, j:j + nc] for j in range(r)], axis=3)
    blocks = (blocks + pos).reshape(B, G, nc, CMP_LEN * D)
    return jax.nn.gelu(blocks @ w1) @ w2


def nsa_compressed_branch(q, kc, vc):
    S = q.shape[3]
    nc = kc.shape[2]
    t = jnp.arange(S)
    valid = (jnp.arange(nc) * CMP_STRIDE + CMP_LEN - 1)[None, :] <= t[:, None]
    s = jnp.einsum('bgrsd,bgcd->bgrsc', q, kc)
    p = jnp.where(valid, jax.nn.softmax(jnp.where(valid, s, NEG), axis=-1), 0.0)
    return jnp.einsum('bgrsc,bgcd->bgrsd', p, vc), p


def nsa_selected_branch(q, k, v, p_cmp):
    B, G, R, S, D = q.shape
    nc = p_cmp.shape[-1]
    ns = S // SEL_BLOCK
    n_sel = min(SEL_N, ns)
    cs = jnp.arange(nc) * CMP_STRIDE
    ss = jnp.arange(ns) * SEL_BLOCK
    ov = jnp.minimum(cs[:, None] + CMP_LEN, ss[None, :] + SEL_BLOCK) - jnp.maximum(cs[:, None], ss[None, :])
    M = (jnp.clip(ov, 0, None) / CMP_STRIDE).astype(jnp.float32)
    score = jnp.einsum('bgrsc,cn->bgsn', p_cmp, M)
    t = jnp.arange(S)
    blk = jnp.arange(ns)
    cur = (t // SEL_BLOCK)[:, None]
    valid = blk[None, :] * SEL_BLOCK <= t[:, None]
    forced = (blk[None, :] == 0) | (blk[None, :] == cur) | (blk[None, :] == cur - 1)
    score = jnp.where(valid, score + jnp.where(forced, FORCE_BONUS, 0.0), NEG)
    _, idx = lax.top_k(score, n_sel)

    kb = k.reshape(B, G, ns, SEL_BLOCK, D)
    vb = v.reshape(B, G, ns, SEL_BLOCK, D)
    nq = S // SEL_QBLK
    q_ch = q.reshape(B, G, R, nq, SEL_QBLK, D).transpose(3, 0, 1, 2, 4, 5)
    i_ch = idx.reshape(B, G, nq, SEL_QBLK, n_sel).transpose(2, 0, 1, 3, 4)
    t_ch = t.reshape(nq, SEL_QBLK)
    bi = jnp.arange(B)[:, None, None, None]
    gi = jnp.arange(G)[None, :, None, None]

    def block(args):
        qb, ib, tb = args
        kg = kb[bi, gi, ib].reshape(B, G, SEL_QBLK, n_sel * SEL_BLOCK, D)
        vg = vb[bi, gi, ib].reshape(B, G, SEL_QBLK, n_sel * SEL_BLOCK, D)
        kpos = (ib[..., None] * SEL_BLOCK + jnp.arange(SEL_BLOCK)).reshape(B, G, SEL_QBLK, n_sel * SEL_BLOCK)
        mask = (kpos <= tb[None, None, :, None])[:, :, None]
        s = jnp.einsum('bgrqd,bgqkd->bgrqk', qb, kg)
        p = jax.nn.softmax(jnp.where(mask, s, NEG), axis=-1)
        return jnp.einsum('bgrqk,bgqkd->bgrqd', p, vg)

    o = lax.map(block, (q_ch, i_ch, t_ch))
    return o.transpose(1, 2, 3, 0, 4, 5).reshape(B, G, R, S, D)


def nsa_window_branch(q, k, v):
    B, G, R, S, D = q.shape
    QB = ATT_QBLK
    KL = WIN + QB
    nb = S // QB
    kp = jnp.pad(k, ((0, 0), (0, 0), (WIN, 0), (0, 0)))
    vp = jnp.pad(v, ((0, 0), (0, 0), (WIN, 0), (0, 0)))
    q_bl = q.reshape(B, G, R, nb, QB, D).transpose(3, 0, 1, 2, 4, 5)

    def block(args):
        qb, i = args
        start = i * QB
        kb = lax.dynamic_slice_in_dim(kp, start, KL, axis=2)
        vb = lax.dynamic_slice_in_dim(vp, start, KL, axis=2)
        tq = start + jnp.arange(QB)
        tk = start - WIN + jnp.arange(KL)
        mask = (tk[None, :] <= tq[:, None]) & (tk[None, :] > tq[:, None] - WIN) & (tk[None, :] >= 0)
        s = jnp.einsum('bgrqd,bgkd->bgrqk', qb, kb)
        p = jax.nn.softmax(jnp.where(mask, s, NEG), axis=-1)
        return jnp.einsum('bgrqk,bgkd->bgrqd', p, vb)

    o = lax.map(block, (q_bl, jnp.arange(nb)))
    return o.transpose(1, 2, 3, 0, 4, 5).reshape(B, G, R, S, D)


def nsa_mixer(xn, w_in, w_out, pos_k, w1_k, w2_k, pos_v, w1_v, w2_v, cos, sin):
    B, S, _ = xn.shape
    H, G, D = NSA_HEADS, NSA_KV, NSA_HD
    R = H // G
    f32 = jnp.float32
    q, kc, vc, ks, vs, kw, vw, gt = split_cols((xn @ w_in).astype(f32), [H * D] + [G * D] * 6 + [3 * H])
    q = apply_rope(q.reshape(B, S, H, D), cos, sin) * (D ** -0.5)
    q = q.reshape(B, S, G, R, D).transpose(0, 2, 3, 1, 4)
    rk = lambda t: apply_rope(t.reshape(B, S, G, D), cos, sin).transpose(0, 2, 1, 3)
    rv = lambda t: t.reshape(B, S, G, D).transpose(0, 2, 1, 3)
    kc, ks, kw = rk(kc), rk(ks), rk(kw)
    vc, vs, vw = rv(vc), rv(vs), rv(vw)
    gates = jax.nn.sigmoid(gt).reshape(B, S, 3, G, R).transpose(2, 0, 3, 4, 1)[..., None]
    kcmp = nsa_compress(kc, pos_k.astype(f32), w1_k.astype(f32), w2_k.astype(f32))
    vcmp = nsa_compress(vc, pos_v.astype(f32), w1_v.astype(f32), w2_v.astype(f32))
    o_cmp, p_cmp = nsa_compressed_branch(q, kcmp, vcmp)
    o_slc = nsa_selected_branch(q, ks, vs, p_cmp)
    o_win = nsa_window_branch(q, kw, vw)
    o = gates[0] * o_cmp + gates[1] * o_slc + gates[2] * o_win
    o = o.transpose(0, 3, 1, 2, 4).reshape(B, S, H * D).astype(xn.dtype)
    return o @ w_out


def conv_ffn(xn, w_gu, conv_w, w_down):
    gate, up = jnp.split(xn @ w_gu, 2, axis=-1)
    return (jax.nn.silu(causal_dwconv(gate, conv_w)) * up) @ w_down


def setup_inputs(seed: int = 0) -> dict:
    key = jax.random.key(seed)
    ks = iter(jax.random.split(key, 32))
    nrm = lambda shape, scale: jax.random.normal(next(ks), shape, jnp.float32) * scale
    NE = (DEPTH + 1) // 2
    NO = DEPTH // 2
    x = nrm((BATCH, SEQ, D_MODEL), 1.0)
    norm_w = 1.0 + nrm((DEPTH, 4, D_MODEL), 0.02)
    ab_w_in = nrm((NE, D_MODEL, AB_COLS), D_MODEL ** -0.5)
    ab_w_out = nrm((NE, GDN_W + SC_WIDTH, D_MODEL), (GDN_W + SC_WIDTH) ** -0.5)
    gdn_conv = nrm((NE, GDN_CONV, 3 * GDN_W), GDN_CONV ** -0.5)
    gdn_a_log = jnp.log(jax.random.uniform(next(ks), (NE, GDN_HEADS), jnp.float32, 1.0, 16.0))
    dt = jnp.exp(jax.random.uniform(next(ks), (NE, GDN_HEADS), jnp.float32, math.log(1e-3), math.log(1e-1)))
    gdn_dt_bias = dt + jnp.log(-jnp.expm1(-dt))
    gdn_norm = 1.0 + nrm((NE, GDN_HD), 0.02)
    sc_conv = nrm((NE, SC_CONV, SC_WIDTH), SC_CONV ** -0.5)
    nsa_w_in = nrm((NO, D_MODEL, NSA_COLS), D_MODEL ** -0.5)
    nsa_w_out = nrm((NO, NSA_Q, D_MODEL), NSA_Q ** -0.5)
    cmp_pos_k = nrm((NO, CMP_LEN, NSA_HD), 0.02)
    cmp_w1_k = nrm((NO, CMP_LEN * NSA_HD, CMP_HID), (CMP_LEN * NSA_HD) ** -0.5)
    cmp_w2_k = nrm((NO, CMP_HID, NSA_HD), CMP_HID ** -0.5)
    cmp_pos_v = nrm((NO, CMP_LEN, NSA_HD), 0.02)
    cmp_w1_v = nrm((NO, CMP_LEN * NSA_HD, CMP_HID), (CMP_LEN * NSA_HD) ** -0.5)
    cmp_w2_v = nrm((NO, CMP_HID, NSA_HD), CMP_HID ** -0.5)
    ffn_w_gu = nrm((DEPTH, D_MODEL, 2 * D_FF), D_MODEL ** -0.5)
    ffn_conv = nrm((DEPTH, FFN_CONV, D_FF), FFN_CONV ** -0.5)
    ffn_w_down = nrm((DEPTH, D_FF, D_MODEL), D_FF ** -0.5)
    return {"x": x, "norm_w": norm_w, "ab_w_in": ab_w_in, "ab_w_out": ab_w_out,
            "gdn_conv": gdn_conv, "gdn_a_log": gdn_a_log, "gdn_dt_bias": gdn_dt_bias,
            "gdn_norm": gdn_norm, "sc_conv": sc_conv, "nsa_w_in": nsa_w_in, "nsa_w_out": nsa_w_out,
            "cmp_pos_k": cmp_pos_k, "cmp_w1_k": cmp_w1_k, "cmp_w2_k": cmp_w2_k,
            "cmp_pos_v": cmp_pos_v, "cmp_w1_v": cmp_w1_v, "cmp_w2_v": cmp_w2_v,
            "ffn_w_gu": ffn_w_gu, "ffn_conv": ffn_conv, "ffn_w_down": ffn_w_down}


def reference(x, norm_w, ab_w_in, ab_w_out, gdn_conv, gdn_a_log, gdn_dt_bias, gdn_norm, sc_conv,
              nsa_w_in, nsa_w_out, cmp_pos_k, cmp_w1_k, cmp_w2_k, cmp_pos_v, cmp_w1_v, cmp_w2_v,
              ffn_w_gu, ffn_conv, ffn_w_down):
    cos, sin = rope_tables(x.shape[1], NSA_HD)
    h = x
    for l in range(DEPTH):
        i = l // 2
        xn = rmsnorm(h, norm_w[l, 0])
        if l % 2 == 0:
            m = delta_shortconv_mixer(xn, ab_w_in[i], ab_w_out[i], gdn_conv[i], gdn_a_log[i],
                                      gdn_dt_bias[i], gdn_norm[i], sc_conv[i])
        else:
            m = nsa_mixer(xn, nsa_w_in[i], nsa_w_out[i], cmp_pos_k[i], cmp_w1_k[i], cmp_w2_k[i],
                          cmp_pos_v[i], cmp_w1_v[i], cmp_w2_v[i], cos, sin)
        h = h + rmsnorm(m, norm_w[l, 1])
        f = conv_ffn(rmsnorm(h, norm_w[l, 2]), ffn_w_gu[l], ffn_conv[l], ffn_w_down[l])
        h = h + rmsnorm(f, norm_w[l, 3])
    return h
```

```python
import functools
import math

import jax
import jax.numpy as jnp
import numpy as np
from jax import lax
from jax.experimental import pallas as pl
from jax.experimental.pallas import tpu as pltpu

F32 = jnp.float32
BF16 = jnp.bfloat16

EPS = 1e-6
NEG = -1e30

GDN_HEADS = 8
GDN_HD = 128
GDN_CHUNK = 64
NSA_HEADS = 16
NSA_KV = 4
NSA_HD = 128
CMP_LEN = 32
CMP_STRIDE = 16
SEL_BLOCK = 64
SEL_N = 16
WIN = 512
FORCE_BONUS = 1000.0
ROPE_THETA = 10000.0

VMEM_LIMIT_BYTES = 56 * 1024 * 1024
BF16_ROWS = 16


def _cparams(*sem):
    return pltpu.CompilerParams(dimension_semantics=sem, vmem_limit_bytes=VMEM_LIMIT_BYTES)


def _rms(x, w):
    return x * lax.rsqrt(jnp.mean(x * x, axis=-1, keepdims=True) + EPS) * w


def _norm_mm_kernel(h_ref, nw_ref, w_ref, o_ref, xn_ref):
    @pl.when(pl.program_id(1) == 0)
    def _():
        xn_ref[...] = _rms(h_ref[...], nw_ref[...]).astype(BF16)

    o_ref[...] = jnp.dot(xn_ref[...], w_ref[...], preferred_element_type=F32).astype(o_ref.dtype)


def norm_matmul(h, nw, w, out_dtype, tm=512, tn=1024):
    M, D = h.shape
    N = w.shape[1]
    tn = min(tn, N)
    return pl.pallas_call(
        _norm_mm_kernel,
        grid=(M // tm, N // tn),
        in_specs=[pl.BlockSpec((tm, D), lambda i, j: (i, 0)),
                  pl.BlockSpec((1, D), lambda i, j: (0, 0)),
                  pl.BlockSpec((D, tn), lambda i, j: (0, j))],
        out_specs=pl.BlockSpec((tm, tn), lambda i, j: (i, j)),
        out_shape=jax.ShapeDtypeStruct((M, N), out_dtype),
        scratch_shapes=[pltpu.VMEM((tm, D), BF16)],
        compiler_params=_cparams("parallel", "arbitrary"),
        name="norm_matmul",
    )(h, nw.reshape(1, D), w)


def _out_proj_kernel(*refs, n_parts, concat):
    parts = refs[:n_parts]
    w_ref, h_ref, nw_ref, o_ref = refs[n_parts:]
    if concat:
        x = jnp.concatenate([p[...] for p in parts], axis=1)
    else:
        x = parts[0][...].astype(F32)
        for p in parts[1:]:
            x = x + p[...].astype(F32)
        x = x.astype(BF16)
    m = jnp.dot(x, w_ref[...], preferred_element_type=F32)
    o_ref[...] = h_ref[...] + _rms(m, nw_ref[...])


def out_proj(parts, w, h, nw, concat, tm=512):
    M, D = h.shape
    K = w.shape[0]
    in_specs = [pl.BlockSpec((tm, p.shape[1]), lambda i: (i, 0)) for p in parts]
    in_specs += [pl.BlockSpec((K, D), lambda i: (0, 0)),
                 pl.BlockSpec((tm, D), lambda i: (i, 0)),
                 pl.BlockSpec((1, D), lambda i: (0, 0))]
    return pl.pallas_call(
        functools.partial(_out_proj_kernel, n_parts=len(parts), concat=concat),
        grid=(M // tm,),
        in_specs=in_specs,
        out_specs=pl.BlockSpec((tm, D), lambda i: (i, 0)),
        out_shape=jax.ShapeDtypeStruct((M, D), F32),
        compiler_params=_cparams("parallel"),
        name="out_proj",
    )(*parts, w, h, nw.reshape(1, D))


def _ffn_kernel(h_ref, halo_ref, nw2_ref, wg_ref, wu_ref, cw_ref, wd_ref, nw3_ref, o_ref,
                xn_ref, g_ref, acc_ref, *, tiles_per_seq):
    i = pl.program_id(0)
    j = pl.program_id(1)
    tm = h_ref.shape[0]
    H = BF16_ROWS

    @pl.when(j == 0)
    def _():
        xn_ref[pl.ds(H, tm), :] = _rms(h_ref[...], nw2_ref[...]).astype(BF16)
        halo = _rms(halo_ref[...], nw2_ref[...])
        halo = jnp.where(i % tiles_per_seq == 0, 0.0, halo)
        xn_ref[pl.ds(0, H), :] = halo.astype(BF16)
        acc_ref[...] = jnp.zeros_like(acc_ref)

    g_ref[...] = jnp.dot(xn_ref[...], wg_ref[...], preferred_element_type=F32)
    u = jnp.dot(xn_ref[pl.ds(H, tm), :], wu_ref[...], preferred_element_type=F32)
    cw = cw_ref[...]
    c = (g_ref[pl.ds(H, tm), :] * cw[2:3, :] + g_ref[pl.ds(H - 1, tm), :] * cw[1:2, :]
         + g_ref[pl.ds(H - 2, tm), :] * cw[0:1, :])
    hid = (c * jax.nn.sigmoid(c) * u).astype(BF16)
    acc_ref[...] += jnp.dot(hid, wd_ref[...], preferred_element_type=F32)

    @pl.when(j == pl.num_programs(1) - 1)
    def _():
        o_ref[...] = h_ref[...] + _rms(acc_ref[...], nw3_ref[...])


def conv_ffn_block(h, seq_len, nw2, w_gu, conv_w, w_down, nw3, tm=512, tf=512):
    M, D = h.shape
    FF = w_down.shape[0]
    nf = FF // tf
    H = BF16_ROWS
    kern = functools.partial(_ffn_kernel, tiles_per_seq=seq_len // tm)
    return pl.pallas_call(
        kern,
        grid=(M // tm, nf),
        in_specs=[pl.BlockSpec((tm, D), lambda i, j: (i, 0)),
                  pl.BlockSpec((H, D), lambda i, j: (jnp.maximum(i * (tm // H) - 1, 0), 0)),
                  pl.BlockSpec((1, D), lambda i, j: (0, 0)),
                  pl.BlockSpec((D, tf), lambda i, j: (0, j)),
                  pl.BlockSpec((D, tf), lambda i, j: (0, nf + j)),
                  pl.BlockSpec((3, tf), lambda i, j: (0, j)),
                  pl.BlockSpec((tf, D), lambda i, j: (j, 0)),
                  pl.BlockSpec((1, D), lambda i, j: (0, 0))],
        out_specs=pl.BlockSpec((tm, D), lambda i, j: (i, 0)),
        out_shape=jax.ShapeDtypeStruct((M, D), F32),
        scratch_shapes=[pltpu.VMEM((tm + H, D), BF16),
                        pltpu.VMEM((tm + H, tf), F32),
                        pltpu.VMEM((tm, D), F32)],
        compiler_params=_cparams("parallel", "arbitrary"),
        name="conv_ffn",
    )(h, h, nw2.reshape(1, D), w_gu, w_gu, conv_w, w_down, nw3.reshape(1, D))


def split_cols(t, widths):
    offs = np.cumsum(widths)[:-1].tolist()
    return jnp.split(t, offs, axis=-1)


def rmsnorm(x, w):
    xf = x.astype(jnp.float32)
    y = xf * lax.rsqrt(jnp.mean(xf * xf, axis=-1, keepdims=True) + EPS)
    return (y * w.astype(jnp.float32)).astype(x.dtype)


def l2norm(x):
    return x * lax.rsqrt(jnp.sum(x * x, axis=-1, keepdims=True) + EPS)


def causal_dwconv(x, w):
    K = w.shape[0]
    S = x.shape[1]
    xp = jnp.pad(x, ((0, 0), (K - 1, 0), (0, 0)))
    return sum(xp[:, k:k + S] * w[k] for k in range(K))


def rope_tables(S, dim):
    inv = 1.0 / (ROPE_THETA ** (jnp.arange(0, dim, 2, dtype=jnp.float32) / dim))
    ang = jnp.arange(S, dtype=jnp.float32)[:, None] * inv[None, :]
    return jnp.cos(ang), jnp.sin(ang)


def apply_rope(x, cos, sin):
    x1, x2 = jnp.split(x, 2, axis=-1)
    c = cos[None, :, None, :]
    s = sin[None, :, None, :]
    return jnp.concatenate([x1 * c - x2 * s, x2 * c + x1 * s], axis=-1)


def gated_delta_chunked(q, k, v, g, beta):
    B, H, S, D = q.shape
    C = GDN_CHUNK
    N = S // C
    q = q * (D ** -0.5)
    rs = lambda t: t.reshape((B, H, N, C) + t.shape[3:])
    q, k, v, beta = rs(q), rs(k), rs(v), rs(beta)
    g = jnp.cumsum(rs(g), axis=-1)
    kb = k * beta[..., None]
    vb = v * beta[..., None]
    tril = jnp.tril(jnp.ones((C, C), dtype=bool))
    stril = jnp.tril(jnp.ones((C, C), dtype=bool), -1)
    gdiff = g[..., :, None] - g[..., None, :]
    decay = jnp.where(tril, jnp.exp(jnp.where(tril, gdiff, 0.0)), 0.0)
    Lm = jnp.where(stril, jnp.einsum('bhnid,bhnjd->bhnij', kb, k) * decay, 0.0)
    eye = jnp.eye(C, dtype=jnp.float32)
    T = lax.linalg.triangular_solve(eye + Lm, jnp.broadcast_to(eye, Lm.shape),
                                    left_side=True, lower=True)
    u = T @ vb
    w = T @ (kb * jnp.exp(g)[..., None])
    qk = jnp.where(tril, jnp.einsum('bhnid,bhnjd->bhnij', q, k) * decay, 0.0)

    def step(state, xs):
        qc, kc, uc, wc, gc, ac = xs
        vnew = uc - wc @ state
        o = (qc * jnp.exp(gc)[..., None]) @ state + ac @ vnew
        glast = gc[..., -1]
        kdec = kc * jnp.exp(glast[..., None] - gc)[..., None]
        state = state * jnp.exp(glast)[..., None, None] + jnp.einsum('bhcd,bhce->bhde', kdec, vnew)
        return state, o

    xs = tuple(jnp.moveaxis(t, 2, 0) for t in (q, k, u, w, g, qk))
    s0 = jnp.zeros((B, H, D, D), jnp.float32)
    _, o = lax.scan(step, s0, xs)
    return jnp.moveaxis(o, 0, 2).reshape(B, H, S, D)


def delta_shortconv_core(y, conv_qkv, a_log, dt_bias, o_norm, conv_sc):
    B, S, _ = y.shape
    H, D = GDN_HEADS, GDN_HD
    SCW = H * D
    f32 = jnp.float32
    q, k, v, z, a, b, hb, gb, gc = split_cols(y, [H * D] * 4 + [H] * 2 + [SCW] * 3)
    qkv = jax.nn.silu(causal_dwconv(jnp.concatenate([q, k, v], axis=-1), conv_qkv)).astype(f32)
    q, k, v = [t.reshape(B, S, H, D).transpose(0, 2, 1, 3) for t in jnp.split(qkv, 3, axis=-1)]
    g = -jnp.exp(a_log.astype(f32)) * jax.nn.softplus(a.astype(f32) + dt_bias.astype(f32))
    beta = jax.nn.sigmoid(b.astype(f32))
    o = gated_delta_chunked(l2norm(q), l2norm(k), v, g.transpose(0, 2, 1), beta.transpose(0, 2, 1))
    o = rmsnorm(o.transpose(0, 2, 1, 3), o_norm) * jax.nn.silu(z.astype(f32).reshape(B, S, H, D))
    y_a = o.reshape(B, S, H * D)
    y_b = gb * causal_dwconv(gc * hb, conv_sc)
    return jnp.concatenate([y_a, y_b], axis=-1)


def nsa_compress(t, pos, w1, w2):
    B, G, S, D = t.shape
    r = CMP_LEN // CMP_STRIDE
    nc = S // CMP_STRIDE - r + 1
    tr = t.reshape(B, G, S // CMP_STRIDE, CMP_STRIDE, D)
    blocks = jnp.concatenate([tr[:, :, j:j + nc] for j in range(r)], axis=3)
    blocks = (blocks + pos).reshape(B, G, nc, CMP_LEN * D)
    return jax.nn.gelu(blocks @ w1) @ w2


def nsa_compressed_branch(q, kc, vc):
    S = q.shape[3]
    nc = kc.shape[2]
    t = jnp.arange(S)
    valid = (jnp.arange(nc) * CMP_STRIDE + CMP_LEN - 1)[None, :] <= t[:, None]
    s = jnp.einsum('bgrsd,bgcd->bgrsc', q, kc)
    p = jnp.where(valid, jax.nn.softmax(jnp.where(valid, s, NEG), axis=-1), 0.0)
    return jnp.einsum('bgrsc,bgcd->bgrsd', p, vc), p


def nsa_selected_branch(q, k, v, p_cmp):
    B, G, R, S, D = q.shape
    nc = p_cmp.shape[-1]
    ns = S // SEL_BLOCK
    n_sel = min(SEL_N, ns)
    cs = jnp.arange(nc) * CMP_STRIDE
    ss = jnp.arange(ns) * SEL_BLOCK
    ov = jnp.minimum(cs[:, None] + CMP_LEN, ss[None, :] + SEL_BLOCK) - jnp.maximum(cs[:, None], ss[None, :])
    M = (jnp.clip(ov, 0, None) / CMP_STRIDE).astype(jnp.float32)
    score = jnp.einsum('bgrsc,cn->bgsn', p_cmp, M)
    t = jnp.arange(S)
    blk = jnp.arange(ns)
    cur = (t // SEL_BLOCK)[:, None]
    valid = blk[None, :] * SEL_BLOCK <= t[:, None]
    forced = (blk[None, :] == 0) | (blk[None, :] == cur) | (blk[None, :] == cur - 1)
    score = jnp.where(valid, score + jnp.where(forced, FORCE_BONUS, 0.0), NEG)
    _, idx = lax.top_k(score, n_sel)
    sel = jnp.sum(jax.nn.one_hot(idx, ns, dtype=jnp.float32), axis=-2) > 0
    s = jnp.einsum('bgrsd,bgkd->bgrsk', q, k)
    mask = jnp.repeat(sel, SEL_BLOCK, axis=-1) & (t[None, :] <= t[:, None])
    p = jax.nn.softmax(jnp.where(mask[:, :, None], s, NEG), axis=-1)
    return jnp.einsum('bgrsk,bgkd->bgrsd', p, v)


def nsa_window_branch(q, k, v):
    S = q.shape[3]
    t = jnp.arange(S)
    mask = (t[None, :] <= t[:, None]) & (t[None, :] > t[:, None] - WIN)
    s = jnp.einsum('bgrsd,bgkd->bgrsk', q, k)
    p = jax.nn.softmax(jnp.where(mask, s, NEG), axis=-1)
    return jnp.einsum('bgrsk,bgkd->bgrsd', p, v)


def nsa_core(y, pos_k, w1_k, w2_k, pos_v, w1_v, w2_v, cos, sin):
    B, S, _ = y.shape
    H, G, D = NSA_HEADS, NSA_KV, NSA_HD
    R = H // G
    f32 = jnp.float32
    q, kc, vc, ks, vs, kw, vw, gt = split_cols(y, [H * D] + [G * D] * 6 + [3 * H])
    q = apply_rope(q.reshape(B, S, H, D), cos, sin) * (D ** -0.5)
    q = q.reshape(B, S, G, R, D).transpose(0, 2, 3, 1, 4)
    rk = lambda t: apply_rope(t.reshape(B, S, G, D), cos, sin).transpose(0, 2, 1, 3)
    rv = lambda t: t.reshape(B, S, G, D).transpose(0, 2, 1, 3)
    kc, ks, kw = rk(kc), rk(ks), rk(kw)
    vc, vs, vw = rv(vc), rv(vs), rv(vw)
    gates = jax.nn.sigmoid(gt).reshape(B, S, 3, G, R).transpose(2, 0, 3, 4, 1)[..., None]
    kcmp = nsa_compress(kc, pos_k.astype(f32), w1_k.astype(f32), w2_k.astype(f32))
    vcmp = nsa_compress(vc, pos_v.astype(f32), w1_v.astype(f32), w2_v.astype(f32))
    o_cmp, p_cmp = nsa_compressed_branch(q, kcmp, vcmp)
    o_slc = nsa_selected_branch(q, ks, vs, p_cmp)
    o_win = nsa_window_branch(q, kw, vw)
    o = gates[0] * o_cmp + gates[1] * o_slc + gates[2] * o_win
    return o.transpose(0, 3, 1, 2, 4).reshape(B, S, H * D)


def kernel(x, norm_w, ab_w_in, ab_w_out, gdn_conv, gdn_a_log, gdn_dt_bias, gdn_norm, sc_conv, nsa_w_in, nsa_w_out, cmp_pos_k, cmp_w1_k, cmp_w2_k, cmp_pos_v, cmp_w1_v, cmp_w2_v, ffn_w_gu, ffn_conv, ffn_w_down):
    B, S, D = x.shape
    depth = norm_w.shape[0]
    cos, sin = rope_tables(S, NSA_HD)
    h = x.reshape(B * S, D)

    def padded(w):
        n = w.shape[1]
        return jnp.pad(w.astype(BF16), ((0, 0), (0, -n % 1024)))

    for l in range(depth):
        i = l // 2
        if l % 2 == 0:
            n = ab_w_in.shape[2]
            y = norm_matmul(h, norm_w[l, 0], padded(ab_w_in[i]), F32)[:, :n]
            o = delta_shortconv_core(y.reshape(B, S, -1), gdn_conv[i], gdn_a_log[i], gdn_dt_bias[i],
                                     gdn_norm[i], sc_conv[i])
            w_out = ab_w_out[i]
        else:
            n = nsa_w_in.shape[2]
            y = norm_matmul(h, norm_w[l, 0], padded(nsa_w_in[i]), F32)[:, :n]
            o = nsa_core(y.reshape(B, S, -1), cmp_pos_k[i], cmp_w1_k[i], cmp_w2_k[i],
                         cmp_pos_v[i], cmp_w1_v[i], cmp_w2_v[i], cos, sin)
            w_out = nsa_w_out[i]
        h = out_proj([o.reshape(B * S, D).astype(BF16)], w_out.astype(BF16), h, norm_w[l, 1], concat=False)
        h = conv_ffn_block(h, S, norm_w[l, 2], ffn_w_gu[l].astype(BF16), ffn_conv[l],
                           ffn_w_down[l].astype(BF16), norm_w[l, 3])
    return h.reshape(B, S, D)
```

```python
import functools
import math

import jax
import jax.numpy as jnp
import numpy as np
from jax import lax
from jax.experimental import pallas as pl
from jax.experimental.pallas import tpu as pltpu

F32 = jnp.float32
BF16 = jnp.bfloat16

EPS = 1e-6
NEG = -1e30

GDN_HEADS = 8
GDN_HD = 128
GDN_CHUNK = 64
NSA_HEADS = 16
NSA_KV = 4
NSA_HD = 128
CMP_LEN = 32
CMP_STRIDE = 16
SEL_BLOCK = 64
SEL_N = 16
WIN = 512
FORCE_BONUS = 1000.0
ROPE_THETA = 10000.0

VMEM_LIMIT_BYTES = 56 * 1024 * 1024
BF16_ROWS = 16


def _cparams(*sem):
    return pltpu.CompilerParams(dimension_semantics=sem, vmem_limit_bytes=VMEM_LIMIT_BYTES)


def _rms(x, w):
    return x * lax.rsqrt(jnp.mean(x * x, axis=-1, keepdims=True) + EPS) * w


def _norm_mm_kernel(h_ref, nw_ref, w_ref, o_ref, xn_ref):
    @pl.when(pl.program_id(1) == 0)
    def _():
        xn_ref[...] = _rms(h_ref[...], nw_ref[...]).astype(BF16)

    o_ref[...] = jnp.dot(xn_ref[...], w_ref[...], preferred_element_type=F32).astype(o_ref.dtype)


def norm_matmul(h, nw, w, out_dtype, tm=512, tn=1024):
    M, D = h.shape
    N = w.shape[1]
    tn = min(tn, N)
    return pl.pallas_call(
        _norm_mm_kernel,
        grid=(M // tm, N // tn),
        in_specs=[pl.BlockSpec((tm, D), lambda i, j: (i, 0)),
                  pl.BlockSpec((1, D), lambda i, j: (0, 0)),
                  pl.BlockSpec((D, tn), lambda i, j: (0, j))],
        out_specs=pl.BlockSpec((tm, tn), lambda i, j: (i, j)),
        out_shape=jax.ShapeDtypeStruct((M, N), out_dtype),
        scratch_shapes=[pltpu.VMEM((tm, D), BF16)],
        compiler_params=_cparams("parallel", "arbitrary"),
        name="norm_matmul",
    )(h, nw.reshape(1, D), w)


def _out_proj_kernel(*refs, n_parts, concat):
    parts = refs[:n_parts]
    w_ref, h_ref, nw_ref, o_ref = refs[n_parts:]
    if concat:
        x = jnp.concatenate([p[...] for p in parts], axis=1)
    else:
        x = parts[0][...].astype(F32)
        for p in parts[1:]:
            x = x + p[...].astype(F32)
        x = x.astype(BF16)
    m = jnp.dot(x, w_ref[...], preferred_element_type=F32)
    o_ref[...] = h_ref[...] + _rms(m, nw_ref[...])


def out_proj(parts, w, h, nw, concat, tm=512):
    M, D = h.shape
    K = w.shape[0]
    in_specs = [pl.BlockSpec((tm, p.shape[1]), lambda i: (i, 0)) for p in parts]
    in_specs += [pl.BlockSpec((K, D), lambda i: (0, 0)),
                 pl.BlockSpec((tm, D), lambda i: (i, 0)),
                 pl.BlockSpec((1, D), lambda i: (0, 0))]
    return pl.pallas_call(
        functools.partial(_out_proj_kernel, n_parts=len(parts), concat=concat),
        grid=(M // tm,),
        in_specs=in_specs,
        out_specs=pl.BlockSpec((tm, D), lambda i: (i, 0)),
        out_shape=jax.ShapeDtypeStruct((M, D), F32),
        compiler_params=_cparams("parallel"),
        name="out_proj",
    )(*parts, w, h, nw.reshape(1, D))


def _ffn_kernel(h_ref, halo_ref, nw2_ref, wg_ref, wu_ref, cw_ref, wd_ref, nw3_ref, o_ref,
                xn_ref, g_ref, acc_ref, *, tiles_per_seq):
    i = pl.program_id(0)
    j = pl.program_id(1)
    tm = h_ref.shape[0]
    H = BF16_ROWS

    @pl.when(j == 0)
    def _():
        xn_ref[pl.ds(H, tm), :] = _rms(h_ref[...], nw2_ref[...]).astype(BF16)
        halo = _rms(halo_ref[...], nw2_ref[...])
        halo = jnp.where(i % tiles_per_seq == 0, 0.0, halo)
        xn_ref[pl.ds(0, H), :] = halo.astype(BF16)
        acc_ref[...] = jnp.zeros_like(acc_ref)

    g_ref[...] = jnp.dot(xn_ref[...], wg_ref[...], preferred_element_type=F32)
    u = jnp.dot(xn_ref[pl.ds(H, tm), :], wu_ref[...], preferred_element_type=F32)
    cw = cw_ref[...]
    c = (g_ref[pl.ds(H, tm), :] * cw[2:3, :] + g_ref[pl.ds(H - 1, tm), :] * cw[1:2, :]
         + g_ref[pl.ds(H - 2, tm), :] * cw[0:1, :])
    hid = (c * jax.nn.sigmoid(c) * u).astype(BF16)
    acc_ref[...] += jnp.dot(hid, wd_ref[...], preferred_element_type=F32)

    @pl.when(j == pl.num_programs(1) - 1)
    def _():
        o_ref[...] = h_ref[...] + _rms(acc_ref[...], nw3_ref[...])


def conv_ffn_block(h, seq_len, nw2, w_gu, conv_w, w_down, nw3, tm=512, tf=512):
    M, D = h.shape
    FF = w_down.shape[0]
    nf = FF // tf
    H = BF16_ROWS
    kern = functools.partial(_ffn_kernel, tiles_per_seq=seq_len // tm)
    return pl.pallas_call(
        kern,
        grid=(M // tm, nf),
        in_specs=[pl.BlockSpec((tm, D), lambda i, j: (i, 0)),
                  pl.BlockSpec((H, D), lambda i, j: (jnp.maximum(i * (tm // H) - 1, 0), 0)),
                  pl.BlockSpec((1, D), lambda i, j: (0, 0)),
                  pl.BlockSpec((D, tf), lambda i, j: (0, j)),
                  pl.BlockSpec((D, tf), lambda i, j: (0, nf + j)),
                  pl.BlockSpec((3, tf), lambda i, j: (0, j)),
                  pl.BlockSpec((tf, D), lambda i, j: (j, 0)),
                  pl.BlockSpec((1, D), lambda i, j: (0, 0))],
        out_specs=pl.BlockSpec((tm, D), lambda i, j: (i, 0)),
        out_shape=jax.ShapeDtypeStruct((M, D), F32),
        scratch_shapes=[pltpu.VMEM((tm + H, D), BF16),
                        pltpu.VMEM((tm + H, tf), F32),
                        pltpu.VMEM((tm, D), F32)],
        compiler_params=_cparams("parallel", "arbitrary"),
        name="conv_ffn",
    )(h, h, nw2.reshape(1, D), w_gu, w_gu, conv_w, w_down, nw3.reshape(1, D))


CONV_HALO = 8


def _silu(x):
    return x * jax.nn.sigmoid(x)


def _causal_conv_ext(ext_ref, x, cw, first):
    T = x.shape[0]
    K = cw.shape[0]

    @pl.when(first)
    def _():
        ext_ref[pl.ds(0, CONV_HALO), :] = jnp.zeros((CONV_HALO, x.shape[1]), F32)

    @pl.when(jnp.logical_not(first))
    def _():
        ext_ref[pl.ds(0, CONV_HALO), :] = ext_ref[pl.ds(T, CONV_HALO), :]

    ext_ref[pl.ds(CONV_HALO, T), :] = x
    y = ext_ref[pl.ds(CONV_HALO, T), :] * cw[K - 1:K, :]
    for k in range(K - 1):
        y = y + ext_ref[pl.ds(CONV_HALO - (K - 1) + k, T), :] * cw[k:k + 1, :]
    return y


def _split_bf16(x):
    hi = x.astype(BF16)
    return hi, (x - hi.astype(F32)).astype(BF16)


def _dot_3pass(a, b):
    ah, al = _split_bf16(a)
    bh, bl = _split_bf16(b)
    d = functools.partial(jnp.dot, preferred_element_type=F32)
    return d(ah, bh) + (d(ah, bl) + d(al, bh))


def _tri_inverse(lm):
    C = lm.shape[0]
    ii = lax.broadcasted_iota(jnp.int32, (C, C), 0)
    jj = lax.broadcasted_iota(jnp.int32, (C, C), 1)
    x = jnp.where(ii == jj, 1.0, 0.0)
    s = 1
    while s < C:
        off = (ii // (2 * s) == jj // (2 * s)) & (ii // s != jj // s)
        coff = jnp.where(off, lm, 0.0)
        if s == 1:
            x = x - coff
        else:
            x = x - _dot_3pass(_dot_3pass(x, coff), x)
        s *= 2
    return x


def _gdn_kernel(q_ref, k_ref, v_ref, z_ref, ab_ref, cq_ref, ck_ref, cv_ref, alog_ref, dtb_ref, on_ref,
                o_ref, qe_ref, ke_ref, ve_ref, st_ref):
    h = pl.program_id(1)
    t = pl.program_id(2)
    T, D = q_ref.shape
    C = GDN_CHUNK
    first = t == 0

    @pl.when(first)
    def _():
        st_ref[...] = jnp.zeros_like(st_ref)

    q = _silu(_causal_conv_ext(qe_ref, q_ref[...].astype(F32), cq_ref[...], first))
    k = _silu(_causal_conv_ext(ke_ref, k_ref[...].astype(F32), ck_ref[...], first))
    v = _silu(_causal_conv_ext(ve_ref, v_ref[...].astype(F32), cv_ref[...], first))
    q = q * lax.rsqrt(jnp.sum(q * q, axis=-1, keepdims=True) + EPS) * (D ** -0.5)
    k = k * lax.rsqrt(jnp.sum(k * k, axis=-1, keepdims=True) + EPS)

    ab = ab_ref[...]
    lane = lax.broadcasted_iota(jnp.int32, ab.shape, 1)
    x = ab + dtb_ref[...]
    g_all = -jnp.exp(alog_ref[...]) * (jnp.maximum(x, 0.0) + jnp.log1p(jnp.exp(-jnp.abs(x))))
    g_all = jnp.where(lane < GDN_HEADS, g_all, 0.0)
    ti = lax.broadcasted_iota(jnp.int32, (T, T), 0)
    tj = lax.broadcasted_iota(jnp.int32, (T, T), 1)
    cum = jnp.where((ti >= tj) & (ti // C == tj // C), 1.0, 0.0)
    gc_all = jnp.dot(cum, g_all, preferred_element_type=F32, precision=lax.Precision.HIGHEST)
    gcol = jnp.sum(jnp.where(lane == h, gc_all, 0.0), axis=-1, keepdims=True)
    beta = jnp.sum(jnp.where(lane == GDN_HEADS + h, jax.nn.sigmoid(ab), 0.0), axis=-1, keepdims=True)
    sel = jnp.where(lax.broadcasted_iota(jnp.int32, (8, ab.shape[1]), 1) == h, 1.0, 0.0)
    grow = lax.dot_general(sel, gc_all, (((1,), (1,)), ((), ())), preferred_element_type=F32,
                           precision=lax.Precision.HIGHEST)[0:1, :]

    ii = lax.broadcasted_iota(jnp.int32, (C, C), 0)
    jj = lax.broadcasted_iota(jnp.int32, (C, C), 1)
    tril = ii >= jj
    state = st_ref[...]
    for c in range(T // C):
        r = slice(c * C, (c + 1) * C)
        qc, kc, vc, gc, bc = q[r], k[r], v[r], gcol[r], beta[r]
        kb = kc * bc
        vb = vc * bc
        decay = jnp.where(tril, jnp.exp(jnp.where(tril, gc - grow[:, r], 0.0)), 0.0)
        kcb = kc.astype(BF16)
        qk = lax.dot_general(jnp.concatenate([qc, kb], axis=0).astype(BF16), kcb,
                             (((1,), (1,)), ((), ())), preferred_element_type=F32)
        a_qk = qk[:C] * decay
        lm = jnp.where(ii > jj, qk[C:] * decay, 0.0)
        tm = _tri_inverse(lm)
        eg = jnp.exp(gc)
        uw = jnp.dot(tm.astype(BF16), jnp.concatenate([vb, kb * eg], axis=1).astype(BF16),
                     preferred_element_type=F32)
        u, w = uw[:, :D], uw[:, D:]
        ws = jnp.dot(jnp.concatenate([w, qc * eg], axis=0).astype(BF16), state.astype(BF16),
                     preferred_element_type=F32)
        vnew = u - ws[:C]
        vnb = vnew.astype(BF16)
        o = ws[C:] + jnp.dot(a_qk.astype(BF16), vnb, preferred_element_type=F32)
        glast = gc[C - 1:C, :]
        kdec = kc * jnp.exp(glast - gc)
        state = state * jnp.exp(glast) + jnp.dot(kdec.T.astype(BF16), vnb, preferred_element_type=F32)
        zc = z_ref[pl.ds(c * C, C), :].astype(F32)
        o_ref[pl.ds(c * C, C), :] = (_rms(o, on_ref[...]) * _silu(zc)).astype(o_ref.dtype)
    st_ref[...] = state


def gdn_heads(y, ab, conv_qkv, a_log, dt_bias, o_norm, batch, seq_len, T=256):
    H, D = GDN_HEADS, GDN_HD
    nT = seq_len // T
    lanes = ab.shape[1]
    pad = lambda p: jnp.pad(p.astype(F32), (0, lanes - p.shape[0])).reshape(1, lanes)
    row = lambda b, h, t: b * nT + t
    col = lambda off: (lambda b, h, t: (row(b, h, t), off + h))
    cw = lambda off: (lambda b, h, t: (0, off + h))
    const = lambda b, h, t: (0, 0)
    return pl.pallas_call(
        _gdn_kernel,
        grid=(batch, H, nT),
        in_specs=[pl.BlockSpec((T, D), col(0)), pl.BlockSpec((T, D), col(H)),
                  pl.BlockSpec((T, D), col(2 * H)), pl.BlockSpec((T, D), col(3 * H)),
                  pl.BlockSpec((T, lanes), lambda b, h, t: (row(b, h, t), 0)),
                  pl.BlockSpec((conv_qkv.shape[0], D), cw(0)),
                  pl.BlockSpec((conv_qkv.shape[0], D), cw(H)),
                  pl.BlockSpec((conv_qkv.shape[0], D), cw(2 * H)),
                  pl.BlockSpec((1, lanes), const), pl.BlockSpec((1, lanes), const),
                  pl.BlockSpec((1, D), const)],
        out_specs=pl.BlockSpec((T, D), col(0)),
        out_shape=jax.ShapeDtypeStruct((batch * seq_len, H * D), BF16),
        scratch_shapes=[pltpu.VMEM((T + CONV_HALO, D), F32), pltpu.VMEM((T + CONV_HALO, D), F32),
                        pltpu.VMEM((T + CONV_HALO, D), F32), pltpu.VMEM((D, D), F32)],
        compiler_params=_cparams("parallel", "parallel", "arbitrary"),
        name="gdn_heads",
    )(y, y, y, y, ab, conv_qkv, conv_qkv, conv_qkv, pad(a_log), pad(dt_bias), o_norm.reshape(1, D))


def _shortconv_kernel(hb_ref, gb_ref, gc_ref, cw_ref, o_ref, ext_ref):
    first = pl.program_id(2) == 0
    x = gc_ref[...].astype(F32) * hb_ref[...].astype(F32)
    y = _causal_conv_ext(ext_ref, x, cw_ref[...], first)
    o_ref[...] = (gb_ref[...].astype(F32) * y).astype(o_ref.dtype)


def short_conv(y, col_off, conv_sc, batch, seq_len, T=512, tc=512):
    W = conv_sc.shape[1]
    nT = seq_len // T
    nC = W // tc
    spec = lambda off: pl.BlockSpec((T, tc), lambda b, c, t: (b * nT + t, (col_off + off) // tc + c))
    return pl.pallas_call(
        _shortconv_kernel,
        grid=(batch, nC, nT),
        in_specs=[spec(0), spec(W), spec(2 * W),
                  pl.BlockSpec((conv_sc.shape[0], tc), lambda b, c, t: (0, c))],
        out_specs=pl.BlockSpec((T, tc), lambda b, c, t: (b * nT + t, c)),
        out_shape=jax.ShapeDtypeStruct((batch * seq_len, W), BF16),
        scratch_shapes=[pltpu.VMEM((T + CONV_HALO, tc), F32)],
        compiler_params=_cparams("parallel", "parallel", "arbitrary"),
        name="short_conv",
    )(y, y, y, conv_sc)


def split_cols(t, widths):
    offs = np.cumsum(widths)[:-1].tolist()
    return jnp.split(t, offs, axis=-1)


def rmsnorm(x, w):
    xf = x.astype(jnp.float32)
    y = xf * lax.rsqrt(jnp.mean(xf * xf, axis=-1, keepdims=True) + EPS)
    return (y * w.astype(jnp.float32)).astype(x.dtype)


def l2norm(x):
    return x * lax.rsqrt(jnp.sum(x * x, axis=-1, keepdims=True) + EPS)


def causal_dwconv(x, w):
    K = w.shape[0]
    S = x.shape[1]
    xp = jnp.pad(x, ((0, 0), (K - 1, 0), (0, 0)))
    return sum(xp[:, k:k + S] * w[k] for k in range(K))


def rope_tables(S, dim):
    inv = 1.0 / (ROPE_THETA ** (jnp.arange(0, dim, 2, dtype=jnp.float32) / dim))
    ang = jnp.arange(S, dtype=jnp.float32)[:, None] * inv[None, :]
    return jnp.cos(ang), jnp.sin(ang)


def apply_rope(x, cos, sin):
    x1, x2 = jnp.split(x, 2, axis=-1)
    c = cos[None, :, None, :]
    s = sin[None, :, None, :]
    return jnp.concatenate([x1 * c - x2 * s, x2 * c + x1 * s], axis=-1)


def gated_delta_chunked(q, k, v, g, beta):
    B, H, S, D = q.shape
    C = GDN_CHUNK
    N = S // C
    q = q * (D ** -0.5)
    rs = lambda t: t.reshape((B, H, N, C) + t.shape[3:])
    q, k, v, beta = rs(q), rs(k), rs(v), rs(beta)
    g = jnp.cumsum(rs(g), axis=-1)
    kb = k * beta[..., None]
    vb = v * beta[..., None]
    tril = jnp.tril(jnp.ones((C, C), dtype=bool))
    stril = jnp.tril(jnp.ones((C, C), dtype=bool), -1)
    gdiff = g[..., :, None] - g[..., None, :]
    decay = jnp.where(tril, jnp.exp(jnp.where(tril, gdiff, 0.0)), 0.0)
    Lm = jnp.where(stril, jnp.einsum('bhnid,bhnjd->bhnij', kb, k) * decay, 0.0)
    eye = jnp.eye(C, dtype=jnp.float32)
    T = lax.linalg.triangular_solve(eye + Lm, jnp.broadcast_to(eye, Lm.shape),
                                    left_side=True, lower=True)
    u = T @ vb
    w = T @ (kb * jnp.exp(g)[..., None])
    qk = jnp.where(tril, jnp.einsum('bhnid,bhnjd->bhnij', q, k) * decay, 0.0)

    def step(state, xs):
        qc, kc, uc, wc, gc, ac = xs
        vnew = uc - wc @ state
        o = (qc * jnp.exp(gc)[..., None]) @ state + ac @ vnew
        glast = gc[..., -1]
        kdec = kc * jnp.exp(glast[..., None] - gc)[..., None]
        state = state * jnp.exp(glast)[..., None, None] + jnp.einsum('bhcd,bhce->bhde', kdec, vnew)
        return state, o

    xs = tuple(jnp.moveaxis(t, 2, 0) for t in (q, k, u, w, g, qk))
    s0 = jnp.zeros((B, H, D, D), jnp.float32)
    _, o = lax.scan(step, s0, xs)
    return jnp.moveaxis(o, 0, 2).reshape(B, H, S, D)


def delta_shortconv_core(y, conv_qkv, a_log, dt_bias, o_norm, conv_sc):
    B, S, _ = y.shape
    H, D = GDN_HEADS, GDN_HD
    SCW = H * D
    f32 = jnp.float32
    q, k, v, z, a, b, hb, gb, gc = split_cols(y, [H * D] * 4 + [H] * 2 + [SCW] * 3)
    qkv = jax.nn.silu(causal_dwconv(jnp.concatenate([q, k, v], axis=-1), conv_qkv)).astype(f32)
    q, k, v = [t.reshape(B, S, H, D).transpose(0, 2, 1, 3) for t in jnp.split(qkv, 3, axis=-1)]
    g = -jnp.exp(a_log.astype(f32)) * jax.nn.softplus(a.astype(f32) + dt_bias.astype(f32))
    beta = jax.nn.sigmoid(b.astype(f32))
    o = gated_delta_chunked(l2norm(q), l2norm(k), v, g.transpose(0, 2, 1), beta.transpose(0, 2, 1))
    o = rmsnorm(o.transpose(0, 2, 1, 3), o_norm) * jax.nn.silu(z.astype(f32).reshape(B, S, H, D))
    y_a = o.reshape(B, S, H * D)
    y_b = gb * causal_dwconv(gc * hb, conv_sc)
    return jnp.concatenate([y_a, y_b], axis=-1)


def nsa_compress(t, pos, w1, w2):
    B, G, S, D = t.shape
    r = CMP_LEN // CMP_STRIDE
    nc = S // CMP_STRIDE - r + 1
    tr = t.reshape(B, G, S // CMP_STRIDE, CMP_STRIDE, D)
    blocks = jnp.concatenate([tr[:, :, j:j + nc] for j in range(r)], axis=3)
    blocks = (blocks + pos).reshape(B, G, nc, CMP_LEN * D)
    return jax.nn.gelu(blocks @ w1) @ w2


def nsa_compressed_branch(q, kc, vc):
    S = q.shape[3]
    nc = kc.shape[2]
    t = jnp.arange(S)
    valid = (jnp.arange(nc) * CMP_STRIDE + CMP_LEN - 1)[None, :] <= t[:, None]
    s = jnp.einsum('bgrsd,bgcd->bgrsc', q, kc)
    p = jnp.where(valid, jax.nn.softmax(jnp.where(valid, s, NEG), axis=-1), 0.0)
    return jnp.einsum('bgrsc,bgcd->bgrsd', p, vc), p


def nsa_selected_branch(q, k, v, p_cmp):
    B, G, R, S, D = q.shape
    nc = p_cmp.shape[-1]
    ns = S // SEL_BLOCK
    n_sel = min(SEL_N, ns)
    cs = jnp.arange(nc) * CMP_STRIDE
    ss = jnp.arange(ns) * SEL_BLOCK
    ov = jnp.minimum(cs[:, None] + CMP_LEN, ss[None, :] + SEL_BLOCK) - jnp.maximum(cs[:, None], ss[None, :])
    M = (jnp.clip(ov, 0, None) / CMP_STRIDE).astype(jnp.float32)
    score = jnp.einsum('bgrsc,cn->bgsn', p_cmp, M)
    t = jnp.arange(S)
    blk = jnp.arange(ns)
    cur = (t // SEL_BLOCK)[:, None]
    valid = blk[None, :] * SEL_BLOCK <= t[:, None]
    forced = (blk[None, :] == 0) | (blk[None, :] == cur) | (blk[None, :] == cur - 1)
    score = jnp.where(valid, score + jnp.where(forced, FORCE_BONUS, 0.0), NEG)
    _, idx = lax.top_k(score, n_sel)
    sel = jnp.sum(jax.nn.one_hot(idx, ns, dtype=jnp.float32), axis=-2) > 0
    s = jnp.einsum('bgrsd,bgkd->bgrsk', q, k)
    mask = jnp.repeat(sel, SEL_BLOCK, axis=-1) & (t[None, :] <= t[:, None])
    p = jax.nn.softmax(jnp.where(mask[:, :, None], s, NEG), axis=-1)
    return jnp.einsum('bgrsk,bgkd->bgrsd', p, v)


def nsa_window_branch(q, k, v):
    S = q.shape[3]
    t = jnp.arange(S)
    mask = (t[None, :] <= t[:, None]) & (t[None, :] > t[:, None] - WIN)
    s = jnp.einsum('bgrsd,bgkd->bgrsk', q, k)
    p = jax.nn.softmax(jnp.where(mask, s, NEG), axis=-1)
    return jnp.einsum('bgrsk,bgkd->bgrsd', p, v)


def nsa_core(y, pos_k, w1_k, w2_k, pos_v, w1_v, w2_v, cos, sin):
    B, S, _ = y.shape
    H, G, D = NSA_HEADS, NSA_KV, NSA_HD
    R = H // G
    f32 = jnp.float32
    q, kc, vc, ks, vs, kw, vw, gt = split_cols(y, [H * D] + [G * D] * 6 + [3 * H])
    q = apply_rope(q.reshape(B, S, H, D), cos, sin) * (D ** -0.5)
    q = q.reshape(B, S, G, R, D).transpose(0, 2, 3, 1, 4)
    rk = lambda t: apply_rope(t.reshape(B, S, G, D), cos, sin).transpose(0, 2, 1, 3)
    rv = lambda t: t.reshape(B, S, G, D).transpose(0, 2, 1, 3)
    kc, ks, kw = rk(kc), rk(ks), rk(kw)
    vc, vs, vw = rv(vc), rv(vs), rv(vw)
    gates = jax.nn.sigmoid(gt).reshape(B, S, 3, G, R).transpose(2, 0, 3, 4, 1)[..., None]
    kcmp = nsa_compress(kc, pos_k.astype(f32), w1_k.astype(f32), w2_k.astype(f32))
    vcmp = nsa_compress(vc, pos_v.astype(f32), w1_v.astype(f32), w2_v.astype(f32))
    o_cmp, p_cmp = nsa_compressed_branch(q, kcmp, vcmp)
    o_slc = nsa_selected_branch(q, ks, vs, p_cmp)
    o_win = nsa_window_branch(q, kw, vw)
    o = gates[0] * o_cmp + gates[1] * o_slc + gates[2] * o_win
    return o.transpose(0, 3, 1, 2, 4).reshape(B, S, H * D)


def kernel(x, norm_w, ab_w_in, ab_w_out, gdn_conv, gdn_a_log, gdn_dt_bias, gdn_norm, sc_conv, nsa_w_in, nsa_w_out, cmp_pos_k, cmp_w1_k, cmp_w2_k, cmp_pos_v, cmp_w1_v, cmp_w2_v, ffn_w_gu, ffn_conv, ffn_w_down):
    B, S, D = x.shape
    p = dict(norm_w=norm_w, ab_w_in=ab_w_in, ab_w_out=ab_w_out, gdn_conv=gdn_conv, gdn_a_log=gdn_a_log,
             gdn_dt_bias=gdn_dt_bias, gdn_norm=gdn_norm, sc_conv=sc_conv, nsa_w_in=nsa_w_in,
             nsa_w_out=nsa_w_out, cmp_pos_k=cmp_pos_k, cmp_w1_k=cmp_w1_k, cmp_w2_k=cmp_w2_k,
             cmp_pos_v=cmp_pos_v, cmp_w1_v=cmp_w1_v, cmp_w2_v=cmp_w2_v, ffn_w_gu=ffn_w_gu,
             ffn_conv=ffn_conv, ffn_w_down=ffn_w_down)
    cos, sin = rope_tables(S, NSA_HD)
    h = x.reshape(B * S, D)
    for l in range(norm_w.shape[0]):
        _, h = run_layer(h, l, B, S, cos, sin, p)
    return h.reshape(B, S, D)


def run_layer(h, l, B, S, cos, sin, p):
    D = h.shape[1]
    i = l // 2
    norm_w = p["norm_w"]
    if l % 2 == 0:
        gw = GDN_HEADS * GDN_HD
        w = p["ab_w_in"][i]
        w_main = jnp.concatenate([w[:, :4 * gw], w[:, 4 * gw + 2 * GDN_HEADS:]], axis=1).astype(BF16)
        w_ab = jnp.pad(w[:, 4 * gw:4 * gw + 2 * GDN_HEADS], ((0, 0), (0, 128 - 2 * GDN_HEADS))).astype(BF16)
        y = norm_matmul(h, norm_w[l, 0], w_main, BF16)
        ab = norm_matmul(h, norm_w[l, 0], w_ab, F32)
        y_a = gdn_heads(y, ab, p["gdn_conv"][i], p["gdn_a_log"][i], p["gdn_dt_bias"][i], p["gdn_norm"][i], B, S)
        y_b = short_conv(y, 4 * gw, p["sc_conv"][i], B, S)
        h1 = out_proj([y_a, y_b], p["ab_w_out"][i].astype(BF16), h, norm_w[l, 1], concat=True)
    else:
        w = p["nsa_w_in"][i]
        n = w.shape[1]
        wp = jnp.pad(w.astype(BF16), ((0, 0), (0, -n % 1024)))
        y = norm_matmul(h, norm_w[l, 0], wp, F32)[:, :n]
        o = nsa_core(y.reshape(B, S, -1), p["cmp_pos_k"][i], p["cmp_w1_k"][i], p["cmp_w2_k"][i],
                     p["cmp_pos_v"][i], p["cmp_w1_v"][i], p["cmp_w2_v"][i], cos, sin)
        h1 = out_proj([o.reshape(B * S, D).astype(BF16)], p["nsa_w_out"][i].astype(BF16), h, norm_w[l, 1],
                      concat=False)
    h2 = conv_ffn_block(h1, S, norm_w[l, 2], p["ffn_w_gu"][l].astype(BF16), p["ffn_conv"][l],
                        p["ffn_w_down"][l].astype(BF16), norm_w[l, 3])
    return h1, h2
```

```python
import functools
import math

import jax
import jax.numpy as jnp
import numpy as np
from jax import lax
from jax.experimental import pallas as pl
from jax.experimental.pallas import tpu as pltpu

F32 = jnp.float32
BF16 = jnp.bfloat16

EPS = 1e-6
NEG = -1e30

GDN_HEADS = 8
GDN_HD = 128
GDN_CHUNK = 64
NSA_HEADS = 16
NSA_KV = 4
NSA_HD = 128
CMP_LEN = 32
CMP_STRIDE = 16
SEL_BLOCK = 64
SEL_N = 16
WIN = 512
FORCE_BONUS = 1000.0
ROPE_THETA = 10000.0

VMEM_LIMIT_BYTES = 56 * 1024 * 1024
BF16_ROWS = 16


def _cparams(*sem):
    return pltpu.CompilerParams(dimension_semantics=sem, vmem_limit_bytes=VMEM_LIMIT_BYTES)


def _rms(x, w):
    return x * lax.rsqrt(jnp.mean(x * x, axis=-1, keepdims=True) + EPS) * w


def _norm_mm_kernel(h_ref, nw_ref, w_ref, o_ref, xn_ref):
    @pl.when(pl.program_id(1) == 0)
    def _():
        xn_ref[...] = _rms(h_ref[...], nw_ref[...]).astype(BF16)

    o_ref[...] = jnp.dot(xn_ref[...], w_ref[...], preferred_element_type=F32).astype(o_ref.dtype)


def norm_matmul(h, nw, w, out_dtype, tm=512, tn=1024):
    M, D = h.shape
    N = w.shape[1]
    tn = min(tn, N)
    return pl.pallas_call(
        _norm_mm_kernel,
        grid=(M // tm, N // tn),
        in_specs=[pl.BlockSpec((tm, D), lambda i, j: (i, 0)),
                  pl.BlockSpec((1, D), lambda i, j: (0, 0)),
                  pl.BlockSpec((D, tn), lambda i, j: (0, j))],
        out_specs=pl.BlockSpec((tm, tn), lambda i, j: (i, j)),
        out_shape=jax.ShapeDtypeStruct((M, N), out_dtype),
        scratch_shapes=[pltpu.VMEM((tm, D), BF16)],
        compiler_params=_cparams("parallel", "arbitrary"),
        name="norm_matmul",
    )(h, nw.reshape(1, D), w)


def _out_proj_kernel(*refs, n_parts, concat):
    parts = refs[:n_parts]
    w_ref, h_ref, nw_ref, o_ref = refs[n_parts:]
    if concat:
        x = jnp.concatenate([p[...] for p in parts], axis=1)
    else:
        x = parts[0][...].astype(F32)
        for p in parts[1:]:
            x = x + p[...].astype(F32)
        x = x.astype(BF16)
    m = jnp.dot(x, w_ref[...], preferred_element_type=F32)
    o_ref[...] = h_ref[...] + _rms(m, nw_ref[...])


def out_proj(parts, w, h, nw, concat, tm=512):
    M, D = h.shape
    K = w.shape[0]
    in_specs = [pl.BlockSpec((tm, p.shape[1]), lambda i: (i, 0)) for p in parts]
    in_specs += [pl.BlockSpec((K, D), lambda i: (0, 0)),
                 pl.BlockSpec((tm, D), lambda i: (i, 0)),
                 pl.BlockSpec((1, D), lambda i: (0, 0))]
    return pl.pallas_call(
        functools.partial(_out_proj_kernel, n_parts=len(parts), concat=concat),
        grid=(M // tm,),
        in_specs=in_specs,
        out_specs=pl.BlockSpec((tm, D), lambda i: (i, 0)),
        out_shape=jax.ShapeDtypeStruct((M, D), F32),
        compiler_params=_cparams("parallel"),
        name="out_proj",
    )(*parts, w, h, nw.reshape(1, D))


def _ffn_kernel(h_ref, halo_ref, nw2_ref, wg_ref, wu_ref, cw_ref, wd_ref, nw3_ref, o_ref,
                xn_ref, g_ref, acc_ref, *, tiles_per_seq):
    i = pl.program_id(0)
    j = pl.program_id(1)
    tm = h_ref.shape[0]
    H = BF16_ROWS

    @pl.when(j == 0)
    def _():
        xn_ref[pl.ds(H, tm), :] = _rms(h_ref[...], nw2_ref[...]).astype(BF16)
        halo = _rms(halo_ref[...], nw2_ref[...])
        halo = jnp.where(i % tiles_per_seq == 0, 0.0, halo)
        xn_ref[pl.ds(0, H), :] = halo.astype(BF16)
        acc_ref[...] = jnp.zeros_like(acc_ref)

    g_ref[...] = jnp.dot(xn_ref[...], wg_ref[...], preferred_element_type=F32)
    u = jnp.dot(xn_ref[pl.ds(H, tm), :], wu_ref[...], preferred_element_type=F32)
    cw = cw_ref[...]
    c = (g_ref[pl.ds(H, tm), :] * cw[2:3, :] + g_ref[pl.ds(H - 1, tm), :] * cw[1:2, :]
         + g_ref[pl.ds(H - 2, tm), :] * cw[0:1, :])
    hid = (c * jax.nn.sigmoid(c) * u).astype(BF16)
    acc_ref[...] += jnp.dot(hid, wd_ref[...], preferred_element_type=F32)

    @pl.when(j == pl.num_programs(1) - 1)
    def _():
        o_ref[...] = h_ref[...] + _rms(acc_ref[...], nw3_ref[...])


def conv_ffn_block(h, seq_len, nw2, w_gu, conv_w, w_down, nw3, tm=512, tf=512):
    M, D = h.shape
    FF = w_down.shape[0]
    nf = FF // tf
    H = BF16_ROWS
    kern = functools.partial(_ffn_kernel, tiles_per_seq=seq_len // tm)
    return pl.pallas_call(
        kern,
        grid=(M // tm, nf),
        in_specs=[pl.BlockSpec((tm, D), lambda i, j: (i, 0)),
                  pl.BlockSpec((H, D), lambda i, j: (jnp.maximum(i * (tm // H) - 1, 0), 0)),
                  pl.BlockSpec((1, D), lambda i, j: (0, 0)),
                  pl.BlockSpec((D, tf), lambda i, j: (0, j)),
                  pl.BlockSpec((D, tf), lambda i, j: (0, nf + j)),
                  pl.BlockSpec((3, tf), lambda i, j: (0, j)),
                  pl.BlockSpec((tf, D), lambda i, j: (j, 0)),
                  pl.BlockSpec((1, D), lambda i, j: (0, 0))],
        out_specs=pl.BlockSpec((tm, D), lambda i, j: (i, 0)),
        out_shape=jax.ShapeDtypeStruct((M, D), F32),
        scratch_shapes=[pltpu.VMEM((tm + H, D), BF16),
                        pltpu.VMEM((tm + H, tf), F32),
                        pltpu.VMEM((tm, D), F32)],
        compiler_params=_cparams("parallel", "arbitrary"),
        name="conv_ffn",
    )(h, h, nw2.reshape(1, D), w_gu, w_gu, conv_w, w_down, nw3.reshape(1, D))


CONV_HALO = 8


def _silu(x):
    return x * jax.nn.sigmoid(x)


def _causal_conv_ext(ext_ref, x, cw, first):
    T = x.shape[0]
    K = cw.shape[0]

    @pl.when(first)
    def _():
        ext_ref[pl.ds(0, CONV_HALO), :] = jnp.zeros((CONV_HALO, x.shape[1]), F32)

    @pl.when(jnp.logical_not(first))
    def _():
        ext_ref[pl.ds(0, CONV_HALO), :] = ext_ref[pl.ds(T, CONV_HALO), :]

    ext_ref[pl.ds(CONV_HALO, T), :] = x
    y = ext_ref[pl.ds(CONV_HALO, T), :] * cw[K - 1:K, :]
    for k in range(K - 1):
        y = y + ext_ref[pl.ds(CONV_HALO - (K - 1) + k, T), :] * cw[k:k + 1, :]
    return y


def _split_bf16(x):
    hi = x.astype(BF16)
    return hi, (x - hi.astype(F32)).astype(BF16)


def _dot_3pass(a, b):
    ah, al = _split_bf16(a)
    bh, bl = _split_bf16(b)
    d = functools.partial(jnp.dot, preferred_element_type=F32)
    return d(ah, bh) + (d(ah, bl) + d(al, bh))


def _tri_inverse(lm):
    C = lm.shape[0]
    ii = lax.broadcasted_iota(jnp.int32, (C, C), 0)
    jj = lax.broadcasted_iota(jnp.int32, (C, C), 1)
    x = jnp.where(ii == jj, 1.0, 0.0)
    s = 1
    while s < C:
        off = (ii // (2 * s) == jj // (2 * s)) & (ii // s != jj // s)
        coff = jnp.where(off, lm, 0.0)
        if s == 1:
            x = x - coff
        else:
            x = x - _dot_3pass(_dot_3pass(x, coff), x)
        s *= 2
    return x


def _gdn_kernel(q_ref, k_ref, v_ref, z_ref, ab_ref, cq_ref, ck_ref, cv_ref, alog_ref, dtb_ref, on_ref,
                o_ref, qe_ref, ke_ref, ve_ref, st_ref):
    h = pl.program_id(1)
    t = pl.program_id(2)
    T, D = q_ref.shape
    C = GDN_CHUNK
    first = t == 0

    @pl.when(first)
    def _():
        st_ref[...] = jnp.zeros_like(st_ref)

    q = _silu(_causal_conv_ext(qe_ref, q_ref[...].astype(F32), cq_ref[...], first))
    k = _silu(_causal_conv_ext(ke_ref, k_ref[...].astype(F32), ck_ref[...], first))
    v = _silu(_causal_conv_ext(ve_ref, v_ref[...].astype(F32), cv_ref[...], first))
    q = q * lax.rsqrt(jnp.sum(q * q, axis=-1, keepdims=True) + EPS) * (D ** -0.5)
    k = k * lax.rsqrt(jnp.sum(k * k, axis=-1, keepdims=True) + EPS)

    ab = ab_ref[...]
    lane = lax.broadcasted_iota(jnp.int32, ab.shape, 1)
    x = ab + dtb_ref[...]
    g_all = -jnp.exp(alog_ref[...]) * (jnp.maximum(x, 0.0) + jnp.log1p(jnp.exp(-jnp.abs(x))))
    g_all = jnp.where(lane < GDN_HEADS, g_all, 0.0)
    ti = lax.broadcasted_iota(jnp.int32, (T, T), 0)
    tj = lax.broadcasted_iota(jnp.int32, (T, T), 1)
    cum = jnp.where((ti >= tj) & (ti // C == tj // C), 1.0, 0.0)
    gc_all = jnp.dot(cum, g_all, preferred_element_type=F32, precision=lax.Precision.HIGHEST)
    gcol = jnp.sum(jnp.where(lane == h, gc_all, 0.0), axis=-1, keepdims=True)
    beta = jnp.sum(jnp.where(lane == GDN_HEADS + h, jax.nn.sigmoid(ab), 0.0), axis=-1, keepdims=True)
    sel = jnp.where(lax.broadcasted_iota(jnp.int32, (8, ab.shape[1]), 1) == h, 1.0, 0.0)
    grow = lax.dot_general(sel, gc_all, (((1,), (1,)), ((), ())), preferred_element_type=F32,
                           precision=lax.Precision.HIGHEST)[0:1, :]

    ii = lax.broadcasted_iota(jnp.int32, (C, C), 0)
    jj = lax.broadcasted_iota(jnp.int32, (C, C), 1)
    tril = ii >= jj
    state = st_ref[...]
    for c in range(T // C):
        r = slice(c * C, (c + 1) * C)
        qc, kc, vc, gc, bc = q[r], k[r], v[r], gcol[r], beta[r]
        kb = kc * bc
        vb = vc * bc
        decay = jnp.where(tril, jnp.exp(jnp.where(tril, gc - grow[:, r], 0.0)), 0.0)
        kcb = kc.astype(BF16)
        qk = lax.dot_general(jnp.concatenate([qc, kb], axis=0).astype(BF16), kcb,
                             (((1,), (1,)), ((), ())), preferred_element_type=F32)
        a_qk = qk[:C] * decay
        lm = jnp.where(ii > jj, qk[C:] * decay, 0.0)
        tm = _tri_inverse(lm)
        eg = jnp.exp(gc)
        uw = jnp.dot(tm.astype(BF16), jnp.concatenate([vb, kb * eg], axis=1).astype(BF16),
                     preferred_element_type=F32)
        u, w = uw[:, :D], uw[:, D:]
        ws = jnp.dot(jnp.concatenate([w, qc * eg], axis=0).astype(BF16), state.astype(BF16),
                     preferred_element_type=F32)
        vnew = u - ws[:C]
        vnb = vnew.astype(BF16)
        o = ws[C:] + jnp.dot(a_qk.astype(BF16), vnb, preferred_element_type=F32)
        glast = gc[C - 1:C, :]
        kdec = kc * jnp.exp(glast - gc)
        state = state * jnp.exp(glast) + jnp.dot(kdec.T.astype(BF16), vnb, preferred_element_type=F32)
        zc = z_ref[pl.ds(c * C, C), :].astype(F32)
        o_ref[pl.ds(c * C, C), :] = (_rms(o, on_ref[...]) * _silu(zc)).astype(o_ref.dtype)
    st_ref[...] = state


def gdn_heads(y, ab, conv_qkv, a_log, dt_bias, o_norm, batch, seq_len, T=256):
    H, D = GDN_HEADS, GDN_HD
    nT = seq_len // T
    lanes = ab.shape[1]
    pad = lambda p: jnp.pad(p.astype(F32), (0, lanes - p.shape[0])).reshape(1, lanes)
    row = lambda b, h, t: b * nT + t
    col = lambda off: (lambda b, h, t: (row(b, h, t), off + h))
    cw = lambda off: (lambda b, h, t: (0, off + h))
    const = lambda b, h, t: (0, 0)
    return pl.pallas_call(
        _gdn_kernel,
        grid=(batch, H, nT),
        in_specs=[pl.BlockSpec((T, D), col(0)), pl.BlockSpec((T, D), col(H)),
                  pl.BlockSpec((T, D), col(2 * H)), pl.BlockSpec((T, D), col(3 * H)),
                  pl.BlockSpec((T, lanes), lambda b, h, t: (row(b, h, t), 0)),
                  pl.BlockSpec((conv_qkv.shape[0], D), cw(0)),
                  pl.BlockSpec((conv_qkv.shape[0], D), cw(H)),
                  pl.BlockSpec((conv_qkv.shape[0], D), cw(2 * H)),
                  pl.BlockSpec((1, lanes), const), pl.BlockSpec((1, lanes), const),
                  pl.BlockSpec((1, D), const)],
        out_specs=pl.BlockSpec((T, D), col(0)),
        out_shape=jax.ShapeDtypeStruct((batch * seq_len, H * D), BF16),
        scratch_shapes=[pltpu.VMEM((T + CONV_HALO, D), F32), pltpu.VMEM((T + CONV_HALO, D), F32),
                        pltpu.VMEM((T + CONV_HALO, D), F32), pltpu.VMEM((D, D), F32)],
        compiler_params=_cparams("parallel", "parallel", "arbitrary"),
        name="gdn_heads",
    )(y, y, y, y, ab, conv_qkv, conv_qkv, conv_qkv, pad(a_log), pad(dt_bias), o_norm.reshape(1, D))


def _shortconv_kernel(hb_ref, gb_ref, gc_ref, cw_ref, o_ref, ext_ref):
    first = pl.program_id(2) == 0
    x = gc_ref[...].astype(F32) * hb_ref[...].astype(F32)
    y = _causal_conv_ext(ext_ref, x, cw_ref[...], first)
    o_ref[...] = (gb_ref[...].astype(F32) * y).astype(o_ref.dtype)


def short_conv(y, col_off, conv_sc, batch, seq_len, T=512, tc=512):
    W = conv_sc.shape[1]
    nT = seq_len // T
    nC = W // tc
    spec = lambda off: pl.BlockSpec((T, tc), lambda b, c, t: (b * nT + t, (col_off + off) // tc + c))
    return pl.pallas_call(
        _shortconv_kernel,
        grid=(batch, nC, nT),
        in_specs=[spec(0), spec(W), spec(2 * W),
                  pl.BlockSpec((conv_sc.shape[0], tc), lambda b, c, t: (0, c))],
        out_specs=pl.BlockSpec((T, tc), lambda b, c, t: (b * nT + t, c)),
        out_shape=jax.ShapeDtypeStruct((batch * seq_len, W), BF16),
        scratch_shapes=[pltpu.VMEM((T + CONV_HALO, tc), F32)],
        compiler_params=_cparams("parallel", "parallel", "arbitrary"),
        name="short_conv",
    )(y, y, y, conv_sc)


def _nsa_prep_kernel(y_ref, cos_ref, sin_ref, q_ref, kc_ref, vc_ref, ks_ref, vs_ref, kw_ref, vw_ref):
    T = y_ref.shape[0]
    H, G, D = NSA_HEADS, NSA_KV, NSA_HD
    cos = cos_ref[...]
    sin = sin_ref[...]

    def head(c):
        return y_ref[:, c * D:(c + 1) * D]

    def rope(x):
        x = x.astype(F32)
        return x * cos + pltpu.roll(x, D // 2, 1) * sin

    for hh in range(H):
        q_ref[0, hh] = (rope(head(hh)) * (D ** -0.5)).astype(BF16)
    t0 = pl.program_id(1) * T
    tok = t0 + lax.broadcasted_iota(jnp.int32, (T, D), 0)
    lane = lax.broadcasted_iota(jnp.int32, (T, D), 1)
    onehot = jnp.where(tok // SEL_BLOCK == lane, 1.0, 0.0).astype(BF16)
    for g in range(G):
        kc_ref[0, g] = rope(head(H + g)).astype(BF16)
        vc_ref[0, g] = head(H + G + g)
        ks_ref[0, g, :, 0:D] = rope(head(H + 2 * G + g)).astype(BF16)
        ks_ref[0, g, :, D:2 * D] = onehot
        vs_ref[0, g] = head(H + 3 * G + g)
        kw_ref[0, g] = rope(head(H + 4 * G + g)).astype(BF16)
        vw_ref[0, g] = head(H + 5 * G + g)


def nsa_prep(y, cos2, sin2, batch, seq_len, T=512):
    H, G, D = NSA_HEADS, NSA_KV, NSA_HD
    nT = seq_len // T
    assert seq_len // SEL_BLOCK <= D
    grp = lambda w: jax.ShapeDtypeStruct((batch, G, seq_len, w), BF16)
    gspec = lambda w: pl.BlockSpec((1, G, T, w), lambda b, t: (b, 0, t, 0))
    return pl.pallas_call(
        _nsa_prep_kernel,
        grid=(batch, nT),
        in_specs=[pl.BlockSpec((T, y.shape[1]), lambda b, t: (b * nT + t, 0)),
                  pl.BlockSpec((T, D), lambda b, t: (t, 0)),
                  pl.BlockSpec((T, D), lambda b, t: (t, 0))],
        out_specs=[pl.BlockSpec((1, H, T, D), lambda b, t: (b, 0, t, 0)),
                   gspec(D), gspec(D), gspec(2 * D), gspec(D), gspec(D), gspec(D)],
        out_shape=[jax.ShapeDtypeStruct((batch, H, seq_len, D), BF16),
                   grp(D), grp(D), grp(2 * D), grp(D), grp(D), grp(D)],
        compiler_params=_cparams("parallel", "parallel"),
        name="nsa_prep",
    )(y, cos2, sin2)


def _gelu_tanh(x):
    return 0.5 * x * (1.0 + jnp.tanh(math.sqrt(2.0 / math.pi) * (x + 0.044715 * (x * x * x))))


def _compress_kernel(x_ref, pos_ref, w1_ref, w2_ref, o_ref):
    x = x_ref[0].astype(F32)
    half = x.shape[1]
    pos = pos_ref[...]
    top = jnp.dot((x + pos[:, :half]).astype(BF16), w1_ref[0:half, :], preferred_element_type=F32)
    bot = jnp.dot((x + pos[:, half:]).astype(BF16), w1_ref[half:2 * half, :], preferred_element_type=F32)
    n = x.shape[0]
    hid = top + pltpu.roll(bot, n - 1, 0)
    o_ref[0] = jnp.dot(_gelu_tanh(hid).astype(BF16), w2_ref[...], preferred_element_type=F32).astype(BF16)


def nsa_compress_blocks(t, pos, w1, w2):
    B, G, S, D = t.shape
    assert CMP_LEN == 2 * CMP_STRIDE
    n = S // CMP_STRIDE
    x = t.reshape(B * G, n, CMP_STRIDE * D)
    hidden = w1.shape[1]
    return pl.pallas_call(
        _compress_kernel,
        grid=(B * G,),
        in_specs=[pl.BlockSpec((1, n, CMP_STRIDE * D), lambda i: (i, 0, 0)),
                  pl.BlockSpec((1, CMP_LEN * D), lambda i: (0, 0)),
                  pl.BlockSpec((CMP_LEN * D, hidden), lambda i: (0, 0)),
                  pl.BlockSpec((hidden, D), lambda i: (0, 0))],
        out_specs=pl.BlockSpec((1, n, D), lambda i: (i, 0, 0)),
        out_shape=jax.ShapeDtypeStruct((B * G, n, D), BF16),
        compiler_params=_cparams("parallel"),
        name="nsa_compress",
    )(x, pos.reshape(1, CMP_LEN * D).astype(F32), w1.astype(BF16), w2.astype(BF16))


def _gate_col(sig, idx):
    lane = lax.broadcasted_iota(jnp.int32, sig.shape, 1)
    return jnp.sum(jnp.where(lane == idx, sig, 0.0), axis=-1, keepdims=True)


def _cmp_select_kernel(q_ref, kc_ref, vc_ref, gl_ref, ov_ref, o_ref, sb_ref):
    g = pl.program_id(1)
    i = pl.program_id(2)
    R, tq, D = q_ref.shape[1], q_ref.shape[2], q_ref.shape[3]
    nc = kc_ref.shape[1]
    ns = ov_ref.shape[0]
    kc = kc_ref[0]
    vc = vc_ref[0]
    sig = jax.nn.sigmoid(gl_ref[...])
    t_col = i * tq + lax.broadcasted_iota(jnp.int32, (tq, nc), 0)
    c_row = lax.broadcasted_iota(jnp.int32, (tq, nc), 1)
    valid = c_row * CMP_STRIDE + (CMP_LEN - 1) <= t_col
    psum = jnp.zeros((tq, nc), F32)
    for r in range(R):
        s = lax.dot_general(q_ref[0, r], kc, (((1,), (1,)), ((), ())), preferred_element_type=F32)
        s = jnp.where(valid, s, NEG)
        e = jnp.where(valid, jnp.exp(s - jnp.max(s, axis=-1, keepdims=True)), 0.0)
        den = jnp.sum(e, axis=-1, keepdims=True)
        p = e / jnp.where(den > 0.0, den, 1.0)
        psum = psum + p
        o = jnp.dot(p.astype(BF16), vc, preferred_element_type=F32)
        o_ref[:, r * D:(r + 1) * D] = (o * _gate_col(sig, g * R + r)).astype(o_ref.dtype)

    ov = ov_ref[...]
    nt = (((1,), (1,)), ((), ()))
    hi = psum.astype(BF16)
    r1 = psum - hi.astype(F32)
    mid = r1.astype(BF16)
    lo = (r1 - mid.astype(F32)).astype(BF16)
    score = (lax.dot_general(ov, hi, nt, preferred_element_type=F32)
             + (lax.dot_general(ov, mid, nt, preferred_element_type=F32)
                + lax.dot_general(ov, lo, nt, preferred_element_type=F32)))
    n_col = lax.broadcasted_iota(jnp.int32, (ns, tq), 0)
    t_row = i * tq + lax.broadcasted_iota(jnp.int32, (ns, tq), 1)
    cur = t_row // SEL_BLOCK
    forced = (n_col == 0) | (n_col == cur) | (n_col == cur - 1)
    score = jnp.where(n_col * SEL_BLOCK <= t_row, score + jnp.where(forced, FORCE_BONUS, 0.0), NEG)
    rank = jnp.zeros((ns, tq), F32)
    for m in range(ns):
        sm = score[m:m + 1, :]
        ahead = (sm > score) | ((sm == score) & (n_col > m))
        rank = rank + jnp.where(ahead, 1.0, 0.0)
    bias = jnp.where(rank < float(min(SEL_N, ns)), 0.0, NEG)
    bias = jnp.concatenate([bias, jnp.zeros((sb_ref.shape[3] - ns, tq), F32)], axis=0)
    sb_ref[0, 0] = bias.T.astype(sb_ref.dtype)


def _overlap_matrix(nc, ns):
    cs = np.arange(nc) * CMP_STRIDE
    ss = np.arange(ns) * SEL_BLOCK
    ov = np.minimum(cs[None, :] + CMP_LEN, ss[:, None] + SEL_BLOCK) - np.maximum(cs[None, :], ss[:, None])
    ov = np.clip(ov, 0, None) / CMP_STRIDE
    ov[:, nc - CMP_LEN // CMP_STRIDE + 1:] = 0.0
    return jnp.asarray(ov, BF16)


def nsa_cmp_select(q, kcmp, vcmp, gl, batch, seq_len, tq=256):
    H, G, D = NSA_HEADS, NSA_KV, NSA_HD
    R = H // G
    nq = seq_len // tq
    nc = kcmp.shape[1]
    ns = seq_len // SEL_BLOCK
    ov = _overlap_matrix(nc, ns)
    return pl.pallas_call(
        _cmp_select_kernel,
        grid=(batch, G, nq),
        in_specs=[pl.BlockSpec((1, R, tq, D), lambda b, g, i: (b, g, i, 0)),
                  pl.BlockSpec((1, nc, D), lambda b, g, i: (b * G + g, 0, 0)),
                  pl.BlockSpec((1, nc, D), lambda b, g, i: (b * G + g, 0, 0)),
                  pl.BlockSpec((tq, gl.shape[1]), lambda b, g, i: (b * nq + i, 0)),
                  pl.BlockSpec((ns, nc), lambda b, g, i: (0, 0))],
        out_specs=[pl.BlockSpec((tq, R * D), lambda b, g, i: (b * nq + i, g)),
                   pl.BlockSpec((1, 1, tq, D), lambda b, g, i: (b, g, i, 0))],
        out_shape=[jax.ShapeDtypeStruct((batch * seq_len, H * D), BF16),
                   jax.ShapeDtypeStruct((batch, G, seq_len, D), BF16)],
        compiler_params=_cparams("parallel", "parallel", "parallel"),
        name="nsa_cmp_select",
    )(q, kcmp, vcmp, gl, ov)


def _sel_win_kernel(q_ref, sb_ref, ks_ref, vs_ref, kw_ref, vw_ref, gl_ref, o_ref, m_ref, l_ref, acc_ref,
                    *, tk):
    g = pl.program_id(1)
    i = pl.program_id(2)
    R, tq, D = q_ref.shape[1], q_ref.shape[2], q_ref.shape[3]
    rows = R * tq
    nt = (((1,), (1,)), ((), ()))
    q = jnp.concatenate([q_ref[0, r] for r in range(R)], axis=0)
    sb = sb_ref[0, 0]
    q_aug = jnp.concatenate([q, jnp.concatenate([sb] * R, axis=0)], axis=1)
    t_rows = i * tq + (lax.broadcasted_iota(jnp.int32, (rows, 1), 0) & (tq - 1))

    m_ref[...] = jnp.full_like(m_ref, NEG)
    l_ref[...] = jnp.zeros_like(l_ref)
    acc_ref[...] = jnp.zeros_like(acc_ref)

    def tile(j, causal):
        k = ks_ref[0, 0, pl.ds(pl.multiple_of(j * tk, tk), tk), :]
        v = vs_ref[0, 0, pl.ds(pl.multiple_of(j * tk, tk), tk), :]
        s = lax.dot_general(q_aug, k, nt, preferred_element_type=F32)
        if causal:
            kpos = j * tk + lax.broadcasted_iota(jnp.int32, (rows, tk), 1)
            s = jnp.where(kpos <= t_rows, s, NEG)
        m_old = m_ref[...]
        m_new = jnp.maximum(m_old, jnp.max(s, axis=-1, keepdims=True))
        alpha = jnp.exp(m_old - m_new)
        p = jnp.exp(s - m_new)
        l_ref[...] = l_ref[...] * alpha + jnp.sum(p, axis=-1, keepdims=True)
        acc_ref[...] = acc_ref[...] * alpha + jnp.dot(p.astype(BF16), v, preferred_element_type=F32)
        m_ref[...] = m_new

    last = (i * tq) // tk

    def body(j, carry):
        tile(j, False)
        return carry

    lax.fori_loop(0, last, body, 0)
    tile(last, True)
    o_slc = acc_ref[...] / l_ref[...]

    wlen = WIN + tq
    start = pl.multiple_of(jnp.maximum(i * tq - WIN, 0), tq)
    kw = kw_ref[0, 0, pl.ds(start, wlen), :]
    vw = vw_ref[0, 0, pl.ds(start, wlen), :]
    s = lax.dot_general(q, kw, nt, preferred_element_type=F32)
    kpos = start + lax.broadcasted_iota(jnp.int32, (rows, wlen), 1)
    s = jnp.where((kpos <= t_rows) & (kpos > t_rows - WIN), s, NEG)
    p = jnp.exp(s - jnp.max(s, axis=-1, keepdims=True))
    o_win = jnp.dot(p.astype(BF16), vw, preferred_element_type=F32) / jnp.sum(p, axis=-1, keepdims=True)

    sig = jax.nn.sigmoid(gl_ref[...])
    H = NSA_HEADS
    for r in range(R):
        rs = slice(r * tq, (r + 1) * tq)
        o = (o_slc[rs] * _gate_col(sig, H + g * R + r) + o_win[rs] * _gate_col(sig, 2 * H + g * R + r))
        o_ref[:, r * D:(r + 1) * D] = o.astype(o_ref.dtype)


def nsa_selected_window(q, sb, ks_aug, vs, kw, vw, gl, batch, seq_len, tq=128, tk=512):
    H, G, D = NSA_HEADS, NSA_KV, NSA_HD
    R = H // G
    nq = seq_len // tq
    assert tk % tq == 0 and WIN % tq == 0 and tq & (tq - 1) == 0 and seq_len >= WIN + tq
    full = lambda w: pl.BlockSpec((1, 1, seq_len, w), lambda b, g, i: (b, g, 0, 0))
    return pl.pallas_call(
        functools.partial(_sel_win_kernel, tk=tk),
        grid=(batch, G, nq),
        in_specs=[pl.BlockSpec((1, R, tq, D), lambda b, g, i: (b, g, i, 0)),
                  pl.BlockSpec((1, 1, tq, D), lambda b, g, i: (b, g, i, 0)),
                  full(2 * D), full(D), full(D), full(D),
                  pl.BlockSpec((tq, gl.shape[1]), lambda b, g, i: (b * nq + i, 0))],
        out_specs=pl.BlockSpec((tq, R * D), lambda b, g, i: (b * nq + i, g)),
        out_shape=jax.ShapeDtypeStruct((batch * seq_len, H * D), BF16),
        scratch_shapes=[pltpu.VMEM((R * tq, 1), F32), pltpu.VMEM((R * tq, 1), F32),
                        pltpu.VMEM((R * tq, D), F32)],
        compiler_params=_cparams("parallel", "parallel", "arbitrary"),
        name="nsa_selected_window",
    )(q, sb, ks_aug, vs, kw, vw, gl)


def nsa_mixer_parts(y, gl, p, i, cos, sin, batch, seq_len):
    D = NSA_HD
    cos2 = jnp.concatenate([cos, cos], axis=1)
    sin2 = jnp.concatenate([-sin, sin], axis=1)
    q, kc, vc, ks_aug, vs, kw, vw = nsa_prep(y, cos2, sin2, batch, seq_len)
    kcmp = nsa_compress_blocks(kc, p["cmp_pos_k"][i], p["cmp_w1_k"][i], p["cmp_w2_k"][i])
    vcmp = nsa_compress_blocks(vc, p["cmp_pos_v"][i], p["cmp_w1_v"][i], p["cmp_w2_v"][i])
    o_cmp, sb = nsa_cmp_select(q, kcmp, vcmp, gl, batch, seq_len)
    o_sw = nsa_selected_window(q, sb, ks_aug, vs, kw, vw, gl, batch, seq_len)
    return o_cmp, o_sw


def split_cols(t, widths):
    offs = np.cumsum(widths)[:-1].tolist()
    return jnp.split(t, offs, axis=-1)


def rmsnorm(x, w):
    xf = x.astype(jnp.float32)
    y = xf * lax.rsqrt(jnp.mean(xf * xf, axis=-1, keepdims=True) + EPS)
    return (y * w.astype(jnp.float32)).astype(x.dtype)


def l2norm(x):
    return x * lax.rsqrt(jnp.sum(x * x, axis=-1, keepdims=True) + EPS)


def causal_dwconv(x, w):
    K = w.shape[0]
    S = x.shape[1]
    xp = jnp.pad(x, ((0, 0), (K - 1, 0), (0, 0)))
    return sum(xp[:, k:k + S] * w[k] for k in range(K))


def rope_tables(S, dim):
    inv = 1.0 / (ROPE_THETA ** (jnp.arange(0, dim, 2, dtype=jnp.float32) / dim))
    ang = jnp.arange(S, dtype=jnp.float32)[:, None] * inv[None, :]
    return jnp.cos(ang), jnp.sin(ang)


def apply_rope(x, cos, sin):
    x1, x2 = jnp.split(x, 2, axis=-1)
    c = cos[None, :, None, :]
    s = sin[None, :, None, :]
    return jnp.concatenate([x1 * c - x2 * s, x2 * c + x1 * s], axis=-1)


def gated_delta_chunked(q, k, v, g, beta):
    B, H, S, D = q.shape
    C = GDN_CHUNK
    N = S // C
    q = q * (D ** -0.5)
    rs = lambda t: t.reshape((B, H, N, C) + t.shape[3:])
    q, k, v, beta = rs(q), rs(k), rs(v), rs(beta)
    g = jnp.cumsum(rs(g), axis=-1)
    kb = k * beta[..., None]
    vb = v * beta[..., None]
    tril = jnp.tril(jnp.ones((C, C), dtype=bool))
    stril = jnp.tril(jnp.ones((C, C), dtype=bool), -1)
    gdiff = g[..., :, None] - g[..., None, :]
    decay = jnp.where(tril, jnp.exp(jnp.where(tril, gdiff, 0.0)), 0.0)
    Lm = jnp.where(stril, jnp.einsum('bhnid,bhnjd->bhnij', kb, k) * decay, 0.0)
    eye = jnp.eye(C, dtype=jnp.float32)
    T = lax.linalg.triangular_solve(eye + Lm, jnp.broadcast_to(eye, Lm.shape),
                                    left_side=True, lower=True)
    u = T @ vb
    w = T @ (kb * jnp.exp(g)[..., None])
    qk = jnp.where(tril, jnp.einsum('bhnid,bhnjd->bhnij', q, k) * decay, 0.0)

    def step(state, xs):
        qc, kc, uc, wc, gc, ac = xs
        vnew = uc - wc @ state
        o = (qc * jnp.exp(gc)[..., None]) @ state + ac @ vnew
        glast = gc[..., -1]
        kdec = kc * jnp.exp(glast[..., None] - gc)[..., None]
        state = state * jnp.exp(glast)[..., None, None] + jnp.einsum('bhcd,bhce->bhde', kdec, vnew)
        return state, o

    xs = tuple(jnp.moveaxis(t, 2, 0) for t in (q, k, u, w, g, qk))
    s0 = jnp.zeros((B, H, D, D), jnp.float32)
    _, o = lax.scan(step, s0, xs)
    return jnp.moveaxis(o, 0, 2).reshape(B, H, S, D)


def delta_shortconv_core(y, conv_qkv, a_log, dt_bias, o_norm, conv_sc):
    B, S, _ = y.shape
    H, D = GDN_HEADS, GDN_HD
    SCW = H * D
    f32 = jnp.float32
    q, k, v, z, a, b, hb, gb, gc = split_cols(y, [H * D] * 4 + [H] * 2 + [SCW] * 3)
    qkv = jax.nn.silu(causal_dwconv(jnp.concatenate([q, k, v], axis=-1), conv_qkv)).astype(f32)
    q, k, v = [t.reshape(B, S, H, D).transpose(0, 2, 1, 3) for t in jnp.split(qkv, 3, axis=-1)]
    g = -jnp.exp(a_log.astype(f32)) * jax.nn.softplus(a.astype(f32) + dt_bias.astype(f32))
    beta = jax.nn.sigmoid(b.astype(f32))
    o = gated_delta_chunked(l2norm(q), l2norm(k), v, g.transpose(0, 2, 1), beta.transpose(0, 2, 1))
    o = rmsnorm(o.transpose(0, 2, 1, 3), o_norm) * jax.nn.silu(z.astype(f32).reshape(B, S, H, D))
    y_a = o.reshape(B, S, H * D)
    y_b = gb * causal_dwconv(gc * hb, conv_sc)
    return jnp.concatenate([y_a, y_b], axis=-1)


def nsa_compress(t, pos, w1, w2):
    B, G, S, D = t.shape
    r = CMP_LEN // CMP_STRIDE
    nc = S // CMP_STRIDE - r + 1
    tr = t.reshape(B, G, S // CMP_STRIDE, CMP_STRIDE, D)
    blocks = jnp.concatenate([tr[:, :, j:j + nc] for j in range(r)], axis=3)
    blocks = (blocks + pos).reshape(B, G, nc, CMP_LEN * D)
    return jax.nn.gelu(blocks @ w1) @ w2


def nsa_compressed_branch(q, kc, vc):
    S = q.shape[3]
    nc = kc.shape[2]
    t = jnp.arange(S)
    valid = (jnp.arange(nc) * CMP_STRIDE + CMP_LEN - 1)[None, :] <= t[:, None]
    s = jnp.einsum('bgrsd,bgcd->bgrsc', q, kc)
    p = jnp.where(valid, jax.nn.softmax(jnp.where(valid, s, NEG), axis=-1), 0.0)
    return jnp.einsum('bgrsc,bgcd->bgrsd', p, vc), p


def nsa_selected_branch(q, k, v, p_cmp):
    B, G, R, S, D = q.shape
    nc = p_cmp.shape[-1]
    ns = S // SEL_BLOCK
    n_sel = min(SEL_N, ns)
    cs = jnp.arange(nc) * CMP_STRIDE
    ss = jnp.arange(ns) * SEL_BLOCK
    ov = jnp.minimum(cs[:, None] + CMP_LEN, ss[None, :] + SEL_BLOCK) - jnp.maximum(cs[:, None], ss[None, :])
    M = (jnp.clip(ov, 0, None) / CMP_STRIDE).astype(jnp.float32)
    score = jnp.einsum('bgrsc,cn->bgsn', p_cmp, M)
    t = jnp.arange(S)
    blk = jnp.arange(ns)
    cur = (t // SEL_BLOCK)[:, None]
    valid = blk[None, :] * SEL_BLOCK <= t[:, None]
    forced = (blk[None, :] == 0) | (blk[None, :] == cur) | (blk[None, :] == cur - 1)
    score = jnp.where(valid, score + jnp.where(forced, FORCE_BONUS, 0.0), NEG)
    _, idx = lax.top_k(score, n_sel)
    sel = jnp.sum(jax.nn.one_hot(idx, ns, dtype=jnp.float32), axis=-2) > 0
    s = jnp.einsum('bgrsd,bgkd->bgrsk', q, k)
    mask = jnp.repeat(sel, SEL_BLOCK, axis=-1) & (t[None, :] <= t[:, None])
    p = jax.nn.softmax(jnp.where(mask[:, :, None], s, NEG), axis=-1)
    return jnp.einsum('bgrsk,bgkd->bgrsd', p, v)


def nsa_window_branch(q, k, v):
    S = q.shape[3]
    t = jnp.arange(S)
    mask = (t[None, :] <= t[:, None]) & (t[None, :] > t[:, None] - WIN)
    s = jnp.einsum('bgrsd,bgkd->bgrsk', q, k)
    p = jax.nn.softmax(jnp.where(mask, s, NEG), axis=-1)
    return jnp.einsum('bgrsk,bgkd->bgrsd', p, v)


def nsa_core(y, pos_k, w1_k, w2_k, pos_v, w1_v, w2_v, cos, sin):
    B, S, _ = y.shape
    H, G, D = NSA_HEADS, NSA_KV, NSA_HD
    R = H // G
    f32 = jnp.float32
    q, kc, vc, ks, vs, kw, vw, gt = split_cols(y, [H * D] + [G * D] * 6 + [3 * H])
    q = apply_rope(q.reshape(B, S, H, D), cos, sin) * (D ** -0.5)
    q = q.reshape(B, S, G, R, D).transpose(0, 2, 3, 1, 4)
    rk = lambda t: apply_rope(t.reshape(B, S, G, D), cos, sin).transpose(0, 2, 1, 3)
    rv = lambda t: t.reshape(B, S, G, D).transpose(0, 2, 1, 3)
    kc, ks, kw = rk(kc), rk(ks), rk(kw)
    vc, vs, vw = rv(vc), rv(vs), rv(vw)
    gates = jax.nn.sigmoid(gt).reshape(B, S, 3, G, R).transpose(2, 0, 3, 4, 1)[..., None]
    kcmp = nsa_compress(kc, pos_k.astype(f32), w1_k.astype(f32), w2_k.astype(f32))
    vcmp = nsa_compress(vc, pos_v.astype(f32), w1_v.astype(f32), w2_v.astype(f32))
    o_cmp, p_cmp = nsa_compressed_branch(q, kcmp, vcmp)
    o_slc = nsa_selected_branch(q, ks, vs, p_cmp)
    o_win = nsa_window_branch(q, kw, vw)
    o = gates[0] * o_cmp + gates[1] * o_slc + gates[2] * o_win
    return o.transpose(0, 3, 1, 2, 4).reshape(B, S, H * D)


def kernel(x, norm_w, ab_w_in, ab_w_out, gdn_conv, gdn_a_log, gdn_dt_bias, gdn_norm, sc_conv, nsa_w_in, nsa_w_out, cmp_pos_k, cmp_w1_k, cmp_w2_k, cmp_pos_v, cmp_w1_v, cmp_w2_v, ffn_w_gu, ffn_conv, ffn_w_down):
    B, S, D = x.shape
    p = dict(norm_w=norm_w, ab_w_in=ab_w_in, ab_w_out=ab_w_out, gdn_conv=gdn_conv, gdn_a_log=gdn_a_log,
             gdn_dt_bias=gdn_dt_bias, gdn_norm=gdn_norm, sc_conv=sc_conv, nsa_w_in=nsa_w_in,
             nsa_w_out=nsa_w_out, cmp_pos_k=cmp_pos_k, cmp_w1_k=cmp_w1_k, cmp_w2_k=cmp_w2_k,
             cmp_pos_v=cmp_pos_v, cmp_w1_v=cmp_w1_v, cmp_w2_v=cmp_w2_v, ffn_w_gu=ffn_w_gu,
             ffn_conv=ffn_conv, ffn_w_down=ffn_w_down)
    cos, sin = rope_tables(S, NSA_HD)
    h = x.reshape(B * S, D)
    for l in range(norm_w.shape[0]):
        _, h = run_layer(h, l, B, S, cos, sin, p)
    return h.reshape(B, S, D)


def run_layer(h, l, B, S, cos, sin, p):
    D = h.shape[1]
    i = l // 2
    norm_w = p["norm_w"]
    if l % 2 == 0:
        gw = GDN_HEADS * GDN_HD
        w = p["ab_w_in"][i]
        w_main = jnp.concatenate([w[:, :4 * gw], w[:, 4 * gw + 2 * GDN_HEADS:]], axis=1).astype(BF16)
        w_ab = jnp.pad(w[:, 4 * gw:4 * gw + 2 * GDN_HEADS], ((0, 0), (0, 128 - 2 * GDN_HEADS))).astype(BF16)
        y = norm_matmul(h, norm_w[l, 0], w_main, BF16)
        ab = norm_matmul(h, norm_w[l, 0], w_ab, F32)
        y_a = gdn_heads(y, ab, p["gdn_conv"][i], p["gdn_a_log"][i], p["gdn_dt_bias"][i], p["gdn_norm"][i], B, S)
        y_b = short_conv(y, 4 * gw, p["sc_conv"][i], B, S)
        h1 = out_proj([y_a, y_b], p["ab_w_out"][i].astype(BF16), h, norm_w[l, 1], concat=True)
    else:
        w = p["nsa_w_in"][i]
        nm = NSA_HEADS * NSA_HD + 6 * NSA_KV * NSA_HD
        w_gl = jnp.pad(w[:, nm:], ((0, 0), (0, 128 - (w.shape[1] - nm)))).astype(BF16)
        y = norm_matmul(h, norm_w[l, 0], w[:, :nm].astype(BF16), BF16)
        gl = norm_matmul(h, norm_w[l, 0], w_gl, F32)
        o_cmp, o_sw = nsa_mixer_parts(y, gl, p, i, cos, sin, B, S)
        h1 = out_proj([o_cmp, o_sw], p["nsa_w_out"][i].astype(BF16), h, norm_w[l, 1], concat=False)
    h2 = conv_ffn_block(h1, S, norm_w[l, 2], p["ffn_w_gu"][l].astype(BF16), p["ffn_conv"][l],
                        p["ffn_w_down"][l].astype(BF16), norm_w[l, 3])
    return h1, h2
```

```python
import functools
import math

import jax
import jax.numpy as jnp
import numpy as np
from jax import lax
from jax.experimental import pallas as pl
from jax.experimental.pallas import tpu as pltpu

F32 = jnp.float32
BF16 = jnp.bfloat16

EPS = 1e-6
NEG = -1e30

GDN_HEADS = 8
GDN_HD = 128
GDN_CHUNK = 64
NSA_HEADS = 16
NSA_KV = 4
NSA_HD = 128
CMP_LEN = 32
CMP_STRIDE = 16
SEL_BLOCK = 64
SEL_N = 16
WIN = 512
FORCE_BONUS = 1000.0
ROPE_THETA = 10000.0

VMEM_LIMIT_BYTES = 56 * 1024 * 1024
BF16_ROWS = 16


def _cparams(*sem):
    return pltpu.CompilerParams(dimension_semantics=sem, vmem_limit_bytes=VMEM_LIMIT_BYTES)


def _rms(x, w):
    return x * lax.rsqrt(jnp.mean(x * x, axis=-1, keepdims=True) + EPS) * w


def _norm_mm_kernel(h_ref, nw_ref, w_ref, o_ref, xn_ref):
    @pl.when(pl.program_id(1) == 0)
    def _():
        xn_ref[...] = _rms(h_ref[...], nw_ref[...]).astype(BF16)

    o_ref[...] = jnp.dot(xn_ref[...], w_ref[...], preferred_element_type=F32).astype(o_ref.dtype)


def norm_matmul(h, nw, w, out_dtype, tm=512, tn=1024):
    M, D = h.shape
    N = w.shape[1]
    tn = min(tn, N)
    return pl.pallas_call(
        _norm_mm_kernel,
        grid=(M // tm, N // tn),
        in_specs=[pl.BlockSpec((tm, D), lambda i, j: (i, 0)),
                  pl.BlockSpec((1, D), lambda i, j: (0, 0)),
                  pl.BlockSpec((D, tn), lambda i, j: (0, j))],
        out_specs=pl.BlockSpec((tm, tn), lambda i, j: (i, j)),
        out_shape=jax.ShapeDtypeStruct((M, N), out_dtype),
        scratch_shapes=[pltpu.VMEM((tm, D), BF16)],
        compiler_params=_cparams("parallel", "arbitrary"),
        name="norm_matmul",
    )(h, nw.reshape(1, D), w)


def _out_proj_kernel(*refs, n_parts, concat):
    parts = refs[:n_parts]
    w_ref, h_ref, nw_ref, o_ref = refs[n_parts:]
    if concat:
        x = jnp.concatenate([p[...] for p in parts], axis=1)
    else:
        x = parts[0][...].astype(F32)
        for p in parts[1:]:
            x = x + p[...].astype(F32)
        x = x.astype(BF16)
    m = jnp.dot(x, w_ref[...], preferred_element_type=F32)
    o_ref[...] = h_ref[...] + _rms(m, nw_ref[...])


def out_proj(parts, w, h, nw, concat, tm=512):
    M, D = h.shape
    K = w.shape[0]
    in_specs = [pl.BlockSpec((tm, p.shape[1]), lambda i: (i, 0)) for p in parts]
    in_specs += [pl.BlockSpec((K, D), lambda i: (0, 0)),
                 pl.BlockSpec((tm, D), lambda i: (i, 0)),
                 pl.BlockSpec((1, D), lambda i: (0, 0))]
    return pl.pallas_call(
        functools.partial(_out_proj_kernel, n_parts=len(parts), concat=concat),
        grid=(M // tm,),
        in_specs=in_specs,
        out_specs=pl.BlockSpec((tm, D), lambda i: (i, 0)),
        out_shape=jax.ShapeDtypeStruct((M, D), F32),
        compiler_params=_cparams("parallel"),
        name="out_proj",
    )(*parts, w, h, nw.reshape(1, D))


def _ffn_kernel(h_ref, halo_ref, nw2_ref, wg_ref, wu_ref, cw_ref, wd_ref, nw3_ref, o_ref,
                xn_ref, g_ref, acc_ref, *, tiles_per_seq):
    i = pl.program_id(0)
    j = pl.program_id(1)
    tm = h_ref.shape[0]
    H = BF16_ROWS

    @pl.when(j == 0)
    def _():
        xn_ref[pl.ds(H, tm), :] = _rms(h_ref[...], nw2_ref[...]).astype(BF16)
        halo = _rms(halo_ref[...], nw2_ref[...])
        halo = jnp.where(i % tiles_per_seq == 0, 0.0, halo)
        xn_ref[pl.ds(0, H), :] = halo.astype(BF16)
        acc_ref[...] = jnp.zeros_like(acc_ref)

    g_ref[...] = jnp.dot(xn_ref[...], wg_ref[...], preferred_element_type=F32)
    u = jnp.dot(xn_ref[pl.ds(H, tm), :], wu_ref[...], preferred_element_type=F32)
    cw = cw_ref[...]
    c = (g_ref[pl.ds(H, tm), :] * cw[2:3, :] + g_ref[pl.ds(H - 1, tm), :] * cw[1:2, :]
         + g_ref[pl.ds(H - 2, tm), :] * cw[0:1, :])
    hid = (c * jax.nn.sigmoid(c) * u).astype(BF16)
    acc_ref[...] += jnp.dot(hid, wd_ref[...], preferred_element_type=F32)

    @pl.when(j == pl.num_programs(1) - 1)
    def _():
        o_ref[...] = h_ref[...] + _rms(acc_ref[...], nw3_ref[...])


def conv_ffn_block(h, seq_len, nw2, w_gu, conv_w, w_down, nw3, tm=512, tf=512):
    M, D = h.shape
    FF = w_down.shape[0]
    nf = FF // tf
    H = BF16_ROWS
    kern = functools.partial(_ffn_kernel, tiles_per_seq=seq_len // tm)
    return pl.pallas_call(
        kern,
        grid=(M // tm, nf),
        in_specs=[pl.BlockSpec((tm, D), lambda i, j: (i, 0)),
                  pl.BlockSpec((H, D), lambda i, j: (jnp.maximum(i * (tm // H) - 1, 0), 0)),
                  pl.BlockSpec((1, D), lambda i, j: (0, 0)),
                  pl.BlockSpec((D, tf), lambda i, j: (0, j)),
                  pl.BlockSpec((D, tf), lambda i, j: (0, nf + j)),
                  pl.BlockSpec((3, tf), lambda i, j: (0, j)),
                  pl.BlockSpec((tf, D), lambda i, j: (j, 0)),
                  pl.BlockSpec((1, D), lambda i, j: (0, 0))],
        out_specs=pl.BlockSpec((tm, D), lambda i, j: (i, 0)),
        out_shape=jax.ShapeDtypeStruct((M, D), F32),
        scratch_shapes=[pltpu.VMEM((tm + H, D), BF16),
                        pltpu.VMEM((tm + H, tf), F32),
                        pltpu.VMEM((tm, D), F32)],
        compiler_params=_cparams("parallel", "arbitrary"),
        name="conv_ffn",
    )(h, h, nw2.reshape(1, D), w_gu, w_gu, conv_w, w_down, nw3.reshape(1, D))


CONV_HALO = 8


def _silu(x):
    return x * jax.nn.sigmoid(x)


def _causal_conv_ext(ext_ref, x, cw, first):
    T = x.shape[0]
    K = cw.shape[0]

    @pl.when(first)
    def _():
        ext_ref[pl.ds(0, CONV_HALO), :] = jnp.zeros((CONV_HALO, x.shape[1]), F32)

    @pl.when(jnp.logical_not(first))
    def _():
        ext_ref[pl.ds(0, CONV_HALO), :] = ext_ref[pl.ds(T, CONV_HALO), :]

    ext_ref[pl.ds(CONV_HALO, T), :] = x
    y = ext_ref[pl.ds(CONV_HALO, T), :] * cw[K - 1:K, :]
    for k in range(K - 1):
        y = y + ext_ref[pl.ds(CONV_HALO - (K - 1) + k, T), :] * cw[k:k + 1, :]
    return y


def _split_bf16(x):
    hi = x.astype(BF16)
    return hi, (x - hi.astype(F32)).astype(BF16)


def _dot_3pass(a, b):
    ah, al = _split_bf16(a)
    bh, bl = _split_bf16(b)
    d = functools.partial(jnp.dot, preferred_element_type=F32)
    return d(ah, bh) + (d(ah, bl) + d(al, bh))


def _dot_1pass(a, b):
    return jnp.dot(a.astype(BF16), b.astype(BF16), preferred_element_type=F32)


TRI_DOT = _dot_1pass


def _tri_inverse(lm, block):
    n = lm.shape[0]
    ii = lax.broadcasted_iota(jnp.int32, (n, n), 0)
    jj = lax.broadcasted_iota(jnp.int32, (n, n), 1)
    x = jnp.where(ii == jj, 1.0, 0.0)
    s = 1
    while s < block:
        off = (ii // (2 * s) == jj // (2 * s)) & (ii // s != jj // s)
        coff = jnp.where(off, lm, 0.0)
        if s == 1:
            x = x - coff
        else:
            x = x - TRI_DOT(TRI_DOT(x, coff), x)
        s *= 2
    return x


def _gdn_kernel(q_ref, k_ref, v_ref, z_ref, ab_ref, cq_ref, ck_ref, cv_ref, alog_ref, dtb_ref, on_ref,
                o_ref, qe_ref, ke_ref, ve_ref, st_ref):
    h = pl.program_id(1)
    t = pl.program_id(2)
    T, D = q_ref.shape
    C = GDN_CHUNK
    first = t == 0

    @pl.when(first)
    def _():
        st_ref[...] = jnp.zeros_like(st_ref)

    q = _silu(_causal_conv_ext(qe_ref, q_ref[...].astype(F32), cq_ref[...], first))
    k = _silu(_causal_conv_ext(ke_ref, k_ref[...].astype(F32), ck_ref[...], first))
    v = _silu(_causal_conv_ext(ve_ref, v_ref[...].astype(F32), cv_ref[...], first))
    q = q * lax.rsqrt(jnp.sum(q * q, axis=-1, keepdims=True) + EPS) * (D ** -0.5)
    k = k * lax.rsqrt(jnp.sum(k * k, axis=-1, keepdims=True) + EPS)

    ab = ab_ref[...]
    lane = lax.broadcasted_iota(jnp.int32, ab.shape, 1)
    x = ab + dtb_ref[...]
    g_all = -jnp.exp(alog_ref[...]) * (jnp.maximum(x, 0.0) + jnp.log1p(jnp.exp(-jnp.abs(x))))
    g_all = jnp.where(lane < GDN_HEADS, g_all, 0.0)
    ti = lax.broadcasted_iota(jnp.int32, (T, T), 0)
    tj = lax.broadcasted_iota(jnp.int32, (T, T), 1)
    tril = (ti >= tj) & (ti // C == tj // C)
    nt = (((1,), (1,)), ((), ()))
    gc_all = jnp.dot(jnp.where(tril, 1.0, 0.0), g_all, preferred_element_type=F32,
                     precision=lax.Precision.HIGHEST)
    gcol = jnp.sum(jnp.where(lane == h, gc_all, 0.0), axis=-1, keepdims=True)
    beta = jnp.sum(jnp.where(lane == GDN_HEADS + h, jax.nn.sigmoid(ab), 0.0), axis=-1, keepdims=True)
    sel = jnp.where(lax.broadcasted_iota(jnp.int32, (8, ab.shape[1]), 1) == h, 1.0, 0.0)
    grow = lax.dot_general(sel, gc_all, nt, preferred_element_type=F32,
                           precision=lax.Precision.HIGHEST)[0:1, :]

    kb = k * beta
    eg = jnp.exp(gcol)
    decay = jnp.where(tril, jnp.exp(jnp.where(tril, gcol - grow, 0.0)), 0.0)
    qk = lax.dot_general(jnp.concatenate([q, kb], axis=0).astype(BF16), k.astype(BF16), nt,
                         preferred_element_type=F32)
    a_qk = (qk[:T] * decay).astype(BF16)
    lm = jnp.where(ti > tj, qk[T:] * decay, 0.0)
    tm = _tri_inverse(lm, C)
    uw = jnp.dot(tm.astype(BF16), jnp.concatenate([v * beta, kb * eg], axis=1).astype(BF16),
                 preferred_element_type=F32)
    qg = q * eg

    state = st_ref[...]
    for c in range(T // C):
        r = slice(c * C, (c + 1) * C)
        gc = gcol[r]
        ws = jnp.dot(jnp.concatenate([uw[r, D:], qg[r]], axis=0).astype(BF16), state.astype(BF16),
                     preferred_element_type=F32)
        vnew = (uw[r, :D] - ws[:C]).astype(BF16)
        o = ws[C:] + jnp.dot(a_qk[r, c * C:(c + 1) * C], vnew, preferred_element_type=F32)
        glast = gc[C - 1:C, :]
        kdec = k[r] * jnp.exp(glast - gc)
        state = state * jnp.exp(glast) + jnp.dot(kdec.T.astype(BF16), vnew, preferred_element_type=F32)
        zc = z_ref[pl.ds(c * C, C), :].astype(F32)
        o_ref[pl.ds(c * C, C), :] = (_rms(o, on_ref[...]) * _silu(zc)).astype(o_ref.dtype)
    st_ref[...] = state


def gdn_heads(y, ab, conv_qkv, a_log, dt_bias, o_norm, batch, seq_len, T=256):
    H, D = GDN_HEADS, GDN_HD
    nT = seq_len // T
    lanes = ab.shape[1]
    pad = lambda p: jnp.pad(p.astype(F32), (0, lanes - p.shape[0])).reshape(1, lanes)
    row = lambda b, h, t: b * nT + t
    col = lambda off: (lambda b, h, t: (row(b, h, t), off + h))
    cw = lambda off: (lambda b, h, t: (0, off + h))
    const = lambda b, h, t: (0, 0)
    return pl.pallas_call(
        _gdn_kernel,
        grid=(batch, H, nT),
        in_specs=[pl.BlockSpec((T, D), col(0)), pl.BlockSpec((T, D), col(H)),
                  pl.BlockSpec((T, D), col(2 * H)), pl.BlockSpec((T, D), col(3 * H)),
                  pl.BlockSpec((T, lanes), lambda b, h, t: (row(b, h, t), 0)),
                  pl.BlockSpec((conv_qkv.shape[0], D), cw(0)),
                  pl.BlockSpec((conv_qkv.shape[0], D), cw(H)),
                  pl.BlockSpec((conv_qkv.shape[0], D), cw(2 * H)),
                  pl.BlockSpec((1, lanes), const), pl.BlockSpec((1, lanes), const),
                  pl.BlockSpec((1, D), const)],
        out_specs=pl.BlockSpec((T, D), col(0)),
        out_shape=jax.ShapeDtypeStruct((batch * seq_len, H * D), BF16),
        scratch_shapes=[pltpu.VMEM((T + CONV_HALO, D), F32), pltpu.VMEM((T + CONV_HALO, D), F32),
                        pltpu.VMEM((T + CONV_HALO, D), F32), pltpu.VMEM((D, D), F32)],
        compiler_params=_cparams("parallel", "parallel", "arbitrary"),
        name="gdn_heads",
    )(y, y, y, y, ab, conv_qkv, conv_qkv, conv_qkv, pad(a_log), pad(dt_bias), o_norm.reshape(1, D))


def _shortconv_kernel(hb_ref, gb_ref, gc_ref, cw_ref, o_ref, ext_ref):
    first = pl.program_id(2) == 0
    x = gc_ref[...].astype(F32) * hb_ref[...].astype(F32)
    y = _causal_conv_ext(ext_ref, x, cw_ref[...], first)
    o_ref[...] = (gb_ref[...].astype(F32) * y).astype(o_ref.dtype)


def short_conv(y, col_off, conv_sc, batch, seq_len, T=512, tc=512):
    W = conv_sc.shape[1]
    nT = seq_len // T
    nC = W // tc
    spec = lambda off: pl.BlockSpec((T, tc), lambda b, c, t: (b * nT + t, (col_off + off) // tc + c))
    return pl.pallas_call(
        _shortconv_kernel,
        grid=(batch, nC, nT),
        in_specs=[spec(0), spec(W), spec(2 * W),
                  pl.BlockSpec((conv_sc.shape[0], tc), lambda b, c, t: (0, c))],
        out_specs=pl.BlockSpec((T, tc), lambda b, c, t: (b * nT + t, c)),
        out_shape=jax.ShapeDtypeStruct((batch * seq_len, W), BF16),
        scratch_shapes=[pltpu.VMEM((T + CONV_HALO, tc), F32)],
        compiler_params=_cparams("parallel", "parallel", "arbitrary"),
        name="short_conv",
    )(y, y, y, conv_sc)


def _nsa_prep_kernel(y_ref, cos_ref, sin_ref, q_ref, kc_ref, vc_ref, ks_ref, vs_ref, kw_ref, vw_ref):
    T = y_ref.shape[0]
    H, G, D = NSA_HEADS, NSA_KV, NSA_HD
    cos = cos_ref[...]
    sin = sin_ref[...]

    def head(c):
        return y_ref[:, c * D:(c + 1) * D]

    def rope(x):
        x = x.astype(F32)
        return x * cos + pltpu.roll(x, D // 2, 1) * sin

    for hh in range(H):
        q_ref[0, hh] = (rope(head(hh)) * (D ** -0.5)).astype(BF16)
    t0 = pl.program_id(1) * T
    tok = t0 + lax.broadcasted_iota(jnp.int32, (T, D), 0)
    lane = lax.broadcasted_iota(jnp.int32, (T, D), 1)
    onehot = jnp.where(tok // SEL_BLOCK == lane, 1.0, 0.0).astype(BF16)
    for g in range(G):
        kc_ref[0, g] = rope(head(H + g)).astype(BF16)
        vc_ref[0, g] = head(H + G + g)
        ks_ref[0, g, :, 0:D] = rope(head(H + 2 * G + g)).astype(BF16)
        ks_ref[0, g, :, D:2 * D] = onehot
        vs_ref[0, g] = head(H + 3 * G + g).astype(F32).T.astype(BF16)
        kw_ref[0, g] = rope(head(H + 4 * G + g)).astype(BF16)
        vw_ref[0, g] = head(H + 5 * G + g).astype(F32).T.astype(BF16)


def nsa_prep(y, cos2, sin2, batch, seq_len, T=512):
    H, G, D = NSA_HEADS, NSA_KV, NSA_HD
    nT = seq_len // T
    assert seq_len // SEL_BLOCK <= D
    grp = lambda w: jax.ShapeDtypeStruct((batch, G, seq_len, w), BF16)
    gspec = lambda w: pl.BlockSpec((1, G, T, w), lambda b, t: (b, 0, t, 0))
    grp_t = jax.ShapeDtypeStruct((batch, G, D, seq_len), BF16)
    tspec = pl.BlockSpec((1, G, D, T), lambda b, t: (b, 0, 0, t))
    return pl.pallas_call(
        _nsa_prep_kernel,
        grid=(batch, nT),
        in_specs=[pl.BlockSpec((T, y.shape[1]), lambda b, t: (b * nT + t, 0)),
                  pl.BlockSpec((T, D), lambda b, t: (t, 0)),
                  pl.BlockSpec((T, D), lambda b, t: (t, 0))],
        out_specs=[pl.BlockSpec((1, H, T, D), lambda b, t: (b, 0, t, 0)),
                   gspec(D), gspec(D), gspec(2 * D), tspec, gspec(D), tspec],
        out_shape=[jax.ShapeDtypeStruct((batch, H, seq_len, D), BF16),
                   grp(D), grp(D), grp(2 * D), grp_t, grp(D), grp_t],
        compiler_params=_cparams("parallel", "parallel"),
        name="nsa_prep",
    )(y, cos2, sin2)


def _gelu_tanh(x):
    return 0.5 * x * (1.0 + jnp.tanh(math.sqrt(2.0 / math.pi) * (x + 0.044715 * (x * x * x))))


def _compress_kernel(x_ref, pos_ref, w1_ref, w2_ref, o_ref):
    x = x_ref[0].astype(F32)
    half = x.shape[1]
    pos = pos_ref[...]
    top = jnp.dot((x + pos[:, :half]).astype(BF16), w1_ref[0:half, :], preferred_element_type=F32)
    bot = jnp.dot((x + pos[:, half:]).astype(BF16), w1_ref[half:2 * half, :], preferred_element_type=F32)
    n = x.shape[0]
    hid = top + pltpu.roll(bot, n - 1, 0)
    o_ref[0] = jnp.dot(_gelu_tanh(hid).astype(BF16), w2_ref[...], preferred_element_type=F32).astype(BF16)


def nsa_compress_blocks(t, pos, w1, w2):
    B, G, S, D = t.shape
    assert CMP_LEN == 2 * CMP_STRIDE
    n = S // CMP_STRIDE
    x = t.reshape(B * G, n, CMP_STRIDE * D)
    hidden = w1.shape[1]
    return pl.pallas_call(
        _compress_kernel,
        grid=(B * G,),
        in_specs=[pl.BlockSpec((1, n, CMP_STRIDE * D), lambda i: (i, 0, 0)),
                  pl.BlockSpec((1, CMP_LEN * D), lambda i: (0, 0)),
                  pl.BlockSpec((CMP_LEN * D, hidden), lambda i: (0, 0)),
                  pl.BlockSpec((hidden, D), lambda i: (0, 0))],
        out_specs=pl.BlockSpec((1, n, D), lambda i: (i, 0, 0)),
        out_shape=jax.ShapeDtypeStruct((B * G, n, D), BF16),
        compiler_params=_cparams("parallel"),
        name="nsa_compress",
    )(x, pos.reshape(1, CMP_LEN * D).astype(F32), w1.astype(BF16), w2.astype(BF16))


def _gate_col(sig, idx):
    lane = lax.broadcasted_iota(jnp.int32, sig.shape, 1)
    return jnp.sum(jnp.where(lane == idx, sig, 0.0), axis=-1, keepdims=True)


def _cmp_select_kernel(q_ref, kc_ref, vc_ref, gl_ref, ov_ref, o_ref, sb_ref):
    g = pl.program_id(1)
    i = pl.program_id(2)
    R, tq, D = q_ref.shape[1], q_ref.shape[2], q_ref.shape[3]
    nc = kc_ref.shape[1]
    ns = ov_ref.shape[0]
    kc = kc_ref[0]
    vc = vc_ref[0]
    sig = jax.nn.sigmoid(gl_ref[...])
    t_col = i * tq + lax.broadcasted_iota(jnp.int32, (tq, nc), 0)
    c_row = lax.broadcasted_iota(jnp.int32, (tq, nc), 1)
    valid = c_row * CMP_STRIDE + (CMP_LEN - 1) <= t_col
    psum = jnp.zeros((tq, nc), F32)
    for r in range(R):
        s = lax.dot_general(q_ref[0, r], kc, (((1,), (1,)), ((), ())), preferred_element_type=F32)
        s = jnp.where(valid, s, NEG)
        e = jnp.where(valid, jnp.exp(s - jnp.max(s, axis=-1, keepdims=True)), 0.0)
        den = jnp.sum(e, axis=-1, keepdims=True)
        p = e / jnp.where(den > 0.0, den, 1.0)
        psum = psum + p
        o = jnp.dot(p.astype(BF16), vc, preferred_element_type=F32)
        o_ref[:, r * D:(r + 1) * D] = (o * _gate_col(sig, g * R + r)).astype(o_ref.dtype)

    ov = ov_ref[...]
    nt = (((1,), (1,)), ((), ()))
    hi = psum.astype(BF16)
    r1 = psum - hi.astype(F32)
    mid = r1.astype(BF16)
    lo = (r1 - mid.astype(F32)).astype(BF16)
    score = (lax.dot_general(ov, hi, nt, preferred_element_type=F32)
             + (lax.dot_general(ov, mid, nt, preferred_element_type=F32)
                + lax.dot_general(ov, lo, nt, preferred_element_type=F32)))
    n_col = lax.broadcasted_iota(jnp.int32, (ns, tq), 0)
    t_row = i * tq + lax.broadcasted_iota(jnp.int32, (ns, tq), 1)
    cur = t_row // SEL_BLOCK
    forced = (n_col == 0) | (n_col == cur) | (n_col == cur - 1)
    score = jnp.where(n_col * SEL_BLOCK <= t_row, score + jnp.where(forced, FORCE_BONUS, 0.0), NEG)
    rank = jnp.zeros((ns, tq), F32)
    for m in range(ns):
        sm = score[m:m + 1, :]
        ahead = (sm > score) | ((sm == score) & (n_col > m))
        rank = rank + jnp.where(ahead, 1.0, 0.0)
    bias = jnp.where(rank < float(min(SEL_N, ns)), 0.0, NEG)
    bias = jnp.concatenate([bias, jnp.zeros((sb_ref.shape[3] - ns, tq), F32)], axis=0)
    sb_ref[0, 0] = bias.T.astype(sb_ref.dtype)


def _overlap_matrix(nc, ns):
    cs = np.arange(nc) * CMP_STRIDE
    ss = np.arange(ns) * SEL_BLOCK
    ov = np.minimum(cs[None, :] + CMP_LEN, ss[:, None] + SEL_BLOCK) - np.maximum(cs[None, :], ss[:, None])
    ov = np.clip(ov, 0, None) / CMP_STRIDE
    ov[:, nc - CMP_LEN // CMP_STRIDE + 1:] = 0.0
    return jnp.asarray(ov, BF16)


def nsa_cmp_select(q, kcmp, vcmp, gl, batch, seq_len, tq=256):
    H, G, D = NSA_HEADS, NSA_KV, NSA_HD
    R = H // G
    nq = seq_len // tq
    nc = kcmp.shape[1]
    ns = seq_len // SEL_BLOCK
    ov = _overlap_matrix(nc, ns)
    return pl.pallas_call(
        _cmp_select_kernel,
        grid=(batch, G, nq),
        in_specs=[pl.BlockSpec((1, R, tq, D), lambda b, g, i: (b, g, i, 0)),
                  pl.BlockSpec((1, nc, D), lambda b, g, i: (b * G + g, 0, 0)),
                  pl.BlockSpec((1, nc, D), lambda b, g, i: (b * G + g, 0, 0)),
                  pl.BlockSpec((tq, gl.shape[1]), lambda b, g, i: (b * nq + i, 0)),
                  pl.BlockSpec((ns, nc), lambda b, g, i: (0, 0))],
        out_specs=[pl.BlockSpec((tq, R * D), lambda b, g, i: (b * nq + i, g)),
                   pl.BlockSpec((1, 1, tq, D), lambda b, g, i: (b, g, i, 0))],
        out_shape=[jax.ShapeDtypeStruct((batch * seq_len, H * D), BF16),
                   jax.ShapeDtypeStruct((batch, G, seq_len, D), BF16)],
        compiler_params=_cparams("parallel", "parallel", "parallel"),
        name="nsa_cmp_select",
    )(q, kcmp, vcmp, gl, ov)


def _sel_win_kernel(q_ref, sb_ref, ks_ref, vs_ref, kw_ref, vw_ref, gl_ref, o_ref, m_ref, l_ref, acc_ref,
                    *, tk):
    g = pl.program_id(1)
    i = pl.program_id(2)
    R, tq, D = q_ref.shape[1], q_ref.shape[2], q_ref.shape[3]
    cols = R * tq
    H = NSA_HEADS
    nt = (((1,), (1,)), ((), ()))
    q = jnp.concatenate([q_ref[0, r] for r in range(R)], axis=0)
    sb = sb_ref[0, 0]
    q_aug = jnp.concatenate([q, jnp.concatenate([sb] * R, axis=0)], axis=1)
    t_cols = i * tq + (lax.broadcasted_iota(jnp.int32, (1, cols), 1) & (tq - 1))

    m_ref[...] = jnp.full_like(m_ref, NEG)
    l_ref[...] = jnp.zeros_like(l_ref)
    acc_ref[...] = jnp.zeros_like(acc_ref)

    def tile(j, causal):
        off = pl.multiple_of(j * tk, tk)
        k = ks_ref[0, 0, pl.ds(off, tk), :]
        vt = vs_ref[0, 0, :, pl.ds(off, tk)]
        s = lax.dot_general(k, q_aug, nt, preferred_element_type=F32)
        if causal:
            kpos = j * tk + lax.broadcasted_iota(jnp.int32, (tk, cols), 0)
            s = jnp.where(kpos <= t_cols, s, NEG)
        m_old = m_ref[0:1, :]
        m_new = jnp.maximum(m_old, jnp.max(s, axis=0, keepdims=True))
        alpha = jnp.exp(m_old - m_new)
        p = jnp.exp(s - m_new)
        l_ref[0:1, :] = l_ref[0:1, :] * alpha + jnp.sum(p, axis=0, keepdims=True)
        acc_ref[...] = acc_ref[...] * alpha + jnp.dot(vt, p.astype(BF16), preferred_element_type=F32)
        m_ref[0:1, :] = m_new

    last = (i * tq) // tk

    def body(j, carry):
        tile(j, False)
        return carry

    lax.fori_loop(0, last, body, 0)
    tile(last, True)

    sig = jax.nn.sigmoid(gl_ref[...])
    row = lax.broadcasted_iota(jnp.int32, (2 * R, sig.shape[1]), 0)
    lane = lax.broadcasted_iota(jnp.int32, (2 * R, sig.shape[1]), 1)
    pick = jnp.where(lane == H + (row // R) * H + g * R + (row % R), 1.0, 0.0)
    gates = lax.dot_general(pick, sig, nt, preferred_element_type=F32,
                            precision=lax.Precision.HIGHEST)
    g_slc = jnp.concatenate([gates[r:r + 1, :] for r in range(R)], axis=1)
    g_win = jnp.concatenate([gates[R + r:R + r + 1, :] for r in range(R)], axis=1)
    o_t = acc_ref[...] * (g_slc / l_ref[0:1, :])

    wlen = WIN + tq
    start = pl.multiple_of(jnp.maximum(i * tq - WIN, 0), tq)
    kw = kw_ref[0, 0, pl.ds(start, wlen), :]
    vwt = vw_ref[0, 0, :, pl.ds(start, wlen)]
    s = lax.dot_general(kw, q, nt, preferred_element_type=F32)
    kpos = start + lax.broadcasted_iota(jnp.int32, (wlen, cols), 0)
    s = jnp.where((kpos <= t_cols) & (kpos > t_cols - WIN), s, NEG)
    p = jnp.exp(s - jnp.max(s, axis=0, keepdims=True))
    o_w = jnp.dot(vwt, p.astype(BF16), preferred_element_type=F32)
    o_t = o_t + o_w * (g_win / jnp.sum(p, axis=0, keepdims=True))

    for r in range(R):
        o_ref[:, r * D:(r + 1) * D] = o_t[:, r * tq:(r + 1) * tq].T.astype(o_ref.dtype)


def nsa_selected_window(q, sb, ks_aug, vs, kw, vw, gl, batch, seq_len, tq=128, tk=512):
    H, G, D = NSA_HEADS, NSA_KV, NSA_HD
    R = H // G
    nq = seq_len // tq
    assert tk % tq == 0 and WIN % tq == 0 and tq & (tq - 1) == 0 and seq_len >= WIN + tq and tq % 128 == 0
    full = lambda w: pl.BlockSpec((1, 1, seq_len, w), lambda b, g, i: (b, g, 0, 0))
    full_t = pl.BlockSpec((1, 1, D, seq_len), lambda b, g, i: (b, g, 0, 0))
    return pl.pallas_call(
        functools.partial(_sel_win_kernel, tk=tk),
        grid=(batch, G, nq),
        in_specs=[pl.BlockSpec((1, R, tq, D), lambda b, g, i: (b, g, i, 0)),
                  pl.BlockSpec((1, 1, tq, D), lambda b, g, i: (b, g, i, 0)),
                  full(2 * D), full_t, full(D), full_t,
                  pl.BlockSpec((tq, gl.shape[1]), lambda b, g, i: (b * nq + i, 0))],
        out_specs=pl.BlockSpec((tq, R * D), lambda b, g, i: (b * nq + i, g)),
        out_shape=jax.ShapeDtypeStruct((batch * seq_len, H * D), BF16),
        scratch_shapes=[pltpu.VMEM((8, R * tq), F32), pltpu.VMEM((8, R * tq), F32),
                        pltpu.VMEM((D, R * tq), F32)],
        compiler_params=_cparams("parallel", "parallel", "arbitrary"),
        name="nsa_selected_window",
    )(q, sb, ks_aug, vs, kw, vw, gl)


def nsa_mixer_parts(y, gl, p, i, cos, sin, batch, seq_len):
    D = NSA_HD
    cos2 = jnp.concatenate([cos, cos], axis=1)
    sin2 = jnp.concatenate([-sin, sin], axis=1)
    q, kc, vc, ks_aug, vs, kw, vw = nsa_prep(y, cos2, sin2, batch, seq_len)
    kcmp = nsa_compress_blocks(kc, p["cmp_pos_k"][i], p["cmp_w1_k"][i], p["cmp_w2_k"][i])
    vcmp = nsa_compress_blocks(vc, p["cmp_pos_v"][i], p["cmp_w1_v"][i], p["cmp_w2_v"][i])
    o_cmp, sb = nsa_cmp_select(q, kcmp, vcmp, gl, batch, seq_len)
    o_sw = nsa_selected_window(q, sb, ks_aug, vs, kw, vw, gl, batch, seq_len)
    return o_cmp, o_sw


def split_cols(t, widths):
    offs = np.cumsum(widths)[:-1].tolist()
    return jnp.split(t, offs, axis=-1)


def rmsnorm(x, w):
    xf = x.astype(jnp.float32)
    y = xf * lax.rsqrt(jnp.mean(xf * xf, axis=-1, keepdims=True) + EPS)
    return (y * w.astype(jnp.float32)).astype(x.dtype)


def l2norm(x):
    return x * lax.rsqrt(jnp.sum(x * x, axis=-1, keepdims=True) + EPS)


def causal_dwconv(x, w):
    K = w.shape[0]
    S = x.shape[1]
    xp = jnp.pad(x, ((0, 0), (K - 1, 0), (0, 0)))
    return sum(xp[:, k:k + S] * w[k] for k in range(K))


def rope_tables(S, dim):
    inv = 1.0 / (ROPE_THETA ** (jnp.arange(0, dim, 2, dtype=jnp.float32) / dim))
    ang = jnp.arange(S, dtype=jnp.float32)[:, None] * inv[None, :]
    return jnp.cos(ang), jnp.sin(ang)


def apply_rope(x, cos, sin):
    x1, x2 = jnp.split(x, 2, axis=-1)
    c = cos[None, :, None, :]
    s = sin[None, :, None, :]
    return jnp.concatenate([x1 * c - x2 * s, x2 * c + x1 * s], axis=-1)


def gated_delta_chunked(q, k, v, g, beta):
    B, H, S, D = q.shape
    C = GDN_CHUNK
    N = S // C
    q = q * (D ** -0.5)
    rs = lambda t: t.reshape((B, H, N, C) + t.shape[3:])
    q, k, v, beta = rs(q), rs(k), rs(v), rs(beta)
    g = jnp.cumsum(rs(g), axis=-1)
    kb = k * beta[..., None]
    vb = v * beta[..., None]
    tril = jnp.tril(jnp.ones((C, C), dtype=bool))
    stril = jnp.tril(jnp.ones((C, C), dtype=bool), -1)
    gdiff = g[..., :, None] - g[..., None, :]
    decay = jnp.where(tril, jnp.exp(jnp.where(tril, gdiff, 0.0)), 0.0)
    Lm = jnp.where(stril, jnp.einsum('bhnid,bhnjd->bhnij', kb, k) * decay, 0.0)
    eye = jnp.eye(C, dtype=jnp.float32)
    T = lax.linalg.triangular_solve(eye + Lm, jnp.broadcast_to(eye, Lm.shape),
                                    left_side=True, lower=True)
    u = T @ vb
    w = T @ (kb * jnp.exp(g)[..., None])
    qk = jnp.where(tril, jnp.einsum('bhnid,bhnjd->bhnij', q, k) * decay, 0.0)

    def step(state, xs):
        qc, kc, uc, wc, gc, ac = xs
        vnew = uc - wc @ state
        o = (qc * jnp.exp(gc)[..., None]) @ state + ac @ vnew
        glast = gc[..., -1]
        kdec = kc * jnp.exp(glast[..., None] - gc)[..., None]
        state = state * jnp.exp(glast)[..., None, None] + jnp.einsum('bhcd,bhce->bhde', kdec, vnew)
        return state, o

    xs = tuple(jnp.moveaxis(t, 2, 0) for t in (q, k, u, w, g, qk))
    s0 = jnp.zeros((B, H, D, D), jnp.float32)
    _, o = lax.scan(step, s0, xs)
    return jnp.moveaxis(o, 0, 2).reshape(B, H, S, D)


def delta_shortconv_core(y, conv_qkv, a_log, dt_bias, o_norm, conv_sc):
    B, S, _ = y.shape
    H, D = GDN_HEADS, GDN_HD
    SCW = H * D
    f32 = jnp.float32
    q, k, v, z, a, b, hb, gb, gc = split_cols(y, [H * D] * 4 + [H] * 2 + [SCW] * 3)
    qkv = jax.nn.silu(causal_dwconv(jnp.concatenate([q, k, v], axis=-1), conv_qkv)).astype(f32)
    q, k, v = [t.reshape(B, S, H, D).transpose(0, 2, 1, 3) for t in jnp.split(qkv, 3, axis=-1)]
    g = -jnp.exp(a_log.astype(f32)) * jax.nn.softplus(a.astype(f32) + dt_bias.astype(f32))
    beta = jax.nn.sigmoid(b.astype(f32))
    o = gated_delta_chunked(l2norm(q), l2norm(k), v, g.transpose(0, 2, 1), beta.transpose(0, 2, 1))
    o = rmsnorm(o.transpose(0, 2, 1, 3), o_norm) * jax.nn.silu(z.astype(f32).reshape(B, S, H, D))
    y_a = o.reshape(B, S, H * D)
    y_b = gb * causal_dwconv(gc * hb, conv_sc)
    return jnp.concatenate([y_a, y_b], axis=-1)


def nsa_compress(t, pos, w1, w2):
    B, G, S, D = t.shape
    r = CMP_LEN // CMP_STRIDE
    nc = S // CMP_STRIDE - r + 1
    tr = t.reshape(B, G, S // CMP_STRIDE, CMP_STRIDE, D)
    blocks = jnp.concatenate([tr[:, :, j:j + nc] for j in range(r)], axis=3)
    blocks = (blocks + pos).reshape(B, G, nc, CMP_LEN * D)
    return jax.nn.gelu(blocks @ w1) @ w2


def nsa_compressed_branch(q, kc, vc):
    S = q.shape[3]
    nc = kc.shape[2]
    t = jnp.arange(S)
    valid = (jnp.arange(nc) * CMP_STRIDE + CMP_LEN - 1)[None, :] <= t[:, None]
    s = jnp.einsum('bgrsd,bgcd->bgrsc', q, kc)
    p = jnp.where(valid, jax.nn.softmax(jnp.where(valid, s, NEG), axis=-1), 0.0)
    return jnp.einsum('bgrsc,bgcd->bgrsd', p, vc), p


def nsa_selected_branch(q, k, v, p_cmp):
    B, G, R, S, D = q.shape
    nc = p_cmp.shape[-1]
    ns = S // SEL_BLOCK
    n_sel = min(SEL_N, ns)
    cs = jnp.arange(nc) * CMP_STRIDE
    ss = jnp.arange(ns) * SEL_BLOCK
    ov = jnp.minimum(cs[:, None] + CMP_LEN, ss[None, :] + SEL_BLOCK) - jnp.maximum(cs[:, None], ss[None, :])
    M = (jnp.clip(ov, 0, None) / CMP_STRIDE).astype(jnp.float32)
    score = jnp.einsum('bgrsc,cn->bgsn', p_cmp, M)
    t = jnp.arange(S)
    blk = jnp.arange(ns)
    cur = (t // SEL_BLOCK)[:, None]
    valid = blk[None, :] * SEL_BLOCK <= t[:, None]
    forced = (blk[None, :] == 0) | (blk[None, :] == cur) | (blk[None, :] == cur - 1)
    score = jnp.where(valid, score + jnp.where(forced, FORCE_BONUS, 0.0), NEG)
    _, idx = lax.top_k(score, n_sel)
    sel = jnp.sum(jax.nn.one_hot(idx, ns, dtype=jnp.float32), axis=-2) > 0
    s = jnp.einsum('bgrsd,bgkd->bgrsk', q, k)
    mask = jnp.repeat(sel, SEL_BLOCK, axis=-1) & (t[None, :] <= t[:, None])
    p = jax.nn.softmax(jnp.where(mask[:, :, None], s, NEG), axis=-1)
    return jnp.einsum('bgrsk,bgkd->bgrsd', p, v)


def nsa_window_branch(q, k, v):
    S = q.shape[3]
    t = jnp.arange(S)
    mask = (t[None, :] <= t[:, None]) & (t[None, :] > t[:, None] - WIN)
    s = jnp.einsum('bgrsd,bgkd->bgrsk', q, k)
    p = jax.nn.softmax(jnp.where(mask, s, NEG), axis=-1)
    return jnp.einsum('bgrsk,bgkd->bgrsd', p, v)


def nsa_core(y, pos_k, w1_k, w2_k, pos_v, w1_v, w2_v, cos, sin):
    B, S, _ = y.shape
    H, G, D = NSA_HEADS, NSA_KV, NSA_HD
    R = H // G
    f32 = jnp.float32
    q, kc, vc, ks, vs, kw, vw, gt = split_cols(y, [H * D] + [G * D] * 6 + [3 * H])
    q = apply_rope(q.reshape(B, S, H, D), cos, sin) * (D ** -0.5)
    q = q.reshape(B, S, G, R, D).transpose(0, 2, 3, 1, 4)
    rk = lambda t: apply_rope(t.reshape(B, S, G, D), cos, sin).transpose(0, 2, 1, 3)
    rv = lambda t: t.reshape(B, S, G, D).transpose(0, 2, 1, 3)
    kc, ks, kw = rk(kc), rk(ks), rk(kw)
    vc, vs, vw = rv(vc), rv(vs), rv(vw)
    gates = jax.nn.sigmoid(gt).reshape(B, S, 3, G, R).transpose(2, 0, 3, 4, 1)[..., None]
    kcmp = nsa_compress(kc, pos_k.astype(f32), w1_k.astype(f32), w2_k.astype(f32))
    vcmp = nsa_compress(vc, pos_v.astype(f32), w1_v.astype(f32), w2_v.astype(f32))
    o_cmp, p_cmp = nsa_compressed_branch(q, kcmp, vcmp)
    o_slc = nsa_selected_branch(q, ks, vs, p_cmp)
    o_win = nsa_window_branch(q, kw, vw)
    o = gates[0] * o_cmp + gates[1] * o_slc + gates[2] * o_win
    return o.transpose(0, 3, 1, 2, 4).reshape(B, S, H * D)


def kernel(x, norm_w, ab_w_in, ab_w_out, gdn_conv, gdn_a_log, gdn_dt_bias, gdn_norm, sc_conv, nsa_w_in, nsa_w_out, cmp_pos_k, cmp_w1_k, cmp_w2_k, cmp_pos_v, cmp_w1_v, cmp_w2_v, ffn_w_gu, ffn_conv, ffn_w_down):
    B, S, D = x.shape
    p = dict(norm_w=norm_w, ab_w_in=ab_w_in, ab_w_out=ab_w_out, gdn_conv=gdn_conv, gdn_a_log=gdn_a_log,
             gdn_dt_bias=gdn_dt_bias, gdn_norm=gdn_norm, sc_conv=sc_conv, nsa_w_in=nsa_w_in,
             nsa_w_out=nsa_w_out, cmp_pos_k=cmp_pos_k, cmp_w1_k=cmp_w1_k, cmp_w2_k=cmp_w2_k,
             cmp_pos_v=cmp_pos_v, cmp_w1_v=cmp_w1_v, cmp_w2_v=cmp_w2_v, ffn_w_gu=ffn_w_gu,
             ffn_conv=ffn_conv, ffn_w_down=ffn_w_down)
    cos, sin = rope_tables(S, NSA_HD)
    h = x.reshape(B * S, D)
    for l in range(norm_w.shape[0]):
        _, h = run_layer(h, l, B, S, cos, sin, p)
    return h.reshape(B, S, D)


def run_layer(h, l, B, S, cos, sin, p):
    D = h.shape[1]
    i = l // 2
    norm_w = p["norm_w"]
    if l % 2 == 0:
        gw = GDN_HEADS * GDN_HD
        w = p["ab_w_in"][i]
        w_main = jnp.concatenate([w[:, :4 * gw], w[:, 4 * gw + 2 * GDN_HEADS:]], axis=1).astype(BF16)
        w_ab = jnp.pad(w[:, 4 * gw:4 * gw + 2 * GDN_HEADS], ((0, 0), (0, 128 - 2 * GDN_HEADS))).astype(BF16)
        y = norm_matmul(h, norm_w[l, 0], w_main, BF16)
        ab = norm_matmul(h, norm_w[l, 0], w_ab, F32)
        y_a = gdn_heads(y, ab, p["gdn_conv"][i], p["gdn_a_log"][i], p["gdn_dt_bias"][i], p["gdn_norm"][i], B, S)
        y_b = short_conv(y, 4 * gw, p["sc_conv"][i], B, S)
        h1 = out_proj([y_a, y_b], p["ab_w_out"][i].astype(BF16), h, norm_w[l, 1], concat=True)
    else:
        w = p["nsa_w_in"][i]
        nm = NSA_HEADS * NSA_HD + 6 * NSA_KV * NSA_HD
        w_gl = jnp.pad(w[:, nm:], ((0, 0), (0, 128 - (w.shape[1] - nm)))).astype(BF16)
        y = norm_matmul(h, norm_w[l, 0], w[:, :nm].astype(BF16), BF16)
        gl = norm_matmul(h, norm_w[l, 0], w_gl, F32)
        o_cmp, o_sw = nsa_mixer_parts(y, gl, p, i, cos, sin, B, S)
        h1 = out_proj([o_cmp, o_sw], p["nsa_w_out"][i].astype(BF16), h, norm_w[l, 1], concat=False)
    h2 = conv_ffn_block(h1, S, norm_w[l, 2], p["ffn_w_gu"][l].astype(BF16), p["ffn_conv"][l],
                        p["ffn_w_down"][l].astype(BF16), norm_w[l, 3])
    return h1, h2
```

```python
import functools
import math

import jax
import jax.numpy as jnp
import numpy as np
from jax import lax
from jax.experimental import pallas as pl
from jax.experimental.pallas import tpu as pltpu

F32 = jnp.float32
BF16 = jnp.bfloat16

EPS = 1e-6
NEG = -1e30

GDN_HEADS = 8
GDN_HD = 128
GDN_CHUNK = 64
NSA_HEADS = 16
NSA_KV = 4
NSA_HD = 128
CMP_LEN = 32
CMP_STRIDE = 16
SEL_BLOCK = 64
SEL_N = 16
WIN = 512
FORCE_BONUS = 1000.0
ROPE_THETA = 10000.0

VMEM_LIMIT_BYTES = 56 * 1024 * 1024
FFN_VMEM_LIMIT_BYTES = 62 * 1024 * 1024
BF16_ROWS = 16


def _cparams(*sem, vmem=VMEM_LIMIT_BYTES):
    return pltpu.CompilerParams(dimension_semantics=sem, vmem_limit_bytes=vmem)


def _rms(x, w):
    return x * lax.rsqrt(jnp.mean(x * x, axis=-1, keepdims=True) + EPS) * w


def _norm_mm_kernel(h_ref, nw_ref, w_ref, ws_ref, o_ref, os_ref, xn_ref):
    @pl.when(pl.program_id(1) == 0)
    def _():
        xn_ref[...] = _rms(h_ref[...], nw_ref[...]).astype(BF16)
        os_ref[...] = jnp.dot(xn_ref[...], ws_ref[...], preferred_element_type=F32)

    o_ref[...] = jnp.dot(xn_ref[...], w_ref[...], preferred_element_type=F32).astype(o_ref.dtype)


def norm_matmul(h, nw, w, w_small, tm=512, tn=1024):
    M, D = h.shape
    N = w.shape[1]
    Ns = w_small.shape[1]
    tn = min(tn, N)
    return pl.pallas_call(
        _norm_mm_kernel,
        grid=(M // tm, N // tn),
        in_specs=[pl.BlockSpec((tm, D), lambda i, j: (i, 0)),
                  pl.BlockSpec((1, D), lambda i, j: (0, 0)),
                  pl.BlockSpec((D, tn), lambda i, j: (0, j)),
                  pl.BlockSpec((D, Ns), lambda i, j: (0, 0))],
        out_specs=[pl.BlockSpec((tm, tn), lambda i, j: (i, j)),
                   pl.BlockSpec((tm, Ns), lambda i, j: (i, 0))],
        out_shape=[jax.ShapeDtypeStruct((M, N), BF16), jax.ShapeDtypeStruct((M, Ns), F32)],
        scratch_shapes=[pltpu.VMEM((tm, D), BF16)],
        compiler_params=_cparams("parallel", "arbitrary"),
        name="norm_matmul",
    )(h, nw.reshape(1, D), w, w_small)


def _out_proj_kernel(*refs, n_parts, concat):
    parts = refs[:n_parts]
    w_ref, h_ref, nw_ref, o_ref = refs[n_parts:]
    if concat:
        x = jnp.concatenate([p[...] for p in parts], axis=1)
    else:
        x = parts[0][...].astype(F32)
        for p in parts[1:]:
            x = x + p[...].astype(F32)
        x = x.astype(BF16)
    m = jnp.dot(x, w_ref[...], preferred_element_type=F32)
    o_ref[...] = h_ref[...] + _rms(m, nw_ref[...])


def out_proj(parts, w, h, nw, concat, tm=512):
    M, D = h.shape
    K = w.shape[0]
    in_specs = [pl.BlockSpec((tm, p.shape[1]), lambda i: (i, 0)) for p in parts]
    in_specs += [pl.BlockSpec((K, D), lambda i: (0, 0)),
                 pl.BlockSpec((tm, D), lambda i: (i, 0)),
                 pl.BlockSpec((1, D), lambda i: (0, 0))]
    return pl.pallas_call(
        functools.partial(_out_proj_kernel, n_parts=len(parts), concat=concat),
        grid=(M // tm,),
        in_specs=in_specs,
        out_specs=pl.BlockSpec((tm, D), lambda i: (i, 0)),
        out_shape=jax.ShapeDtypeStruct((M, D), F32),
        compiler_params=_cparams("parallel"),
        name="out_proj",
    )(*parts, w, h, nw.reshape(1, D))


def _ffn_kernel(h_ref, halo_ref, nw2_ref, wg_ref, wu_ref, cw_ref, wd_ref, nw3_ref, o_ref,
                xn_ref, g_ref, *, tiles_per_seq):
    i = pl.program_id(0)
    j = pl.program_id(1)
    tm = h_ref.shape[0]
    H = BF16_ROWS

    @pl.when(j == 0)
    def _():
        xn_ref[pl.ds(H, tm), :] = _rms(h_ref[...], nw2_ref[...]).astype(BF16)
        halo = _rms(halo_ref[...], nw2_ref[...])
        halo = jnp.where(i % tiles_per_seq == 0, 0.0, halo)
        xn_ref[pl.ds(0, H), :] = halo.astype(BF16)

    g_ref[...] = jnp.dot(xn_ref[...], wg_ref[...], preferred_element_type=F32)
    u = jnp.dot(xn_ref[pl.ds(H, tm), :], wu_ref[...], preferred_element_type=F32)
    cw = cw_ref[...]
    c = (g_ref[pl.ds(H, tm), :] * cw[2:3, :] + g_ref[pl.ds(H - 1, tm), :] * cw[1:2, :]
         + g_ref[pl.ds(H - 2, tm), :] * cw[0:1, :])
    hid = (c * jax.nn.sigmoid(c) * u).astype(BF16)
    part = jnp.dot(hid, wd_ref[...], preferred_element_type=F32)

    @pl.when(j == 0)
    def _():
        o_ref[...] = part

    @pl.when(j > 0)
    def _():
        o_ref[...] += part

    @pl.when(j == pl.num_programs(1) - 1)
    def _():
        o_ref[...] = h_ref[...] + _rms(o_ref[...], nw3_ref[...])


def conv_ffn_block(h, seq_len, nw2, w_gu, conv_w, w_down, nw3, tm=1024, tf=512):
    M, D = h.shape
    FF = w_down.shape[0]
    nf = FF // tf
    H = BF16_ROWS
    kern = functools.partial(_ffn_kernel, tiles_per_seq=seq_len // tm)
    return pl.pallas_call(
        kern,
        grid=(M // tm, nf),
        in_specs=[pl.BlockSpec((tm, D), lambda i, j: (i, 0), pipeline_mode=pl.Buffered(1)),
                  pl.BlockSpec((H, D), lambda i, j: (jnp.maximum(i * (tm // H) - 1, 0), 0)),
                  pl.BlockSpec((1, D), lambda i, j: (0, 0)),
                  pl.BlockSpec((D, tf), lambda i, j: (0, j)),
                  pl.BlockSpec((D, tf), lambda i, j: (0, nf + j)),
                  pl.BlockSpec((3, tf), lambda i, j: (0, j)),
                  pl.BlockSpec((tf, D), lambda i, j: (j, 0)),
                  pl.BlockSpec((1, D), lambda i, j: (0, 0))],
        out_specs=pl.BlockSpec((tm, D), lambda i, j: (i, 0)),
        out_shape=jax.ShapeDtypeStruct((M, D), F32),
        scratch_shapes=[pltpu.VMEM((tm + H, D), BF16),
                        pltpu.VMEM((tm + H, tf), F32)],
        compiler_params=_cparams("parallel", "arbitrary", vmem=FFN_VMEM_LIMIT_BYTES),
        name="conv_ffn",
    )(h, h, nw2.reshape(1, D), w_gu, w_gu, conv_w, w_down, nw3.reshape(1, D))


CONV_HALO = 8


def _silu(x):
    return x * jax.nn.sigmoid(x)


def _causal_conv_ext(ext_ref, x, cw, first):
    T = x.shape[0]
    K = cw.shape[0]

    @pl.when(first)
    def _():
        ext_ref[pl.ds(0, CONV_HALO), :] = jnp.zeros((CONV_HALO, x.shape[1]), F32)

    @pl.when(jnp.logical_not(first))
    def _():
        ext_ref[pl.ds(0, CONV_HALO), :] = ext_ref[pl.ds(T, CONV_HALO), :]

    ext_ref[pl.ds(CONV_HALO, T), :] = x
    y = ext_ref[pl.ds(CONV_HALO, T), :] * cw[K - 1:K, :]
    for k in range(K - 1):
        y = y + ext_ref[pl.ds(CONV_HALO - (K - 1) + k, T), :] * cw[k:k + 1, :]
    return y


def _split_bf16(x):
    hi = x.astype(BF16)
    return hi, (x - hi.astype(F32)).astype(BF16)


def _dot_3pass(a, b):
    ah, al = _split_bf16(a)
    bh, bl = _split_bf16(b)
    d = functools.partial(jnp.dot, preferred_element_type=F32)
    return d(ah, bh) + (d(ah, bl) + d(al, bh))


def _dot_1pass(a, b):
    return jnp.dot(a.astype(BF16), b.astype(BF16), preferred_element_type=F32)


TRI_DOT = _dot_1pass


def _tri_inverse(lm, block):
    n = lm.shape[0]
    ii = lax.broadcasted_iota(jnp.int32, (n, n), 0)
    jj = lax.broadcasted_iota(jnp.int32, (n, n), 1)
    x = jnp.where(ii == jj, 1.0, 0.0)
    s = 1
    while s < block:
        off = (ii // (2 * s) == jj // (2 * s)) & (ii // s != jj // s)
        coff = jnp.where(off, lm, 0.0)
        if s == 1:
            x = x - coff
        else:
            x = x - TRI_DOT(TRI_DOT(x, coff), x)
        s *= 2
    return x


def _gdn_kernel(q_ref, k_ref, v_ref, z_ref, ab_ref, cq_ref, ck_ref, cv_ref, alog_ref, dtb_ref, on_ref,
                o_ref, qe_ref, ke_ref, ve_ref, st_ref):
    t = pl.program_id(2)
    T = q_ref.shape[0]
    D = GDN_HD
    HP = q_ref.shape[1] // D
    C = GDN_CHUNK
    first = t == 0
    nt = (((1,), (1,)), ((), ()))

    @pl.when(first)
    def _():
        st_ref[...] = jnp.zeros_like(st_ref)

    q_all = _silu(_causal_conv_ext(qe_ref, q_ref[...].astype(F32), cq_ref[...], first))
    k_all = _silu(_causal_conv_ext(ke_ref, k_ref[...].astype(F32), ck_ref[...], first))
    v_all = _silu(_causal_conv_ext(ve_ref, v_ref[...].astype(F32), cv_ref[...], first))

    ab = ab_ref[...]
    lane = lax.broadcasted_iota(jnp.int32, ab.shape, 1)
    x = ab + dtb_ref[...]
    g_all = -jnp.exp(alog_ref[...]) * (jnp.maximum(x, 0.0) + jnp.log1p(jnp.exp(-jnp.abs(x))))
    g_all = jnp.where(lane < GDN_HEADS, g_all, 0.0)
    sig_ab = jax.nn.sigmoid(ab)
    ti = lax.broadcasted_iota(jnp.int32, (T, T), 0)
    tj = lax.broadcasted_iota(jnp.int32, (T, T), 1)
    tril = (ti >= tj) & (ti // C == tj // C)
    stril = tril & (ti > tj)
    gc_all = jnp.dot(jnp.where(tril, 1.0, 0.0), g_all, preferred_element_type=F32,
                     precision=lax.Precision.HIGHEST)
    lane8 = lax.broadcasted_iota(jnp.int32, (8, ab.shape[1]), 1)

    heads = range(HP)
    dot = functools.partial(jnp.dot, preferred_element_type=F32)
    hid = [pl.program_id(1) * HP + hh for hh in heads]
    cs = [slice(hh * D, (hh + 1) * D) for hh in heads]
    q = [q_all[:, c] for c in cs]
    k = [k_all[:, c] for c in cs]
    q = [x * lax.rsqrt(jnp.sum(x * x, axis=-1, keepdims=True) + EPS) * (D ** -0.5) for x in q]
    k = [x * lax.rsqrt(jnp.sum(x * x, axis=-1, keepdims=True) + EPS) for x in k]
    gcol = [jnp.sum(jnp.where(lane == h, gc_all, 0.0), axis=-1, keepdims=True) for h in hid]
    beta = [jnp.sum(jnp.where(lane == GDN_HEADS + h, sig_ab, 0.0), axis=-1, keepdims=True) for h in hid]
    grow = [lax.dot_general(jnp.where(lane8 == h, 1.0, 0.0), gc_all, nt, preferred_element_type=F32,
                            precision=lax.Precision.HIGHEST)[0:1, :] for h in hid]

    kb = [k[i] * beta[i] for i in heads]
    eg = [jnp.exp(gcol[i]) for i in heads]
    decay = [jnp.where(tril, jnp.exp(jnp.where(tril, gcol[i] - grow[i], 0.0)), 0.0) for i in heads]
    qk = [lax.dot_general(jnp.concatenate([q[i], kb[i]], axis=0).astype(BF16), k[i].astype(BF16), nt,
                          preferred_element_type=F32) for i in heads]
    a_qk = [(qk[i][:T] * decay[i]).astype(BF16) for i in heads]
    lm = [jnp.where(stril, qk[i][T:] * decay[i], 0.0) for i in heads]

    inv = [jnp.where(ti == tj, 1.0, 0.0) - jnp.where(ti // 2 == tj // 2, lm[i], 0.0) for i in heads]
    s = 2
    while s < C:
        off = (ti // (2 * s) == tj // (2 * s)) & (ti // s != tj // s)
        xb = [inv[i].astype(BF16) for i in heads]
        xc = [dot(xb[i], jnp.where(off, lm[i], 0.0).astype(BF16)) for i in heads]
        inv = [inv[i] - dot(xc[i].astype(BF16), xb[i]) for i in heads]
        s *= 2

    uw = [dot(inv[i].astype(BF16), jnp.concatenate([v_all[:, cs[i]] * beta[i], kb[i] * eg[i]], axis=1).astype(BF16))
          for i in heads]
    qg = [q[i] * eg[i] for i in heads]

    state = [st_ref[i] for i in heads]
    for c in range(T // C):
        r = slice(c * C, (c + 1) * C)
        ws = [dot(jnp.concatenate([uw[i][r, D:], qg[i][r]], axis=0).astype(BF16), state[i].astype(BF16))
              for i in heads]
        vnew = [(uw[i][r, :D] - ws[i][:C]).astype(BF16) for i in heads]
        glast = [gcol[i][c * C + C - 1:(c + 1) * C, :] for i in heads]
        kdec = [(k[i][r] * jnp.exp(glast[i] - gcol[i][r])).T.astype(BF16) for i in heads]
        state = [state[i] * jnp.exp(glast[i]) + dot(kdec[i], vnew[i]) for i in heads]
        for i in heads:
            o = ws[i][C:] + dot(a_qk[i][r, c * C:(c + 1) * C], vnew[i])
            zc = z_ref[pl.ds(c * C, C), cs[i]].astype(F32)
            o_ref[pl.ds(c * C, C), cs[i]] = (_rms(o, on_ref[...]) * _silu(zc)).astype(o_ref.dtype)
    for i in heads:
        st_ref[i] = state[i]


def gdn_heads(y, ab, conv_qkv, a_log, dt_bias, o_norm, batch, seq_len, T=256, heads_per_step=4):
    H, D = GDN_HEADS, GDN_HD
    HP = heads_per_step
    W = HP * D
    nG = H // HP
    nT = seq_len // T
    lanes = ab.shape[1]
    pad = lambda p: jnp.pad(p.astype(F32), (0, lanes - p.shape[0])).reshape(1, lanes)
    row = lambda b, h, t: b * nT + t
    col = lambda off: (lambda b, h, t: (row(b, h, t), off * nG + h))
    cw = lambda off: (lambda b, h, t: (0, off * nG + h))
    const = lambda b, h, t: (0, 0)
    K = conv_qkv.shape[0]
    return pl.pallas_call(
        _gdn_kernel,
        grid=(batch, nG, nT),
        in_specs=[pl.BlockSpec((T, W), col(0)), pl.BlockSpec((T, W), col(1)),
                  pl.BlockSpec((T, W), col(2)), pl.BlockSpec((T, W), col(3)),
                  pl.BlockSpec((T, lanes), lambda b, h, t: (row(b, h, t), 0)),
                  pl.BlockSpec((K, W), cw(0)), pl.BlockSpec((K, W), cw(1)), pl.BlockSpec((K, W), cw(2)),
                  pl.BlockSpec((1, lanes), const), pl.BlockSpec((1, lanes), const),
                  pl.BlockSpec((1, D), const)],
        out_specs=pl.BlockSpec((T, W), col(0)),
        out_shape=jax.ShapeDtypeStruct((batch * seq_len, H * D), BF16),
        scratch_shapes=[pltpu.VMEM((T + CONV_HALO, W), F32), pltpu.VMEM((T + CONV_HALO, W), F32),
                        pltpu.VMEM((T + CONV_HALO, W), F32), pltpu.VMEM((HP, D, D), F32)],
        compiler_params=_cparams("parallel", "parallel", "arbitrary"),
        name="gdn_heads",
    )(y, y, y, y, ab, conv_qkv, conv_qkv, conv_qkv, pad(a_log), pad(dt_bias), o_norm.reshape(1, D))


def _shortconv_kernel(hb_ref, gb_ref, gc_ref, cw_ref, o_ref, ext_ref):
    first = pl.program_id(2) == 0
    x = gc_ref[...].astype(F32) * hb_ref[...].astype(F32)
    y = _causal_conv_ext(ext_ref, x, cw_ref[...], first)
    o_ref[...] = (gb_ref[...].astype(F32) * y).astype(o_ref.dtype)


def short_conv(y, col_off, conv_sc, batch, seq_len, T=512, tc=512):
    W = conv_sc.shape[1]
    nT = seq_len // T
    nC = W // tc
    spec = lambda off: pl.BlockSpec((T, tc), lambda b, c, t: (b * nT + t, (col_off + off) // tc + c))
    return pl.pallas_call(
        _shortconv_kernel,
        grid=(batch, nC, nT),
        in_specs=[spec(0), spec(W), spec(2 * W),
                  pl.BlockSpec((conv_sc.shape[0], tc), lambda b, c, t: (0, c))],
        out_specs=pl.BlockSpec((T, tc), lambda b, c, t: (b * nT + t, c)),
        out_shape=jax.ShapeDtypeStruct((batch * seq_len, W), BF16),
        scratch_shapes=[pltpu.VMEM((T + CONV_HALO, tc), F32)],
        compiler_params=_cparams("parallel", "parallel", "arbitrary"),
        name="short_conv",
    )(y, y, y, conv_sc)


def _nsa_prep_kernel(y_ref, cos_ref, sin_ref, q_ref, kc_ref, vc_ref, ks_ref, vs_ref, kw_ref, vw_ref):
    T = y_ref.shape[0]
    H, G, D = NSA_HEADS, NSA_KV, NSA_HD
    cos = cos_ref[...]
    sin = sin_ref[...]

    def head(c):
        return y_ref[:, c * D:(c + 1) * D]

    def rope(x):
        x = x.astype(F32)
        return x * cos + pltpu.roll(x, D // 2, 1) * sin

    for hh in range(H):
        q_ref[0, hh] = (rope(head(hh)) * (D ** -0.5)).astype(BF16)
    t0 = pl.program_id(1) * T
    tok = t0 + lax.broadcasted_iota(jnp.int32, (T, D), 0)
    lane = lax.broadcasted_iota(jnp.int32, (T, D), 1)
    onehot = jnp.where(tok // SEL_BLOCK == lane, 1.0, 0.0).astype(BF16)
    for g in range(G):
        kc_ref[0, g] = rope(head(H + g)).astype(BF16)
        vc_ref[0, g] = head(H + G + g)
        ks_ref[0, g, :, 0:D] = rope(head(H + 2 * G + g)).astype(BF16)
        ks_ref[0, g, :, D:2 * D] = onehot
        vs_ref[0, g] = head(H + 3 * G + g).astype(F32).T.astype(BF16)
        kw_ref[0, g] = rope(head(H + 4 * G + g)).astype(BF16)
        vw_ref[0, g] = head(H + 5 * G + g).astype(F32).T.astype(BF16)


def nsa_prep(y, cos2, sin2, batch, seq_len, T=512):
    H, G, D = NSA_HEADS, NSA_KV, NSA_HD
    nT = seq_len // T
    assert seq_len // SEL_BLOCK <= D
    grp = lambda w: jax.ShapeDtypeStruct((batch, G, seq_len, w), BF16)
    gspec = lambda w: pl.BlockSpec((1, G, T, w), lambda b, t: (b, 0, t, 0))
    grp_t = jax.ShapeDtypeStruct((batch, G, D, seq_len), BF16)
    tspec = pl.BlockSpec((1, G, D, T), lambda b, t: (b, 0, 0, t))
    return pl.pallas_call(
        _nsa_prep_kernel,
        grid=(batch, nT),
        in_specs=[pl.BlockSpec((T, y.shape[1]), lambda b, t: (b * nT + t, 0)),
                  pl.BlockSpec((T, D), lambda b, t: (t, 0)),
                  pl.BlockSpec((T, D), lambda b, t: (t, 0))],
        out_specs=[pl.BlockSpec((1, H, T, D), lambda b, t: (b, 0, t, 0)),
                   gspec(D), gspec(D), gspec(2 * D), tspec, gspec(D), tspec],
        out_shape=[jax.ShapeDtypeStruct((batch, H, seq_len, D), BF16),
                   grp(D), grp(D), grp(2 * D), grp_t, grp(D), grp_t],
        compiler_params=_cparams("parallel", "parallel"),
        name="nsa_prep",
    )(y, cos2, sin2)


def _gelu_tanh(x):
    return 0.5 * x * (1.0 + jnp.tanh(math.sqrt(2.0 / math.pi) * (x + 0.044715 * (x * x * x))))


def _compress_kernel(x_ref, pos_ref, w1_ref, w2_ref, o_ref):
    x = x_ref[0].astype(F32)
    half = x.shape[1]
    pos = pos_ref[...]
    top = jnp.dot((x + pos[:, :half]).astype(BF16), w1_ref[0:half, :], preferred_element_type=F32)
    bot = jnp.dot((x + pos[:, half:]).astype(BF16), w1_ref[half:2 * half, :], preferred_element_type=F32)
    n = x.shape[0]
    hid = top + pltpu.roll(bot, n - 1, 0)
    o_ref[0] = jnp.dot(_gelu_tanh(hid).astype(BF16), w2_ref[...], preferred_element_type=F32).astype(BF16)


def nsa_compress_blocks(t, pos, w1, w2):
    B, G, S, D = t.shape
    assert CMP_LEN == 2 * CMP_STRIDE
    n = S // CMP_STRIDE
    x = t.reshape(B * G, n, CMP_STRIDE * D)
    hidden = w1.shape[1]
    return pl.pallas_call(
        _compress_kernel,
        grid=(B * G,),
        in_specs=[pl.BlockSpec((1, n, CMP_STRIDE * D), lambda i: (i, 0, 0)),
                  pl.BlockSpec((1, CMP_LEN * D), lambda i: (0, 0)),
                  pl.BlockSpec((CMP_LEN * D, hidden), lambda i: (0, 0)),
                  pl.BlockSpec((hidden, D), lambda i: (0, 0))],
        out_specs=pl.BlockSpec((1, n, D), lambda i: (i, 0, 0)),
        out_shape=jax.ShapeDtypeStruct((B * G, n, D), BF16),
        compiler_params=_cparams("parallel"),
        name="nsa_compress",
    )(x, pos.reshape(1, CMP_LEN * D).astype(F32), w1.astype(BF16), w2.astype(BF16))


def _gate_col(sig, idx):
    lane = lax.broadcasted_iota(jnp.int32, sig.shape, 1)
    return jnp.sum(jnp.where(lane == idx, sig, 0.0), axis=-1, keepdims=True)


def _cmp_select_kernel(q_ref, kc_ref, vc_ref, gl_ref, ov_ref, o_ref, sb_ref):
    g = pl.program_id(1)
    i = pl.program_id(2)
    R, tq, D = q_ref.shape[1], q_ref.shape[2], q_ref.shape[3]
    nc = kc_ref.shape[1]
    ns = ov_ref.shape[0]
    kc = kc_ref[0]
    vc = vc_ref[0]
    sig = jax.nn.sigmoid(gl_ref[...])
    t_col = i * tq + lax.broadcasted_iota(jnp.int32, (tq, nc), 0)
    c_row = lax.broadcasted_iota(jnp.int32, (tq, nc), 1)
    valid = c_row * CMP_STRIDE + (CMP_LEN - 1) <= t_col
    psum = jnp.zeros((tq, nc), F32)
    for r in range(R):
        s = lax.dot_general(q_ref[0, r], kc, (((1,), (1,)), ((), ())), preferred_element_type=F32)
        s = jnp.where(valid, s, NEG)
        e = jnp.where(valid, jnp.exp(s - jnp.max(s, axis=-1, keepdims=True)), 0.0)
        den = jnp.sum(e, axis=-1, keepdims=True)
        p = e / jnp.where(den > 0.0, den, 1.0)
        psum = psum + p
        o = jnp.dot(p.astype(BF16), vc, preferred_element_type=F32)
        o_ref[:, r * D:(r + 1) * D] = (o * _gate_col(sig, g * R + r)).astype(o_ref.dtype)

    ov = ov_ref[...]
    nt = (((1,), (1,)), ((), ()))
    hi = psum.astype(BF16)
    r1 = psum - hi.astype(F32)
    mid = r1.astype(BF16)
    lo = (r1 - mid.astype(F32)).astype(BF16)
    score = (lax.dot_general(ov, hi, nt, preferred_element_type=F32)
             + (lax.dot_general(ov, mid, nt, preferred_element_type=F32)
                + lax.dot_general(ov, lo, nt, preferred_element_type=F32)))
    n_col = lax.broadcasted_iota(jnp.int32, (ns, tq), 0)
    t_row = i * tq + lax.broadcasted_iota(jnp.int32, (ns, tq), 1)
    cur = t_row // SEL_BLOCK
    forced = (n_col == 0) | (n_col == cur) | (n_col == cur - 1)
    score = jnp.where(n_col * SEL_BLOCK <= t_row, score + jnp.where(forced, FORCE_BONUS, 0.0), NEG)
    rank = jnp.zeros((ns, tq), F32)
    for m in range(ns):
        sm = score[m:m + 1, :]
        ahead = (sm > score) | ((sm == score) & (n_col > m))
        rank = rank + jnp.where(ahead, 1.0, 0.0)
    bias = jnp.where(rank < float(min(SEL_N, ns)), 0.0, NEG)
    bias = jnp.concatenate([bias, jnp.zeros((sb_ref.shape[3] - ns, tq), F32)], axis=0)
    sb_ref[0, 0] = bias.T.astype(sb_ref.dtype)


def _overlap_matrix(nc, ns):
    cs = np.arange(nc) * CMP_STRIDE
    ss = np.arange(ns) * SEL_BLOCK
    ov = np.minimum(cs[None, :] + CMP_LEN, ss[:, None] + SEL_BLOCK) - np.maximum(cs[None, :], ss[:, None])
    ov = np.clip(ov, 0, None) / CMP_STRIDE
    ov[:, nc - CMP_LEN // CMP_STRIDE + 1:] = 0.0
    return jnp.asarray(ov, BF16)


def nsa_cmp_select(q, kcmp, vcmp, gl, batch, seq_len, tq=256):
    H, G, D = NSA_HEADS, NSA_KV, NSA_HD
    R = H // G
    nq = seq_len // tq
    nc = kcmp.shape[1]
    ns = seq_len // SEL_BLOCK
    ov = _overlap_matrix(nc, ns)
    return pl.pallas_call(
        _cmp_select_kernel,
        grid=(batch, G, nq),
        in_specs=[pl.BlockSpec((1, R, tq, D), lambda b, g, i: (b, g, i, 0)),
                  pl.BlockSpec((1, nc, D), lambda b, g, i: (b * G + g, 0, 0)),
                  pl.BlockSpec((1, nc, D), lambda b, g, i: (b * G + g, 0, 0)),
                  pl.BlockSpec((tq, gl.shape[1]), lambda b, g, i: (b * nq + i, 0)),
                  pl.BlockSpec((ns, nc), lambda b, g, i: (0, 0))],
        out_specs=[pl.BlockSpec((tq, R * D), lambda b, g, i: (b * nq + i, g)),
                   pl.BlockSpec((1, 1, tq, D), lambda b, g, i: (b, g, i, 0))],
        out_shape=[jax.ShapeDtypeStruct((batch * seq_len, H * D), BF16),
                   jax.ShapeDtypeStruct((batch, G, seq_len, D), BF16)],
        compiler_params=_cparams("parallel", "parallel", "parallel"),
        name="nsa_cmp_select",
    )(q, kcmp, vcmp, gl, ov)


def _sel_win_kernel(q_ref, sb_ref, ks_ref, vs_ref, kw_ref, vw_ref, gl_ref, o_ref, m_ref, l_ref, acc_ref,
                    *, tk):
    g = pl.program_id(1)
    i = pl.program_id(2)
    R, tq, D = q_ref.shape[1], q_ref.shape[2], q_ref.shape[3]
    cols = R * tq
    H = NSA_HEADS
    nt = (((1,), (1,)), ((), ()))
    q = jnp.concatenate([q_ref[0, r] for r in range(R)], axis=0)
    sb = sb_ref[0, 0]
    q_aug = jnp.concatenate([q, jnp.concatenate([sb] * R, axis=0)], axis=1)
    t_cols = i * tq + (lax.broadcasted_iota(jnp.int32, (1, cols), 1) & (tq - 1))

    m_ref[...] = jnp.full_like(m_ref, NEG)
    l_ref[...] = jnp.zeros_like(l_ref)
    acc_ref[...] = jnp.zeros_like(acc_ref)

    def tile(j, causal):
        off = pl.multiple_of(j * tk, tk)
        k = ks_ref[0, 0, pl.ds(off, tk), :]
        vt = vs_ref[0, 0, :, pl.ds(off, tk)]
        s = lax.dot_general(k, q_aug, nt, preferred_element_type=F32)
        if causal:
            kpos = j * tk + lax.broadcasted_iota(jnp.int32, (tk, cols), 0)
            s = jnp.where(kpos <= t_cols, s, NEG)
        m_old = m_ref[0:1, :]
        m_new = jnp.maximum(m_old, jnp.max(s, axis=0, keepdims=True))
        alpha = jnp.exp(m_old - m_new)
        p = jnp.exp(s - m_new)
        l_ref[0:1, :] = l_ref[0:1, :] * alpha + jnp.sum(p, axis=0, keepdims=True)
        acc_ref[...] = acc_ref[...] * alpha + jnp.dot(vt, p.astype(BF16), preferred_element_type=F32)
        m_ref[0:1, :] = m_new

    last = (i * tq) // tk

    def body(j, carry):
        tile(j, False)
        return carry

    lax.fori_loop(0, last, body, 0)
    tile(last, True)

    sig = jax.nn.sigmoid(gl_ref[...])
    row = lax.broadcasted_iota(jnp.int32, (2 * R, sig.shape[1]), 0)
    lane = lax.broadcasted_iota(jnp.int32, (2 * R, sig.shape[1]), 1)
    pick = jnp.where(lane == H + (row // R) * H + g * R + (row % R), 1.0, 0.0)
    gates = lax.dot_general(pick, sig, nt, preferred_element_type=F32,
                            precision=lax.Precision.HIGHEST)
    g_slc = jnp.concatenate([gates[r:r + 1, :] for r in range(R)], axis=1)
    g_win = jnp.concatenate([gates[R + r:R + r + 1, :] for r in range(R)], axis=1)
    o_t = acc_ref[...] * (g_slc / l_ref[0:1, :])

    wlen = WIN + tq
    start = pl.multiple_of(jnp.maximum(i * tq - WIN, 0), tq)
    kw = kw_ref[0, 0, pl.ds(start, wlen), :]
    vwt = vw_ref[0, 0, :, pl.ds(start, wlen)]
    s = lax.dot_general(kw, q, nt, preferred_element_type=F32)
    kpos = start + lax.broadcasted_iota(jnp.int32, (wlen, cols), 0)
    s = jnp.where((kpos <= t_cols) & (kpos > t_cols - WIN), s, NEG)
    p = jnp.exp(s - jnp.max(s, axis=0, keepdims=True))
    o_w = jnp.dot(vwt, p.astype(BF16), preferred_element_type=F32)
    o_t = o_t + o_w * (g_win / jnp.sum(p, axis=0, keepdims=True))

    for r in range(R):
        o_ref[:, r * D:(r + 1) * D] = o_t[:, r * tq:(r + 1) * tq].T.astype(o_ref.dtype)


def nsa_selected_window(q, sb, ks_aug, vs, kw, vw, gl, batch, seq_len, tq=128, tk=512):
    H, G, D = NSA_HEADS, NSA_KV, NSA_HD
    R = H // G
    nq = seq_len // tq
    assert tk % tq == 0 and WIN % tq == 0 and tq & (tq - 1) == 0 and seq_len >= WIN + tq and tq % 128 == 0
    full = lambda w: pl.BlockSpec((1, 1, seq_len, w), lambda b, g, i: (b, g, 0, 0))
    full_t = pl.BlockSpec((1, 1, D, seq_len), lambda b, g, i: (b, g, 0, 0))
    return pl.pallas_call(
        functools.partial(_sel_win_kernel, tk=tk),
        grid=(batch, G, nq),
        in_specs=[pl.BlockSpec((1, R, tq, D), lambda b, g, i: (b, g, i, 0)),
                  pl.BlockSpec((1, 1, tq, D), lambda b, g, i: (b, g, i, 0)),
                  full(2 * D), full_t, full(D), full_t,
                  pl.BlockSpec((tq, gl.shape[1]), lambda b, g, i: (b * nq + i, 0))],
        out_specs=pl.BlockSpec((tq, R * D), lambda b, g, i: (b * nq + i, g)),
        out_shape=jax.ShapeDtypeStruct((batch * seq_len, H * D), BF16),
        scratch_shapes=[pltpu.VMEM((8, R * tq), F32), pltpu.VMEM((8, R * tq), F32),
                        pltpu.VMEM((D, R * tq), F32)],
        compiler_params=_cparams("parallel", "parallel", "arbitrary"),
        name="nsa_selected_window",
    )(q, sb, ks_aug, vs, kw, vw, gl)


def nsa_mixer_parts(y, gl, p, i, cos, sin, batch, seq_len):
    D = NSA_HD
    cos2 = jnp.concatenate([cos, cos], axis=1)
    sin2 = jnp.concatenate([-sin, sin], axis=1)
    q, kc, vc, ks_aug, vs, kw, vw = nsa_prep(y, cos2, sin2, batch, seq_len)
    kcmp = nsa_compress_blocks(kc, p["cmp_pos_k"][i], p["cmp_w1_k"][i], p["cmp_w2_k"][i])
    vcmp = nsa_compress_blocks(vc, p["cmp_pos_v"][i], p["cmp_w1_v"][i], p["cmp_w2_v"][i])
    o_cmp, sb = nsa_cmp_select(q, kcmp, vcmp, gl, batch, seq_len)
    o_sw = nsa_selected_window(q, sb, ks_aug, vs, kw, vw, gl, batch, seq_len)
    return o_cmp, o_sw


def split_cols(t, widths):
    offs = np.cumsum(widths)[:-1].tolist()
    return jnp.split(t, offs, axis=-1)


def rmsnorm(x, w):
    xf = x.astype(jnp.float32)
    y = xf * lax.rsqrt(jnp.mean(xf * xf, axis=-1, keepdims=True) + EPS)
    return (y * w.astype(jnp.float32)).astype(x.dtype)


def l2norm(x):
    return x * lax.rsqrt(jnp.sum(x * x, axis=-1, keepdims=True) + EPS)


def causal_dwconv(x, w):
    K = w.shape[0]
    S = x.shape[1]
    xp = jnp.pad(x, ((0, 0), (K - 1, 0), (0, 0)))
    return sum(xp[:, k:k + S] * w[k] for k in range(K))


def rope_tables(S, dim):
    inv = 1.0 / (ROPE_THETA ** (jnp.arange(0, dim, 2, dtype=jnp.float32) / dim))
    ang = jnp.arange(S, dtype=jnp.float32)[:, None] * inv[None, :]
    return jnp.cos(ang), jnp.sin(ang)


def apply_rope(x, cos, sin):
    x1, x2 = jnp.split(x, 2, axis=-1)
    c = cos[None, :, None, :]
    s = sin[None, :, None, :]
    return jnp.concatenate([x1 * c - x2 * s, x2 * c + x1 * s], axis=-1)


def gated_delta_chunked(q, k, v, g, beta):
    B, H, S, D = q.shape
    C = GDN_CHUNK
    N = S // C
    q = q * (D ** -0.5)
    rs = lambda t: t.reshape((B, H, N, C) + t.shape[3:])
    q, k, v, beta = rs(q), rs(k), rs(v), rs(beta)
    g = jnp.cumsum(rs(g), axis=-1)
    kb = k * beta[..., None]
    vb = v * beta[..., None]
    tril = jnp.tril(jnp.ones((C, C), dtype=bool))
    stril = jnp.tril(jnp.ones((C, C), dtype=bool), -1)
    gdiff = g[..., :, None] - g[..., None, :]
    decay = jnp.where(tril, jnp.exp(jnp.where(tril, gdiff, 0.0)), 0.0)
    Lm = jnp.where(stril, jnp.einsum('bhnid,bhnjd->bhnij', kb, k) * decay, 0.0)
    eye = jnp.eye(C, dtype=jnp.float32)
    T = lax.linalg.triangular_solve(eye + Lm, jnp.broadcast_to(eye, Lm.shape),
                                    left_side=True, lower=True)
    u = T @ vb
    w = T @ (kb * jnp.exp(g)[..., None])
    qk = jnp.where(tril, jnp.einsum('bhnid,bhnjd->bhnij', q, k) * decay, 0.0)

    def step(state, xs):
        qc, kc, uc, wc, gc, ac = xs
        vnew = uc - wc @ state
        o = (qc * jnp.exp(gc)[..., None]) @ state + ac @ vnew
        glast = gc[..., -1]
        kdec = kc * jnp.exp(glast[..., None] - gc)[..., None]
        state = state * jnp.exp(glast)[..., None, None] + jnp.einsum('bhcd,bhce->bhde', kdec, vnew)
        return state, o

    xs = tuple(jnp.moveaxis(t, 2, 0) for t in (q, k, u, w, g, qk))
    s0 = jnp.zeros((B, H, D, D), jnp.float32)
    _, o = lax.scan(step, s0, xs)
    return jnp.moveaxis(o, 0, 2).reshape(B, H, S, D)


def delta_shortconv_core(y, conv_qkv, a_log, dt_bias, o_norm, conv_sc):
    B, S, _ = y.shape
    H, D = GDN_HEADS, GDN_HD
    SCW = H * D
    f32 = jnp.float32
    q, k, v, z, a, b, hb, gb, gc = split_cols(y, [H * D] * 4 + [H] * 2 + [SCW] * 3)
    qkv = jax.nn.silu(causal_dwconv(jnp.concatenate([q, k, v], axis=-1), conv_qkv)).astype(f32)
    q, k, v = [t.reshape(B, S, H, D).transpose(0, 2, 1, 3) for t in jnp.split(qkv, 3, axis=-1)]
    g = -jnp.exp(a_log.astype(f32)) * jax.nn.softplus(a.astype(f32) + dt_bias.astype(f32))
    beta = jax.nn.sigmoid(b.astype(f32))
    o = gated_delta_chunked(l2norm(q), l2norm(k), v, g.transpose(0, 2, 1), beta.transpose(0, 2, 1))
    o = rmsnorm(o.transpose(0, 2, 1, 3), o_norm) * jax.nn.silu(z.astype(f32).reshape(B, S, H, D))
    y_a = o.reshape(B, S, H * D)
    y_b = gb * causal_dwconv(gc * hb, conv_sc)
    return jnp.concatenate([y_a, y_b], axis=-1)


def nsa_compress(t, pos, w1, w2):
    B, G, S, D = t.shape
    r = CMP_LEN // CMP_STRIDE
    nc = S // CMP_STRIDE - r + 1
    tr = t.reshape(B, G, S // CMP_STRIDE, CMP_STRIDE, D)
    blocks = jnp.concatenate([tr[:, :, j:j + nc] for j in range(r)], axis=3)
    blocks = (blocks + pos).reshape(B, G, nc, CMP_LEN * D)
    return jax.nn.gelu(blocks @ w1) @ w2


def nsa_compressed_branch(q, kc, vc):
    S = q.shape[3]
    nc = kc.shape[2]
    t = jnp.arange(S)
    valid = (jnp.arange(nc) * CMP_STRIDE + CMP_LEN - 1)[None, :] <= t[:, None]
    s = jnp.einsum('bgrsd,bgcd->bgrsc', q, kc)
    p = jnp.where(valid, jax.nn.softmax(jnp.where(valid, s, NEG), axis=-1), 0.0)
    return jnp.einsum('bgrsc,bgcd->bgrsd', p, vc), p


def nsa_selected_branch(q, k, v, p_cmp):
    B, G, R, S, D = q.shape
    nc = p_cmp.shape[-1]
    ns = S // SEL_BLOCK
    n_sel = min(SEL_N, ns)
    cs = jnp.arange(nc) * CMP_STRIDE
    ss = jnp.arange(ns) * SEL_BLOCK
    ov = jnp.minimum(cs[:, None] + CMP_LEN, ss[None, :] + SEL_BLOCK) - jnp.maximum(cs[:, None], ss[None, :])
    M = (jnp.clip(ov, 0, None) / CMP_STRIDE).astype(jnp.float32)
    score = jnp.einsum('bgrsc,cn->bgsn', p_cmp, M)
    t = jnp.arange(S)
    blk = jnp.arange(ns)
    cur = (t // SEL_BLOCK)[:, None]
    valid = blk[None, :] * SEL_BLOCK <= t[:, None]
    forced = (blk[None, :] == 0) | (blk[None, :] == cur) | (blk[None, :] == cur - 1)
    score = jnp.where(valid, score + jnp.where(forced, FORCE_BONUS, 0.0), NEG)
    _, idx = lax.top_k(score, n_sel)
    sel = jnp.sum(jax.nn.one_hot(idx, ns, dtype=jnp.float32), axis=-2) > 0
    s = jnp.einsum('bgrsd,bgkd->bgrsk', q, k)
    mask = jnp.repeat(sel, SEL_BLOCK, axis=-1) & (t[None, :] <= t[:, None])
    p = jax.nn.softmax(jnp.where(mask[:, :, None], s, NEG), axis=-1)
    return jnp.einsum('bgrsk,bgkd->bgrsd', p, v)


def nsa_window_branch(q, k, v):
    S = q.shape[3]
    t = jnp.arange(S)
    mask = (t[None, :] <= t[:, None]) & (t[None, :] > t[:, None] - WIN)
    s = jnp.einsum('bgrsd,bgkd->bgrsk', q, k)
    p = jax.nn.softmax(jnp.where(mask, s, NEG), axis=-1)
    return jnp.einsum('bgrsk,bgkd->bgrsd', p, v)


def nsa_core(y, pos_k, w1_k, w2_k, pos_v, w1_v, w2_v, cos, sin):
    B, S, _ = y.shape
    H, G, D = NSA_HEADS, NSA_KV, NSA_HD
    R = H // G
    f32 = jnp.float32
    q, kc, vc, ks, vs, kw, vw, gt = split_cols(y, [H * D] + [G * D] * 6 + [3 * H])
    q = apply_rope(q.reshape(B, S, H, D), cos, sin) * (D ** -0.5)
    q = q.reshape(B, S, G, R, D).transpose(0, 2, 3, 1, 4)
    rk = lambda t: apply_rope(t.reshape(B, S, G, D), cos, sin).transpose(0, 2, 1, 3)
    rv = lambda t: t.reshape(B, S, G, D).transpose(0, 2, 1, 3)
    kc, ks, kw = rk(kc), rk(ks), rk(kw)
    vc, vs, vw = rv(vc), rv(vs), rv(vw)
    gates = jax.nn.sigmoid(gt).reshape(B, S, 3, G, R).transpose(2, 0, 3, 4, 1)[..., None]
    kcmp = nsa_compress(kc, pos_k.astype(f32), w1_k.astype(f32), w2_k.astype(f32))
    vcmp = nsa_compress(vc, pos_v.astype(f32), w1_v.astype(f32), w2_v.astype(f32))
    o_cmp, p_cmp = nsa_compressed_branch(q, kcmp, vcmp)
    o_slc = nsa_selected_branch(q, ks, vs, p_cmp)
    o_win = nsa_window_branch(q, kw, vw)
    o = gates[0] * o_cmp + gates[1] * o_slc + gates[2] * o_win
    return o.transpose(0, 3, 1, 2, 4).reshape(B, S, H * D)


def kernel(x, norm_w, ab_w_in, ab_w_out, gdn_conv, gdn_a_log, gdn_dt_bias, gdn_norm, sc_conv, nsa_w_in, nsa_w_out, cmp_pos_k, cmp_w1_k, cmp_w2_k, cmp_pos_v, cmp_w1_v, cmp_w2_v, ffn_w_gu, ffn_conv, ffn_w_down):
    B, S, D = x.shape
    p = dict(norm_w=norm_w, ab_w_in=ab_w_in, ab_w_out=ab_w_out, gdn_conv=gdn_conv, gdn_a_log=gdn_a_log,
             gdn_dt_bias=gdn_dt_bias, gdn_norm=gdn_norm, sc_conv=sc_conv, nsa_w_in=nsa_w_in,
             nsa_w_out=nsa_w_out, cmp_pos_k=cmp_pos_k, cmp_w1_k=cmp_w1_k, cmp_w2_k=cmp_w2_k,
             cmp_pos_v=cmp_pos_v, cmp_w1_v=cmp_w1_v, cmp_w2_v=cmp_w2_v, ffn_w_gu=ffn_w_gu,
             ffn_conv=ffn_conv, ffn_w_down=ffn_w_down)
    cos, sin = rope_tables(S, NSA_HD)
    h = x.reshape(B * S, D)
    for l in range(norm_w.shape[0]):
        _, h = run_layer(h, l, B, S, cos, sin, p)
    return h.reshape(B, S, D)


def run_layer(h, l, B, S, cos, sin, p):
    D = h.shape[1]
    i = l // 2
    norm_w = p["norm_w"]
    if l % 2 == 0:
        gw = GDN_HEADS * GDN_HD
        w = p["ab_w_in"][i]
        w_main = jnp.concatenate([w[:, :4 * gw], w[:, 4 * gw + 2 * GDN_HEADS:]], axis=1).astype(BF16)
        w_ab = jnp.pad(w[:, 4 * gw:4 * gw + 2 * GDN_HEADS], ((0, 0), (0, 128 - 2 * GDN_HEADS))).astype(BF16)
        y, ab = norm_matmul(h, norm_w[l, 0], w_main, w_ab)
        y_a = gdn_heads(y, ab, p["gdn_conv"][i], p["gdn_a_log"][i], p["gdn_dt_bias"][i], p["gdn_norm"][i], B, S)
        y_b = short_conv(y, 4 * gw, p["sc_conv"][i], B, S)
        h1 = out_proj([y_a, y_b], p["ab_w_out"][i].astype(BF16), h, norm_w[l, 1], concat=True)
    else:
        w = p["nsa_w_in"][i]
        nm = NSA_HEADS * NSA_HD + 6 * NSA_KV * NSA_HD
        w_gl = jnp.pad(w[:, nm:], ((0, 0), (0, 128 - (w.shape[1] - nm)))).astype(BF16)
        y, gl = norm_matmul(h, norm_w[l, 0], w[:, :nm].astype(BF16), w_gl)
        o_cmp, o_sw = nsa_mixer_parts(y, gl, p, i, cos, sin, B, S)
        h1 = out_proj([o_cmp, o_sw], p["nsa_w_out"][i].astype(BF16), h, norm_w[l, 1], concat=False)
    h2 = conv_ffn_block(h1, S, norm_w[l, 2], p["ffn_w_gu"][l].astype(BF16), p["ffn_conv"][l],
                        p["ffn_w_down"][l].astype(BF16), norm_w[l, 3])
    return h1, h2
```

```python
import functools
import math

import jax
import jax.numpy as jnp
import numpy as np
from jax import lax
from jax.experimental import pallas as pl
from jax.experimental.pallas import tpu as pltpu

F32 = jnp.float32
BF16 = jnp.bfloat16

EPS = 1e-6
NEG = -1e30

GDN_HEADS = 8
GDN_HD = 128
GDN_CHUNK = 64
NSA_HEADS = 16
NSA_KV = 4
NSA_HD = 128
CMP_LEN = 32
CMP_STRIDE = 16
SEL_BLOCK = 64
SEL_N = 16
WIN = 512
FORCE_BONUS = 1000.0
ROPE_THETA = 10000.0
LOG2E = math.log2(math.e)
COL_GROUPS = 2

VMEM_LIMIT_BYTES = 56 * 1024 * 1024
BF16_ROWS = 16


def _cparams(*sem, vmem=VMEM_LIMIT_BYTES):
    return pltpu.CompilerParams(dimension_semantics=sem, vmem_limit_bytes=vmem)


def _rms(x, w):
    return x * lax.rsqrt(jnp.mean(x * x, axis=-1, keepdims=True) + EPS) * w


def _norm_mm_kernel(h_ref, nw_ref, w_ref, ws_ref, o_ref, os_ref, xn_ref):
    @pl.when(pl.program_id(1) == 0)
    def _():
        xn_ref[...] = _rms(h_ref[...], nw_ref[...]).astype(BF16)
        os_ref[...] = jnp.dot(xn_ref[...], ws_ref[...], preferred_element_type=F32)

    o_ref[...] = jnp.dot(xn_ref[...], w_ref[...], preferred_element_type=F32).astype(o_ref.dtype)


def norm_matmul(h, nw, w, w_small, tm=1024, tn=1024):
    M, D = h.shape
    N = w.shape[1]
    Ns = w_small.shape[1]
    tn = min(tn, N)
    return pl.pallas_call(
        _norm_mm_kernel,
        grid=(M // tm, N // tn),
        in_specs=[pl.BlockSpec((tm, D), lambda i, j: (i, 0)),
                  pl.BlockSpec((1, D), lambda i, j: (0, 0)),
                  pl.BlockSpec((D, tn), lambda i, j: (0, j)),
                  pl.BlockSpec((D, Ns), lambda i, j: (0, 0))],
        out_specs=[pl.BlockSpec((tm, tn), lambda i, j: (i, j)),
                   pl.BlockSpec((tm, Ns), lambda i, j: (i, 0))],
        out_shape=[jax.ShapeDtypeStruct((M, N), BF16), jax.ShapeDtypeStruct((M, Ns), F32)],
        scratch_shapes=[pltpu.VMEM((tm, D), BF16)],
        compiler_params=_cparams("parallel", "arbitrary"),
        name="norm_matmul",
    )(h, nw.reshape(1, D), w, w_small)


def _out_proj_kernel(*refs, n_parts, concat):
    parts = refs[:n_parts]
    w_ref, h_ref, nw_ref, o_ref = refs[n_parts:]
    if concat:
        x = jnp.concatenate([p[...] for p in parts], axis=1)
    else:
        x = parts[0][...].astype(F32)
        for p in parts[1:]:
            x = x + p[...].astype(F32)
        x = x.astype(BF16)
    m = jnp.dot(x, w_ref[...], preferred_element_type=F32)
    o_ref[...] = h_ref[...] + _rms(m, nw_ref[...])


def out_proj(parts, w, h, nw, concat, tm=512):
    M, D = h.shape
    K = w.shape[0]
    in_specs = [pl.BlockSpec((tm, p.shape[1]), lambda i: (i, 0)) for p in parts]
    in_specs += [pl.BlockSpec((K, D), lambda i: (0, 0)),
                 pl.BlockSpec((tm, D), lambda i: (i, 0)),
                 pl.BlockSpec((1, D), lambda i: (0, 0))]
    return pl.pallas_call(
        functools.partial(_out_proj_kernel, n_parts=len(parts), concat=concat),
        grid=(M // tm,),
        in_specs=in_specs,
        out_specs=pl.BlockSpec((tm, D), lambda i: (i, 0)),
        out_shape=jax.ShapeDtypeStruct((M, D), F32),
        compiler_params=_cparams("parallel"),
        name="out_proj",
    )(*parts, w, h, nw.reshape(1, D))


def _ffn_kernel(h_ref, halo_ref, nw2_ref, wg_ref, wu_ref, cw_ref, wd_ref, nw3_ref, o_ref,
                xn_ref, g_ref, *, tiles_per_seq):
    i = pl.program_id(0)
    j = pl.program_id(1)
    tm = h_ref.shape[0]
    H = BF16_ROWS

    @pl.when(j == 0)
    def _():
        xn_ref[pl.ds(H, tm), :] = _rms(h_ref[...], nw2_ref[...]).astype(BF16)
        halo = _rms(halo_ref[...], nw2_ref[...])
        halo = jnp.where(i % tiles_per_seq == 0, 0.0, halo)
        xn_ref[pl.ds(0, H), :] = halo.astype(BF16)

    g_ref[...] = jnp.dot(xn_ref[...], wg_ref[...], preferred_element_type=F32)
    u = jnp.dot(xn_ref[pl.ds(H, tm), :], wu_ref[...], preferred_element_type=F32)
    cw = cw_ref[...]
    c = (g_ref[pl.ds(H, tm), :] * cw[2:3, :] + g_ref[pl.ds(H - 1, tm), :] * cw[1:2, :]
         + g_ref[pl.ds(H - 2, tm), :] * cw[0:1, :])
    hid = (c * jax.nn.sigmoid(c) * u).astype(BF16)
    part = jnp.dot(hid, wd_ref[...], preferred_element_type=F32)

    @pl.when(j == 0)
    def _():
        o_ref[...] = part

    @pl.when(j > 0)
    def _():
        o_ref[...] += part

    @pl.when(j == pl.num_programs(1) - 1)
    def _():
        o_ref[...] = h_ref[...] + _rms(o_ref[...], nw3_ref[...])


def conv_ffn_block(h, seq_len, nw2, w_gu, conv_w, w_down, nw3, tm=512, tf=512):
    M, D = h.shape
    FF = w_down.shape[0]
    nf = FF // tf
    H = BF16_ROWS
    kern = functools.partial(_ffn_kernel, tiles_per_seq=seq_len // tm)
    return pl.pallas_call(
        kern,
        grid=(M // tm, nf),
        in_specs=[pl.BlockSpec((tm, D), lambda i, j: (i, 0)),
                  pl.BlockSpec((H, D), lambda i, j: (jnp.maximum(i * (tm // H) - 1, 0), 0)),
                  pl.BlockSpec((1, D), lambda i, j: (0, 0)),
                  pl.BlockSpec((D, tf), lambda i, j: (0, j)),
                  pl.BlockSpec((D, tf), lambda i, j: (0, nf + j)),
                  pl.BlockSpec((3, tf), lambda i, j: (0, j)),
                  pl.BlockSpec((tf, D), lambda i, j: (j, 0)),
                  pl.BlockSpec((1, D), lambda i, j: (0, 0))],
        out_specs=pl.BlockSpec((tm, D), lambda i, j: (i, 0)),
        out_shape=jax.ShapeDtypeStruct((M, D), F32),
        scratch_shapes=[pltpu.VMEM((tm + H, D), BF16),
                        pltpu.VMEM((tm + H, tf), F32)],
        compiler_params=_cparams("parallel", "arbitrary"),
        name="conv_ffn",
    )(h, h, nw2.reshape(1, D), w_gu, w_gu, conv_w, w_down, nw3.reshape(1, D))


CONV_HALO = 8


def _silu(x):
    return x * jax.nn.sigmoid(x)


def _causal_conv_ext(ext_ref, x, cw, first):
    T = x.shape[0]
    K = cw.shape[0]

    @pl.when(first)
    def _():
        ext_ref[pl.ds(0, CONV_HALO), :] = jnp.zeros((CONV_HALO, x.shape[1]), F32)

    @pl.when(jnp.logical_not(first))
    def _():
        ext_ref[pl.ds(0, CONV_HALO), :] = ext_ref[pl.ds(T, CONV_HALO), :]

    ext_ref[pl.ds(CONV_HALO, T), :] = x
    y = ext_ref[pl.ds(CONV_HALO, T), :] * cw[K - 1:K, :]
    for k in range(K - 1):
        y = y + ext_ref[pl.ds(CONV_HALO - (K - 1) + k, T), :] * cw[k:k + 1, :]
    return y


def _split_bf16(x):
    hi = x.astype(BF16)
    return hi, (x - hi.astype(F32)).astype(BF16)


def _dot_3pass(a, b):
    ah, al = _split_bf16(a)
    bh, bl = _split_bf16(b)
    d = functools.partial(jnp.dot, preferred_element_type=F32)
    return d(ah, bh) + (d(ah, bl) + d(al, bh))


def _dot_1pass(a, b):
    return jnp.dot(a.astype(BF16), b.astype(BF16), preferred_element_type=F32)


TRI_DOT = _dot_1pass


def _tri_inverse(lm, block):
    n = lm.shape[0]
    ii = lax.broadcasted_iota(jnp.int32, (n, n), 0)
    jj = lax.broadcasted_iota(jnp.int32, (n, n), 1)
    x = jnp.where(ii == jj, 1.0, 0.0)
    s = 1
    while s < block:
        off = (ii // (2 * s) == jj // (2 * s)) & (ii // s != jj // s)
        coff = jnp.where(off, lm, 0.0)
        if s == 1:
            x = x - coff
        else:
            x = x - TRI_DOT(TRI_DOT(x, coff), x)
        s *= 2
    return x


def _gdn_kernel(q_ref, k_ref, v_ref, z_ref, ab_ref, cq_ref, ck_ref, cv_ref, alog_ref, dtb_ref, on_ref,
                o_ref, qe_ref, ke_ref, ve_ref, st_ref):
    t = pl.program_id(2)
    T = q_ref.shape[0]
    D = GDN_HD
    HP = q_ref.shape[1] // D
    C = GDN_CHUNK
    first = t == 0
    nt = (((1,), (1,)), ((), ()))

    @pl.when(first)
    def _():
        st_ref[...] = jnp.zeros_like(st_ref)

    q_all = _silu(_causal_conv_ext(qe_ref, q_ref[...].astype(F32), cq_ref[...], first))
    k_all = _silu(_causal_conv_ext(ke_ref, k_ref[...].astype(F32), ck_ref[...], first))
    v_all = _silu(_causal_conv_ext(ve_ref, v_ref[...].astype(F32), cv_ref[...], first))

    ab = ab_ref[...]
    lane = lax.broadcasted_iota(jnp.int32, ab.shape, 1)
    x = ab + dtb_ref[...]
    g_all = -jnp.exp(alog_ref[...]) * (jnp.maximum(x, 0.0) + jnp.log1p(jnp.exp(-jnp.abs(x))))
    g_all = jnp.where(lane < GDN_HEADS, g_all, 0.0)
    sig_ab = jax.nn.sigmoid(ab)
    ti = lax.broadcasted_iota(jnp.int32, (T, T), 0)
    tj = lax.broadcasted_iota(jnp.int32, (T, T), 1)
    tril = (ti >= tj) & (ti // C == tj // C)
    stril = tril & (ti > tj)
    gc_all = jnp.dot(jnp.where(tril, 1.0, 0.0), g_all, preferred_element_type=F32,
                     precision=lax.Precision.HIGHEST)
    lane8 = lax.broadcasted_iota(jnp.int32, (8, ab.shape[1]), 1)

    heads = range(HP)
    dot = functools.partial(jnp.dot, preferred_element_type=F32)
    hid = [pl.program_id(1) * HP + hh for hh in heads]
    cs = [slice(hh * D, (hh + 1) * D) for hh in heads]
    q = [q_all[:, c] for c in cs]
    k = [k_all[:, c] for c in cs]
    q = [x * lax.rsqrt(jnp.sum(x * x, axis=-1, keepdims=True) + EPS) * (D ** -0.5) for x in q]
    k = [x * lax.rsqrt(jnp.sum(x * x, axis=-1, keepdims=True) + EPS) for x in k]
    gcol = [jnp.sum(jnp.where(lane == h, gc_all, 0.0), axis=-1, keepdims=True) for h in hid]
    beta = [jnp.sum(jnp.where(lane == GDN_HEADS + h, sig_ab, 0.0), axis=-1, keepdims=True) for h in hid]
    grow = [lax.dot_general(jnp.where(lane8 == h, 1.0, 0.0), gc_all, nt, preferred_element_type=F32,
                            precision=lax.Precision.HIGHEST)[0:1, :] for h in hid]

    kb = [k[i] * beta[i] for i in heads]
    eg = [jnp.exp(gcol[i]) for i in heads]
    decay = [jnp.where(tril, jnp.exp(jnp.where(tril, gcol[i] - grow[i], 0.0)), 0.0) for i in heads]
    qk = [lax.dot_general(jnp.concatenate([q[i], kb[i]], axis=0).astype(BF16), k[i].astype(BF16), nt,
                          preferred_element_type=F32) for i in heads]
    a_qk = [(qk[i][:T] * decay[i]).astype(BF16) for i in heads]
    lm = [jnp.where(stril, qk[i][T:] * decay[i], 0.0) for i in heads]

    inv = [jnp.where(ti == tj, 1.0, 0.0) - jnp.where(ti // 2 == tj // 2, lm[i], 0.0) for i in heads]
    s = 2
    while s < C:
        off = (ti // (2 * s) == tj // (2 * s)) & (ti // s != tj // s)
        xb = [inv[i].astype(BF16) for i in heads]
        xc = [dot(xb[i], jnp.where(off, lm[i], 0.0).astype(BF16)) for i in heads]
        inv = [inv[i] - dot(xc[i].astype(BF16), xb[i]) for i in heads]
        s *= 2

    uw = [dot(inv[i].astype(BF16), jnp.concatenate([v_all[:, cs[i]] * beta[i], kb[i] * eg[i]], axis=1).astype(BF16))
          for i in heads]
    qg = [q[i] * eg[i] for i in heads]

    state = [st_ref[i] for i in heads]
    for c in range(T // C):
        r = slice(c * C, (c + 1) * C)
        ws = [dot(jnp.concatenate([uw[i][r, D:], qg[i][r]], axis=0).astype(BF16), state[i].astype(BF16))
              for i in heads]
        vnew = [(uw[i][r, :D] - ws[i][:C]).astype(BF16) for i in heads]
        glast = [gcol[i][c * C + C - 1:(c + 1) * C, :] for i in heads]
        kdec = [(k[i][r] * jnp.exp(glast[i] - gcol[i][r])).T.astype(BF16) for i in heads]
        state = [state[i] * jnp.exp(glast[i]) + dot(kdec[i], vnew[i]) for i in heads]
        for i in heads:
            o = ws[i][C:] + dot(a_qk[i][r, c * C:(c + 1) * C], vnew[i])
            zc = z_ref[pl.ds(c * C, C), cs[i]].astype(F32)
            o_ref[pl.ds(c * C, C), cs[i]] = (_rms(o, on_ref[...]) * _silu(zc)).astype(o_ref.dtype)
    for i in heads:
        st_ref[i] = state[i]


def gdn_heads(y, ab, conv_qkv, a_log, dt_bias, o_norm, batch, seq_len, T=256, heads_per_step=4):
    H, D = GDN_HEADS, GDN_HD
    HP = heads_per_step
    W = HP * D
    nG = H // HP
    nT = seq_len // T
    lanes = ab.shape[1]
    pad = lambda p: jnp.pad(p.astype(F32), (0, lanes - p.shape[0])).reshape(1, lanes)
    row = lambda b, h, t: b * nT + t
    col = lambda off: (lambda b, h, t: (row(b, h, t), off * nG + h))
    cw = lambda off: (lambda b, h, t: (0, off * nG + h))
    const = lambda b, h, t: (0, 0)
    K = conv_qkv.shape[0]
    return pl.pallas_call(
        _gdn_kernel,
        grid=(batch, nG, nT),
        in_specs=[pl.BlockSpec((T, W), col(0)), pl.BlockSpec((T, W), col(1)),
                  pl.BlockSpec((T, W), col(2)), pl.BlockSpec((T, W), col(3)),
                  pl.BlockSpec((T, lanes), lambda b, h, t: (row(b, h, t), 0)),
                  pl.BlockSpec((K, W), cw(0)), pl.BlockSpec((K, W), cw(1)), pl.BlockSpec((K, W), cw(2)),
                  pl.BlockSpec((1, lanes), const), pl.BlockSpec((1, lanes), const),
                  pl.BlockSpec((1, D), const)],
        out_specs=pl.BlockSpec((T, W), col(0)),
        out_shape=jax.ShapeDtypeStruct((batch * seq_len, H * D), BF16),
        scratch_shapes=[pltpu.VMEM((T + CONV_HALO, W), F32), pltpu.VMEM((T + CONV_HALO, W), F32),
                        pltpu.VMEM((T + CONV_HALO, W), F32), pltpu.VMEM((HP, D, D), F32)],
        compiler_params=_cparams("parallel", "parallel", "arbitrary"),
        name="gdn_heads",
    )(y, y, y, y, ab, conv_qkv, conv_qkv, conv_qkv, pad(a_log), pad(dt_bias), o_norm.reshape(1, D))


def _shortconv_kernel(hb_ref, gb_ref, gc_ref, cw_ref, o_ref, ext_ref):
    first = pl.program_id(2) == 0
    x = gc_ref[...].astype(F32) * hb_ref[...].astype(F32)
    y = _causal_conv_ext(ext_ref, x, cw_ref[...], first)
    o_ref[...] = (gb_ref[...].astype(F32) * y).astype(o_ref.dtype)


def short_conv(y, col_off, conv_sc, batch, seq_len, T=512, tc=512):
    W = conv_sc.shape[1]
    nT = seq_len // T
    nC = W // tc
    spec = lambda off: pl.BlockSpec((T, tc), lambda b, c, t: (b * nT + t, (col_off + off) // tc + c))
    return pl.pallas_call(
        _shortconv_kernel,
        grid=(batch, nC, nT),
        in_specs=[spec(0), spec(W), spec(2 * W),
                  pl.BlockSpec((conv_sc.shape[0], tc), lambda b, c, t: (0, c))],
        out_specs=pl.BlockSpec((T, tc), lambda b, c, t: (b * nT + t, c)),
        out_shape=jax.ShapeDtypeStruct((batch * seq_len, W), BF16),
        scratch_shapes=[pltpu.VMEM((T + CONV_HALO, tc), F32)],
        compiler_params=_cparams("parallel", "parallel", "arbitrary"),
        name="short_conv",
    )(y, y, y, conv_sc)


def _nsa_prep_kernel(y_ref, cos_ref, sin_ref, q_ref, kc_ref, vc_ref, ks_ref, vs_ref, kw_ref, vw_ref):
    T = y_ref.shape[0]
    H, G, D = NSA_HEADS, NSA_KV, NSA_HD
    cos = cos_ref[...]
    sin = sin_ref[...]

    def head(c):
        return y_ref[:, c * D:(c + 1) * D]

    def rope(x):
        x = x.astype(F32)
        return x * cos + pltpu.roll(x, D // 2, 1) * sin

    for hh in range(H):
        q_ref[0, hh] = (rope(head(hh)) * (D ** -0.5 * LOG2E)).astype(BF16)
    t0 = pl.program_id(1) * T
    tok = t0 + lax.broadcasted_iota(jnp.int32, (T, D), 0)
    lane = lax.broadcasted_iota(jnp.int32, (T, D), 1)
    onehot = jnp.where(tok // SEL_BLOCK == lane, 1.0, 0.0).astype(BF16)
    for g in range(G):
        kc_ref[0, g] = rope(head(H + g)).astype(BF16)
        vc_ref[0, g] = head(H + G + g)
        ks_ref[0, g, :, 0:D] = rope(head(H + 2 * G + g)).astype(BF16)
        ks_ref[0, g, :, D:2 * D] = onehot
        vs_ref[0, g] = head(H + 3 * G + g).astype(F32).T.astype(BF16)
        kw_ref[0, g] = rope(head(H + 4 * G + g)).astype(BF16)
        vw_ref[0, g] = head(H + 5 * G + g).astype(F32).T.astype(BF16)


def nsa_prep(y, cos2, sin2, batch, seq_len, T=512):
    H, G, D = NSA_HEADS, NSA_KV, NSA_HD
    nT = seq_len // T
    assert seq_len // SEL_BLOCK <= D
    grp = lambda w: jax.ShapeDtypeStruct((batch, G, seq_len, w), BF16)
    gspec = lambda w: pl.BlockSpec((1, G, T, w), lambda b, t: (b, 0, t, 0))
    grp_t = jax.ShapeDtypeStruct((batch, G, D, seq_len), BF16)
    tspec = pl.BlockSpec((1, G, D, T), lambda b, t: (b, 0, 0, t))
    return pl.pallas_call(
        _nsa_prep_kernel,
        grid=(batch, nT),
        in_specs=[pl.BlockSpec((T, y.shape[1]), lambda b, t: (b * nT + t, 0)),
                  pl.BlockSpec((T, D), lambda b, t: (t, 0)),
                  pl.BlockSpec((T, D), lambda b, t: (t, 0))],
        out_specs=[pl.BlockSpec((1, H, T, D), lambda b, t: (b, 0, t, 0)),
                   gspec(D), gspec(D), gspec(2 * D), tspec, gspec(D), tspec],
        out_shape=[jax.ShapeDtypeStruct((batch, H, seq_len, D), BF16),
                   grp(D), grp(D), grp(2 * D), grp_t, grp(D), grp_t],
        compiler_params=_cparams("parallel", "parallel"),
        name="nsa_prep",
    )(y, cos2, sin2)


def _gelu_tanh(x):
    return 0.5 * x * (1.0 + jnp.tanh(math.sqrt(2.0 / math.pi) * (x + 0.044715 * (x * x * x))))


def _compress_kernel(x_ref, pos_ref, w1_ref, w2_ref, o_ref):
    x = x_ref[0].astype(F32)
    half = x.shape[1]
    pos = pos_ref[...]
    top = jnp.dot((x + pos[:, :half]).astype(BF16), w1_ref[0:half, :], preferred_element_type=F32)
    bot = jnp.dot((x + pos[:, half:]).astype(BF16), w1_ref[half:2 * half, :], preferred_element_type=F32)
    n = x.shape[0]
    hid = top + pltpu.roll(bot, n - 1, 0)
    o_ref[0] = jnp.dot(_gelu_tanh(hid).astype(BF16), w2_ref[...], preferred_element_type=F32).astype(BF16)


def nsa_compress_blocks(t, pos, w1, w2):
    B, G, S, D = t.shape
    assert CMP_LEN == 2 * CMP_STRIDE
    n = S // CMP_STRIDE
    x = t.reshape(B * G, n, CMP_STRIDE * D)
    hidden = w1.shape[1]
    return pl.pallas_call(
        _compress_kernel,
        grid=(B * G,),
        in_specs=[pl.BlockSpec((1, n, CMP_STRIDE * D), lambda i: (i, 0, 0)),
                  pl.BlockSpec((1, CMP_LEN * D), lambda i: (0, 0)),
                  pl.BlockSpec((CMP_LEN * D, hidden), lambda i: (0, 0)),
                  pl.BlockSpec((hidden, D), lambda i: (0, 0))],
        out_specs=pl.BlockSpec((1, n, D), lambda i: (i, 0, 0)),
        out_shape=jax.ShapeDtypeStruct((B * G, n, D), BF16),
        compiler_params=_cparams("parallel"),
        name="nsa_compress",
    )(x, pos.reshape(1, CMP_LEN * D).astype(F32), w1.astype(BF16), w2.astype(BF16))


def _gate_col(sig, idx):
    lane = lax.broadcasted_iota(jnp.int32, sig.shape, 1)
    return jnp.sum(jnp.where(lane == idx, sig, 0.0), axis=-1, keepdims=True)


def _cmp_select_kernel(q_ref, kc_ref, vc_ref, gl_ref, ov_ref, o_ref, sb_ref):
    g = pl.program_id(1)
    i = pl.program_id(2)
    R, tq, D = q_ref.shape[1], q_ref.shape[2], q_ref.shape[3]
    nc = kc_ref.shape[1]
    ns = ov_ref.shape[0]
    kc = kc_ref[0]
    vc = vc_ref[0]
    sig = jax.nn.sigmoid(gl_ref[...])
    t_col = i * tq + lax.broadcasted_iota(jnp.int32, (tq, nc), 0)
    c_row = lax.broadcasted_iota(jnp.int32, (tq, nc), 1)
    valid = c_row * CMP_STRIDE + (CMP_LEN - 1) <= t_col
    psum = jnp.zeros((tq, nc), F32)
    for r in range(R):
        s = lax.dot_general(q_ref[0, r], kc, (((1,), (1,)), ((), ())), preferred_element_type=F32)
        s = jnp.where(valid, s, NEG)
        e = jnp.where(valid, jnp.exp2(s - jnp.max(s, axis=-1, keepdims=True)), 0.0)
        den = jnp.sum(e, axis=-1, keepdims=True)
        p = e / jnp.where(den > 0.0, den, 1.0)
        psum = psum + p
        o = jnp.dot(p.astype(BF16), vc, preferred_element_type=F32)
        o_ref[:, r * D:(r + 1) * D] = (o * _gate_col(sig, g * R + r)).astype(o_ref.dtype)

    ov = ov_ref[...]
    nt = (((1,), (1,)), ((), ()))
    hi = psum.astype(BF16)
    r1 = psum - hi.astype(F32)
    mid = r1.astype(BF16)
    lo = (r1 - mid.astype(F32)).astype(BF16)
    score = (lax.dot_general(ov, hi, nt, preferred_element_type=F32)
             + (lax.dot_general(ov, mid, nt, preferred_element_type=F32)
                + lax.dot_general(ov, lo, nt, preferred_element_type=F32)))
    n_col = lax.broadcasted_iota(jnp.int32, (ns, tq), 0)
    t_row = i * tq + lax.broadcasted_iota(jnp.int32, (ns, tq), 1)
    cur = t_row // SEL_BLOCK
    forced = (n_col == 0) | (n_col == cur) | (n_col == cur - 1)
    score = jnp.where(n_col * SEL_BLOCK <= t_row, score + jnp.where(forced, FORCE_BONUS, 0.0), NEG)
    rank = jnp.zeros((ns, tq), F32)
    for m in range(ns):
        sm = score[m:m + 1, :]
        ahead = (sm > score) | ((sm == score) & (n_col > m))
        rank = rank + jnp.where(ahead, 1.0, 0.0)
    bias = jnp.where(rank < float(min(SEL_N, ns)), 0.0, NEG)
    bias = jnp.concatenate([bias, jnp.zeros((sb_ref.shape[3] - ns, tq), F32)], axis=0)
    sb_ref[0, 0] = bias.T.astype(sb_ref.dtype)


def _overlap_matrix(nc, ns):
    cs = np.arange(nc) * CMP_STRIDE
    ss = np.arange(ns) * SEL_BLOCK
    ov = np.minimum(cs[None, :] + CMP_LEN, ss[:, None] + SEL_BLOCK) - np.maximum(cs[None, :], ss[:, None])
    ov = np.clip(ov, 0, None) / CMP_STRIDE
    ov[:, nc - CMP_LEN // CMP_STRIDE + 1:] = 0.0
    return jnp.asarray(ov, BF16)


def nsa_cmp_select(q, kcmp, vcmp, gl, batch, seq_len, tq=256):
    H, G, D = NSA_HEADS, NSA_KV, NSA_HD
    R = H // G
    nq = seq_len // tq
    nc = kcmp.shape[1]
    ns = seq_len // SEL_BLOCK
    ov = _overlap_matrix(nc, ns)
    return pl.pallas_call(
        _cmp_select_kernel,
        grid=(batch, G, nq),
        in_specs=[pl.BlockSpec((1, R, tq, D), lambda b, g, i: (b, g, i, 0)),
                  pl.BlockSpec((1, nc, D), lambda b, g, i: (b * G + g, 0, 0)),
                  pl.BlockSpec((1, nc, D), lambda b, g, i: (b * G + g, 0, 0)),
                  pl.BlockSpec((tq, gl.shape[1]), lambda b, g, i: (b * nq + i, 0)),
                  pl.BlockSpec((ns, nc), lambda b, g, i: (0, 0))],
        out_specs=[pl.BlockSpec((tq, R * D), lambda b, g, i: (b * nq + i, g)),
                   pl.BlockSpec((1, 1, tq, D), lambda b, g, i: (b, g, i, 0))],
        out_shape=[jax.ShapeDtypeStruct((batch * seq_len, H * D), BF16),
                   jax.ShapeDtypeStruct((batch, G, seq_len, D), BF16)],
        compiler_params=_cparams("parallel", "parallel", "parallel"),
        name="nsa_cmp_select",
    )(q, kcmp, vcmp, gl, ov)


def _sel_win_kernel(q_ref, sb_ref, ks_ref, vs_ref, kw_ref, vw_ref, gl_ref, o_ref, m_ref, l_ref, acc_ref,
                    s_ref, *, tk):
    g = pl.program_id(1)
    i = pl.program_id(2)
    R, tq, D = q_ref.shape[1], q_ref.shape[2], q_ref.shape[3]
    cols = R * tq
    H = NSA_HEADS
    nt = (((1,), (1,)), ((), ()))
    q = jnp.concatenate([q_ref[0, r] for r in range(R)], axis=0)
    sb = sb_ref[0, 0]
    q_aug = jnp.concatenate([q, jnp.concatenate([sb] * R, axis=0)], axis=1)
    t_cols = i * tq + (lax.broadcasted_iota(jnp.int32, (1, cols), 1) & (tq - 1))

    m_ref[...] = jnp.full_like(m_ref, NEG)
    l_ref[...] = jnp.zeros_like(l_ref)
    acc_ref[...] = jnp.zeros_like(acc_ref)

    gw = cols // COL_GROUPS
    groups = [slice(c * gw, (c + 1) * gw) for c in range(COL_GROUPS)]
    q_grp = [q_aug[c] for c in groups]
    dot = functools.partial(jnp.dot, preferred_element_type=F32)

    def scores(j):
        k = ks_ref[0, 0, pl.ds(pl.multiple_of(j * tk, tk), tk), :]
        return [lax.dot_general(k, qa, nt, preferred_element_type=F32) for qa in q_grp]

    def accumulate(j, s):
        vt = vs_ref[0, 0, :, pl.ds(pl.multiple_of(j * tk, tk), tk)]
        m_old = [m_ref[0:1, c] for c in groups]
        m_new = [jnp.maximum(mo, jnp.max(x, axis=0, keepdims=True)) for mo, x in zip(m_old, s)]
        alpha = [jnp.exp2(mo - mn) for mo, mn in zip(m_old, m_new)]
        p = [jnp.exp2(x - mn) for x, mn in zip(s, m_new)]
        pv = [dot(vt, x.astype(BF16)) for x in p]
        for n, c in enumerate(groups):
            l_ref[0:1, c] = l_ref[0:1, c] * alpha[n] + jnp.sum(p[n], axis=0, keepdims=True)
            acc_ref[:, c] = acc_ref[:, c] * alpha[n] + pv[n]
            m_ref[0:1, c] = m_new[n]

    last = (i * tq) // tk
    for n, c in enumerate(groups):
        s_ref[:, c] = scores(0)[n]

    def body(j, carry):
        s_next = scores(j + 1)
        accumulate(j, [s_ref[:, c] for c in groups])
        for n, c in enumerate(groups):
            s_ref[:, c] = s_next[n]
        return carry

    lax.fori_loop(0, last, body, 0)
    kpos = last * tk + lax.broadcasted_iota(jnp.int32, (tk, gw), 0)
    accumulate(last, [jnp.where(kpos <= t_cols[:, c], s_ref[:, c], NEG) for c in groups])

    sig = jax.nn.sigmoid(gl_ref[...])
    row = lax.broadcasted_iota(jnp.int32, (2 * R, sig.shape[1]), 0)
    lane = lax.broadcasted_iota(jnp.int32, (2 * R, sig.shape[1]), 1)
    pick = jnp.where(lane == H + (row // R) * H + g * R + (row % R), 1.0, 0.0)
    gates = lax.dot_general(pick, sig, nt, preferred_element_type=F32,
                            precision=lax.Precision.HIGHEST)
    g_slc = jnp.concatenate([gates[r:r + 1, :] for r in range(R)], axis=1)
    g_win = jnp.concatenate([gates[R + r:R + r + 1, :] for r in range(R)], axis=1)
    o_t = acc_ref[...] * (g_slc / l_ref[0:1, :])

    wlen = WIN + tq
    start = pl.multiple_of(jnp.maximum(i * tq - WIN, 0), tq)
    kw = kw_ref[0, 0, pl.ds(start, wlen), :]
    vwt = vw_ref[0, 0, :, pl.ds(start, wlen)]
    kpos = start + lax.broadcasted_iota(jnp.int32, (wlen, gw), 0)
    s = [lax.dot_general(kw, q[c], nt, preferred_element_type=F32) for c in groups]
    s = [jnp.where((kpos <= t_cols[:, c]) & (kpos > t_cols[:, c] - WIN), x, NEG) for x, c in zip(s, groups)]
    p = [jnp.exp2(x - jnp.max(x, axis=0, keepdims=True)) for x in s]
    o_w = [dot(vwt, x.astype(BF16)) * (g_win[:, c] / jnp.sum(x, axis=0, keepdims=True))
           for x, c in zip(p, groups)]
    o_t = o_t + jnp.concatenate(o_w, axis=1)

    for r in range(R):
        o_ref[:, r * D:(r + 1) * D] = o_t[:, r * tq:(r + 1) * tq].T.astype(o_ref.dtype)


def nsa_selected_window(q, sb, ks_aug, vs, kw, vw, gl, batch, seq_len, tq=128, tk=512):
    H, G, D = NSA_HEADS, NSA_KV, NSA_HD
    R = H // G
    nq = seq_len // tq
    assert tk % tq == 0 and WIN % tq == 0 and tq & (tq - 1) == 0 and seq_len >= WIN + tq and tq % 128 == 0
    full = lambda w: pl.BlockSpec((1, 1, seq_len, w), lambda b, g, i: (b, g, 0, 0))
    full_t = pl.BlockSpec((1, 1, D, seq_len), lambda b, g, i: (b, g, 0, 0))
    return pl.pallas_call(
        functools.partial(_sel_win_kernel, tk=tk),
        grid=(batch, G, nq),
        in_specs=[pl.BlockSpec((1, R, tq, D), lambda b, g, i: (b, g, i, 0)),
                  pl.BlockSpec((1, 1, tq, D), lambda b, g, i: (b, g, i, 0)),
                  full(2 * D), full_t, full(D), full_t,
                  pl.BlockSpec((tq, gl.shape[1]), lambda b, g, i: (b * nq + i, 0))],
        out_specs=pl.BlockSpec((tq, R * D), lambda b, g, i: (b * nq + i, g)),
        out_shape=jax.ShapeDtypeStruct((batch * seq_len, H * D), BF16),
        scratch_shapes=[pltpu.VMEM((8, R * tq), F32), pltpu.VMEM((8, R * tq), F32),
                        pltpu.VMEM((D, R * tq), F32), pltpu.VMEM((tk, R * tq), F32)],
        compiler_params=_cparams("parallel", "parallel", "arbitrary"),
        name="nsa_selected_window",
    )(q, sb, ks_aug, vs, kw, vw, gl)


def nsa_mixer_parts(y, gl, p, i, cos, sin, batch, seq_len):
    D = NSA_HD
    cos2 = jnp.concatenate([cos, cos], axis=1)
    sin2 = jnp.concatenate([-sin, sin], axis=1)
    q, kc, vc, ks_aug, vs, kw, vw = nsa_prep(y, cos2, sin2, batch, seq_len)
    kcmp = nsa_compress_blocks(kc, p["cmp_pos_k"][i], p["cmp_w1_k"][i], p["cmp_w2_k"][i])
    vcmp = nsa_compress_blocks(vc, p["cmp_pos_v"][i], p["cmp_w1_v"][i], p["cmp_w2_v"][i])
    o_cmp, sb = nsa_cmp_select(q, kcmp, vcmp, gl, batch, seq_len)
    o_sw = nsa_selected_window(q, sb, ks_aug, vs, kw, vw, gl, batch, seq_len)
    return o_cmp, o_sw


def split_cols(t, widths):
    offs = np.cumsum(widths)[:-1].tolist()
    return jnp.split(t, offs, axis=-1)


def rmsnorm(x, w):
    xf = x.astype(jnp.float32)
    y = xf * lax.rsqrt(jnp.mean(xf * xf, axis=-1, keepdims=True) + EPS)
    return (y * w.astype(jnp.float32)).astype(x.dtype)


def l2norm(x):
    return x * lax.rsqrt(jnp.sum(x * x, axis=-1, keepdims=True) + EPS)


def causal_dwconv(x, w):
    K = w.shape[0]
    S = x.shape[1]
    xp = jnp.pad(x, ((0, 0), (K - 1, 0), (0, 0)))
    return sum(xp[:, k:k + S] * w[k] for k in range(K))


def rope_tables(S, dim):
    inv = 1.0 / (ROPE_THETA ** (jnp.arange(0, dim, 2, dtype=jnp.float32) / dim))
    ang = jnp.arange(S, dtype=jnp.float32)[:, None] * inv[None, :]
    return jnp.cos(ang), jnp.sin(ang)


def apply_rope(x, cos, sin):
    x1, x2 = jnp.split(x, 2, axis=-1)
    c = cos[None, :, None, :]
    s = sin[None, :, None, :]
    return jnp.concatenate([x1 * c - x2 * s, x2 * c + x1 * s], axis=-1)


def gated_delta_chunked(q, k, v, g, beta):
    B, H, S, D = q.shape
    C = GDN_CHUNK
    N = S // C
    q = q * (D ** -0.5)
    rs = lambda t: t.reshape((B, H, N, C) + t.shape[3:])
    q, k, v, beta = rs(q), rs(k), rs(v), rs(beta)
    g = jnp.cumsum(rs(g), axis=-1)
    kb = k * beta[..., None]
    vb = v * beta[..., None]
    tril = jnp.tril(jnp.ones((C, C), dtype=bool))
    stril = jnp.tril(jnp.ones((C, C), dtype=bool), -1)
    gdiff = g[..., :, None] - g[..., None, :]
    decay = jnp.where(tril, jnp.exp(jnp.where(tril, gdiff, 0.0)), 0.0)
    Lm = jnp.where(stril, jnp.einsum('bhnid,bhnjd->bhnij', kb, k) * decay, 0.0)
    eye = jnp.eye(C, dtype=jnp.float32)
    T = lax.linalg.triangular_solve(eye + Lm, jnp.broadcast_to(eye, Lm.shape),
                                    left_side=True, lower=True)
    u = T @ vb
    w = T @ (kb * jnp.exp(g)[..., None])
    qk = jnp.where(tril, jnp.einsum('bhnid,bhnjd->bhnij', q, k) * decay, 0.0)

    def step(state, xs):
        qc, kc, uc, wc, gc, ac = xs
        vnew = uc - wc @ state
        o = (qc * jnp.exp(gc)[..., None]) @ state + ac @ vnew
        glast = gc[..., -1]
        kdec = kc * jnp.exp(glast[..., None] - gc)[..., None]
        state = state * jnp.exp(glast)[..., None, None] + jnp.einsum('bhcd,bhce->bhde', kdec, vnew)
        return state, o

    xs = tuple(jnp.moveaxis(t, 2, 0) for t in (q, k, u, w, g, qk))
    s0 = jnp.zeros((B, H, D, D), jnp.float32)
    _, o = lax.scan(step, s0, xs)
    return jnp.moveaxis(o, 0, 2).reshape(B, H, S, D)


def delta_shortconv_core(y, conv_qkv, a_log, dt_bias, o_norm, conv_sc):
    B, S, _ = y.shape
    H, D = GDN_HEADS, GDN_HD
    SCW = H * D
    f32 = jnp.float32
    q, k, v, z, a, b, hb, gb, gc = split_cols(y, [H * D] * 4 + [H] * 2 + [SCW] * 3)
    qkv = jax.nn.silu(causal_dwconv(jnp.concatenate([q, k, v], axis=-1), conv_qkv)).astype(f32)
    q, k, v = [t.reshape(B, S, H, D).transpose(0, 2, 1, 3) for t in jnp.split(qkv, 3, axis=-1)]
    g = -jnp.exp(a_log.astype(f32)) * jax.nn.softplus(a.astype(f32) + dt_bias.astype(f32))
    beta = jax.nn.sigmoid(b.astype(f32))
    o = gated_delta_chunked(l2norm(q), l2norm(k), v, g.transpose(0, 2, 1), beta.transpose(0, 2, 1))
    o = rmsnorm(o.transpose(0, 2, 1, 3), o_norm) * jax.nn.silu(z.astype(f32).reshape(B, S, H, D))
    y_a = o.reshape(B, S, H * D)
    y_b = gb * causal_dwconv(gc * hb, conv_sc)
    return jnp.concatenate([y_a, y_b], axis=-1)


def nsa_compress(t, pos, w1, w2):
    B, G, S, D = t.shape
    r = CMP_LEN // CMP_STRIDE
    nc = S // CMP_STRIDE - r + 1
    tr = t.reshape(B, G, S // CMP_STRIDE, CMP_STRIDE, D)
    blocks = jnp.concatenate([tr[:, :, j:j + nc] for j in range(r)], axis=3)
    blocks = (blocks + pos).reshape(B, G, nc, CMP_LEN * D)
    return jax.nn.gelu(blocks @ w1) @ w2


def nsa_compressed_branch(q, kc, vc):
    S = q.shape[3]
    nc = kc.shape[2]
    t = jnp.arange(S)
    valid = (jnp.arange(nc) * CMP_STRIDE + CMP_LEN - 1)[None, :] <= t[:, None]
    s = jnp.einsum('bgrsd,bgcd->bgrsc', q, kc)
    p = jnp.where(valid, jax.nn.softmax(jnp.where(valid, s, NEG), axis=-1), 0.0)
    return jnp.einsum('bgrsc,bgcd->bgrsd', p, vc), p


def nsa_selected_branch(q, k, v, p_cmp):
    B, G, R, S, D = q.shape
    nc = p_cmp.shape[-1]
    ns = S // SEL_BLOCK
    n_sel = min(SEL_N, ns)
    cs = jnp.arange(nc) * CMP_STRIDE
    ss = jnp.arange(ns) * SEL_BLOCK
    ov = jnp.minimum(cs[:, None] + CMP_LEN, ss[None, :] + SEL_BLOCK) - jnp.maximum(cs[:, None], ss[None, :])
    M = (jnp.clip(ov, 0, None) / CMP_STRIDE).astype(jnp.float32)
    score = jnp.einsum('bgrsc,cn->bgsn', p_cmp, M)
    t = jnp.arange(S)
    blk = jnp.arange(ns)
    cur = (t // SEL_BLOCK)[:, None]
    valid = blk[None, :] * SEL_BLOCK <= t[:, None]
    forced = (blk[None, :] == 0) | (blk[None, :] == cur) | (blk[None, :] == cur - 1)
    score = jnp.where(valid, score + jnp.where(forced, FORCE_BONUS, 0.0), NEG)
    _, idx = lax.top_k(score, n_sel)
    sel = jnp.sum(jax.nn.one_hot(idx, ns, dtype=jnp.float32), axis=-2) > 0
    s = jnp.einsum('bgrsd,bgkd->bgrsk', q, k)
    mask = jnp.repeat(sel, SEL_BLOCK, axis=-1) & (t[None, :] <= t[:, None])
    p = jax.nn.softmax(jnp.where(mask[:, :, None], s, NEG), axis=-1)
    return jnp.einsum('bgrsk,bgkd->bgrsd', p, v)


def nsa_window_branch(q, k, v):
    S = q.shape[3]
    t = jnp.arange(S)
    mask = (t[None, :] <= t[:, None]) & (t[None, :] > t[:, None] - WIN)
    s = jnp.einsum('bgrsd,bgkd->bgrsk', q, k)
    p = jax.nn.softmax(jnp.where(mask, s, NEG), axis=-1)
    return jnp.einsum('bgrsk,bgkd->bgrsd', p, v)


def nsa_core(y, pos_k, w1_k, w2_k, pos_v, w1_v, w2_v, cos, sin):
    B, S, _ = y.shape
    H, G, D = NSA_HEADS, NSA_KV, NSA_HD
    R = H // G
    f32 = jnp.float32
    q, kc, vc, ks, vs, kw, vw, gt = split_cols(y, [H * D] + [G * D] * 6 + [3 * H])
    q = apply_rope(q.reshape(B, S, H, D), cos, sin) * (D ** -0.5)
    q = q.reshape(B, S, G, R, D).transpose(0, 2, 3, 1, 4)
    rk = lambda t: apply_rope(t.reshape(B, S, G, D), cos, sin).transpose(0, 2, 1, 3)
    rv = lambda t: t.reshape(B, S, G, D).transpose(0, 2, 1, 3)
    kc, ks, kw = rk(kc), rk(ks), rk(kw)
    vc, vs, vw = rv(vc), rv(vs), rv(vw)
    gates = jax.nn.sigmoid(gt).reshape(B, S, 3, G, R).transpose(2, 0, 3, 4, 1)[..., None]
    kcmp = nsa_compress(kc, pos_k.astype(f32), w1_k.astype(f32), w2_k.astype(f32))
    vcmp = nsa_compress(vc, pos_v.astype(f32), w1_v.astype(f32), w2_v.astype(f32))
    o_cmp, p_cmp = nsa_compressed_branch(q, kcmp, vcmp)
    o_slc = nsa_selected_branch(q, ks, vs, p_cmp)
    o_win = nsa_window_branch(q, kw, vw)
    o = gates[0] * o_cmp + gates[1] * o_slc + gates[2] * o_win
    return o.transpose(0, 3, 1, 2, 4).reshape(B, S, H * D)


def kernel(x, norm_w, ab_w_in, ab_w_out, gdn_conv, gdn_a_log, gdn_dt_bias, gdn_norm, sc_conv, nsa_w_in, nsa_w_out, cmp_pos_k, cmp_w1_k, cmp_w2_k, cmp_pos_v, cmp_w1_v, cmp_w2_v, ffn_w_gu, ffn_conv, ffn_w_down):
    B, S, D = x.shape
    p = dict(norm_w=norm_w, ab_w_in=ab_w_in, ab_w_out=ab_w_out, gdn_conv=gdn_conv, gdn_a_log=gdn_a_log,
             gdn_dt_bias=gdn_dt_bias, gdn_norm=gdn_norm, sc_conv=sc_conv, nsa_w_in=nsa_w_in,
             nsa_w_out=nsa_w_out, cmp_pos_k=cmp_pos_k, cmp_w1_k=cmp_w1_k, cmp_w2_k=cmp_w2_k,
             cmp_pos_v=cmp_pos_v, cmp_w1_v=cmp_w1_v, cmp_w2_v=cmp_w2_v, ffn_w_gu=ffn_w_gu,
             ffn_conv=ffn_conv, ffn_w_down=ffn_w_down)
    cos, sin = rope_tables(S, NSA_HD)
    h = x.reshape(B * S, D)
    for l in range(norm_w.shape[0]):
        _, h = run_layer(h, l, B, S, cos, sin, p)
    return h.reshape(B, S, D)


def run_layer(h, l, B, S, cos, sin, p):
    D = h.shape[1]
    i = l // 2
    norm_w = p["norm_w"]
    if l % 2 == 0:
        gw = GDN_HEADS * GDN_HD
        w = p["ab_w_in"][i]
        w_main = jnp.concatenate([w[:, :4 * gw], w[:, 4 * gw + 2 * GDN_HEADS:]], axis=1).astype(BF16)
        w_ab = jnp.pad(w[:, 4 * gw:4 * gw + 2 * GDN_HEADS], ((0, 0), (0, 128 - 2 * GDN_HEADS))).astype(BF16)
        y, ab = norm_matmul(h, norm_w[l, 0], w_main, w_ab)
        y_a = gdn_heads(y, ab, p["gdn_conv"][i], p["gdn_a_log"][i], p["gdn_dt_bias"][i], p["gdn_norm"][i], B, S)
        y_b = short_conv(y, 4 * gw, p["sc_conv"][i], B, S)
        h1 = out_proj([y_a, y_b], p["ab_w_out"][i].astype(BF16), h, norm_w[l, 1], concat=True)
    else:
        w = p["nsa_w_in"][i]
        nm = NSA_HEADS * NSA_HD + 6 * NSA_KV * NSA_HD
        w_gl = jnp.pad(w[:, nm:], ((0, 0), (0, 128 - (w.shape[1] - nm)))).astype(BF16)
        y, gl = norm_matmul(h, norm_w[l, 0], w[:, :nm].astype(BF16), w_gl)
        o_cmp, o_sw = nsa_mixer_parts(y, gl, p, i, cos, sin, B, S)
        h1 = out_proj([o_cmp, o_sw], p["nsa_w_out"][i].astype(BF16), h, norm_w[l, 1], concat=False)
    h2 = conv_ffn_block(h1, S, norm_w[l, 2], p["ffn_w_gu"][l].astype(BF16), p["ffn_conv"][l],
                        p["ffn_w_down"][l].astype(BF16), norm_w[l, 3])
    return h1, h2
```

```python
import functools
import math

import jax
import jax.numpy as jnp
import numpy as np
from jax import lax
from jax.experimental import pallas as pl
from jax.experimental.pallas import tpu as pltpu

F32 = jnp.float32
BF16 = jnp.bfloat16

EPS = 1e-6
NEG = -1e30

GDN_HEADS = 8
GDN_HD = 128
GDN_CHUNK = 64
NSA_HEADS = 16
NSA_KV = 4
NSA_HD = 128
CMP_LEN = 32
CMP_STRIDE = 16
SEL_BLOCK = 64
SEL_N = 16
WIN = 512
FORCE_BONUS = 1000.0
ROPE_THETA = 10000.0
LOG2E = math.log2(math.e)
COL_GROUPS = 4

VMEM_LIMIT_BYTES = 56 * 1024 * 1024
BF16_ROWS = 16


def _cparams(*sem, vmem=VMEM_LIMIT_BYTES):
    return pltpu.CompilerParams(dimension_semantics=sem, vmem_limit_bytes=vmem)


def _rms(x, w):
    return x * lax.rsqrt(jnp.mean(x * x, axis=-1, keepdims=True) + EPS) * w


def _norm_mm_kernel(h_ref, nw_ref, w_ref, ws_ref, o_ref, os_ref, xn_ref):
    @pl.when(pl.program_id(1) == 0)
    def _():
        xn_ref[...] = _rms(h_ref[...], nw_ref[...]).astype(BF16)
        os_ref[...] = jnp.dot(xn_ref[...], ws_ref[...], preferred_element_type=F32)

    o_ref[...] = jnp.dot(xn_ref[...], w_ref[...], preferred_element_type=F32).astype(o_ref.dtype)


def norm_matmul(h, nw, w, w_small, tm=1024, tn=1024):
    M, D = h.shape
    N = w.shape[1]
    Ns = w_small.shape[1]
    tn = min(tn, N)
    return pl.pallas_call(
        _norm_mm_kernel,
        grid=(M // tm, N // tn),
        in_specs=[pl.BlockSpec((tm, D), lambda i, j: (i, 0)),
                  pl.BlockSpec((1, D), lambda i, j: (0, 0)),
                  pl.BlockSpec((D, tn), lambda i, j: (0, j)),
                  pl.BlockSpec((D, Ns), lambda i, j: (0, 0))],
        out_specs=[pl.BlockSpec((tm, tn), lambda i, j: (i, j)),
                   pl.BlockSpec((tm, Ns), lambda i, j: (i, 0))],
        out_shape=[jax.ShapeDtypeStruct((M, N), BF16), jax.ShapeDtypeStruct((M, Ns), F32)],
        scratch_shapes=[pltpu.VMEM((tm, D), BF16)],
        compiler_params=_cparams("parallel", "arbitrary"),
        name="norm_matmul",
    )(h, nw.reshape(1, D), w, w_small)


def _out_proj_kernel(*refs, n_parts, concat):
    parts = refs[:n_parts]
    w_ref, h_ref, nw_ref, o_ref = refs[n_parts:]
    if concat:
        x = jnp.concatenate([p[...] for p in parts], axis=1)
    else:
        x = parts[0][...].astype(F32)
        for p in parts[1:]:
            x = x + p[...].astype(F32)
        x = x.astype(BF16)
    m = jnp.dot(x, w_ref[...], preferred_element_type=F32)
    o_ref[...] = h_ref[...] + _rms(m, nw_ref[...])


def out_proj(parts, w, h, nw, concat, tm=512):
    M, D = h.shape
    K = w.shape[0]
    in_specs = [pl.BlockSpec((tm, p.shape[1]), lambda i: (i, 0)) for p in parts]
    in_specs += [pl.BlockSpec((K, D), lambda i: (0, 0)),
                 pl.BlockSpec((tm, D), lambda i: (i, 0)),
                 pl.BlockSpec((1, D), lambda i: (0, 0))]
    return pl.pallas_call(
        functools.partial(_out_proj_kernel, n_parts=len(parts), concat=concat),
        grid=(M // tm,),
        in_specs=in_specs,
        out_specs=pl.BlockSpec((tm, D), lambda i: (i, 0)),
        out_shape=jax.ShapeDtypeStruct((M, D), F32),
        compiler_params=_cparams("parallel"),
        name="out_proj",
    )(*parts, w, h, nw.reshape(1, D))


def _ffn_kernel(h_ref, halo_ref, nw2_ref, wg_ref, wu_ref, cw_ref, wd_ref, nw3_ref, o_ref,
                xn_ref, g_ref, acc_ref, *, tiles_per_seq):
    i = pl.program_id(0)
    j = pl.program_id(1)
    tm = h_ref.shape[0]
    H = BF16_ROWS

    @pl.when(j == 0)
    def _():
        xn_ref[pl.ds(H, tm), :] = _rms(h_ref[...], nw2_ref[...]).astype(BF16)
        halo = _rms(halo_ref[...], nw2_ref[...])
        halo = jnp.where(i % tiles_per_seq == 0, 0.0, halo)
        xn_ref[pl.ds(0, H), :] = halo.astype(BF16)
        acc_ref[...] = jnp.zeros_like(acc_ref)

    g_ref[...] = jnp.dot(xn_ref[...], wg_ref[...], preferred_element_type=F32)
    u = jnp.dot(xn_ref[pl.ds(H, tm), :], wu_ref[...], preferred_element_type=F32)
    cw = cw_ref[...]
    c = (g_ref[pl.ds(H, tm), :] * cw[2:3, :] + g_ref[pl.ds(H - 1, tm), :] * cw[1:2, :]
         + g_ref[pl.ds(H - 2, tm), :] * cw[0:1, :])
    hid = (c * jax.nn.sigmoid(c) * u).astype(BF16)
    acc_ref[...] += jnp.dot(hid, wd_ref[...], preferred_element_type=F32)

    @pl.when(j == pl.num_programs(1) - 1)
    def _():
        o_ref[...] = h_ref[...] + _rms(acc_ref[...], nw3_ref[...])


def conv_ffn_block(h, seq_len, nw2, w_gu, conv_w, w_down, nw3, tm=512, tf=512):
    M, D = h.shape
    FF = w_down.shape[0]
    nf = FF // tf
    H = BF16_ROWS
    kern = functools.partial(_ffn_kernel, tiles_per_seq=seq_len // tm)
    return pl.pallas_call(
        kern,
        grid=(M // tm, nf),
        in_specs=[pl.BlockSpec((tm, D), lambda i, j: (i, 0)),
                  pl.BlockSpec((H, D), lambda i, j: (jnp.maximum(i * (tm // H) - 1, 0), 0)),
                  pl.BlockSpec((1, D), lambda i, j: (0, 0)),
                  pl.BlockSpec((D, tf), lambda i, j: (0, j)),
                  pl.BlockSpec((D, tf), lambda i, j: (0, nf + j)),
                  pl.BlockSpec((3, tf), lambda i, j: (0, j)),
                  pl.BlockSpec((tf, D), lambda i, j: (j, 0)),
                  pl.BlockSpec((1, D), lambda i, j: (0, 0))],
        out_specs=pl.BlockSpec((tm, D), lambda i, j: (i, 0)),
        out_shape=jax.ShapeDtypeStruct((M, D), F32),
        scratch_shapes=[pltpu.VMEM((tm + H, D), BF16),
                        pltpu.VMEM((tm + H, tf), F32),
                        pltpu.VMEM((tm, D), F32)],
        compiler_params=_cparams("parallel", "arbitrary"),
        name="conv_ffn",
    )(h, h, nw2.reshape(1, D), w_gu, w_gu, conv_w, w_down, nw3.reshape(1, D))


CONV_HALO = 8


def _silu(x):
    return x * jax.nn.sigmoid(x)


def _causal_conv_ext(ext_ref, x, cw, first):
    T = x.shape[0]
    K = cw.shape[0]

    @pl.when(first)
    def _():
        ext_ref[pl.ds(0, CONV_HALO), :] = jnp.zeros((CONV_HALO, x.shape[1]), F32)

    @pl.when(jnp.logical_not(first))
    def _():
        ext_ref[pl.ds(0, CONV_HALO), :] = ext_ref[pl.ds(T, CONV_HALO), :]

    ext_ref[pl.ds(CONV_HALO, T), :] = x
    y = ext_ref[pl.ds(CONV_HALO, T), :] * cw[K - 1:K, :]
    for k in range(K - 1):
        y = y + ext_ref[pl.ds(CONV_HALO - (K - 1) + k, T), :] * cw[k:k + 1, :]
    return y


def _split_bf16(x):
    hi = x.astype(BF16)
    return hi, (x - hi.astype(F32)).astype(BF16)


def _dot_3pass(a, b):
    ah, al = _split_bf16(a)
    bh, bl = _split_bf16(b)
    d = functools.partial(jnp.dot, preferred_element_type=F32)
    return d(ah, bh) + (d(ah, bl) + d(al, bh))


def _dot_1pass(a, b):
    return jnp.dot(a.astype(BF16), b.astype(BF16), preferred_element_type=F32)


TRI_DOT = _dot_1pass


def _tri_inverse(lm, block):
    n = lm.shape[0]
    ii = lax.broadcasted_iota(jnp.int32, (n, n), 0)
    jj = lax.broadcasted_iota(jnp.int32, (n, n), 1)
    x = jnp.where(ii == jj, 1.0, 0.0)
    s = 1
    while s < block:
        off = (ii // (2 * s) == jj // (2 * s)) & (ii // s != jj // s)
        coff = jnp.where(off, lm, 0.0)
        if s == 1:
            x = x - coff
        else:
            x = x - TRI_DOT(TRI_DOT(x, coff), x)
        s *= 2
    return x


def _gdn_kernel(q_ref, k_ref, v_ref, z_ref, ab_ref, cq_ref, ck_ref, cv_ref, alog_ref, dtb_ref, on_ref,
                o_ref, qe_ref, ke_ref, ve_ref, st_ref):
    t = pl.program_id(2)
    T = q_ref.shape[0]
    D = GDN_HD
    HP = q_ref.shape[1] // D
    C = GDN_CHUNK
    first = t == 0
    nt = (((1,), (1,)), ((), ()))

    @pl.when(first)
    def _():
        st_ref[...] = jnp.zeros_like(st_ref)

    q_all = _silu(_causal_conv_ext(qe_ref, q_ref[...].astype(F32), cq_ref[...], first))
    k_all = _silu(_causal_conv_ext(ke_ref, k_ref[...].astype(F32), ck_ref[...], first))
    v_all = _silu(_causal_conv_ext(ve_ref, v_ref[...].astype(F32), cv_ref[...], first))

    ab = ab_ref[...]
    lane = lax.broadcasted_iota(jnp.int32, ab.shape, 1)
    x = ab + dtb_ref[...]
    g_all = -jnp.exp(alog_ref[...]) * (jnp.maximum(x, 0.0) + jnp.log1p(jnp.exp(-jnp.abs(x))))
    g_all = jnp.where(lane < GDN_HEADS, g_all, 0.0)
    sig_ab = jax.nn.sigmoid(ab)
    ti = lax.broadcasted_iota(jnp.int32, (T, T), 0)
    tj = lax.broadcasted_iota(jnp.int32, (T, T), 1)
    tril = (ti >= tj) & (ti // C == tj // C)
    stril = tril & (ti > tj)
    gc_all = jnp.dot(jnp.where(tril, 1.0, 0.0), g_all, preferred_element_type=F32,
                     precision=lax.Precision.HIGHEST)
    lane8 = lax.broadcasted_iota(jnp.int32, (8, ab.shape[1]), 1)

    heads = range(HP)
    dot = functools.partial(jnp.dot, preferred_element_type=F32)
    hid = [pl.program_id(1) * HP + hh for hh in heads]
    cs = [slice(hh * D, (hh + 1) * D) for hh in heads]
    q = [q_all[:, c] for c in cs]
    k = [k_all[:, c] for c in cs]
    q = [x * lax.rsqrt(jnp.sum(x * x, axis=-1, keepdims=True) + EPS) * (D ** -0.5) for x in q]
    k = [x * lax.rsqrt(jnp.sum(x * x, axis=-1, keepdims=True) + EPS) for x in k]
    gcol = [jnp.sum(jnp.where(lane == h, gc_all, 0.0), axis=-1, keepdims=True) for h in hid]
    beta = [jnp.sum(jnp.where(lane == GDN_HEADS + h, sig_ab, 0.0), axis=-1, keepdims=True) for h in hid]
    grow = [lax.dot_general(jnp.where(lane8 == h, 1.0, 0.0), gc_all, nt, preferred_element_type=F32,
                            precision=lax.Precision.HIGHEST)[0:1, :] for h in hid]

    kb = [k[i] * beta[i] for i in heads]
    eg = [jnp.exp(gcol[i]) for i in heads]
    decay = [jnp.where(tril, jnp.exp(jnp.where(tril, gcol[i] - grow[i], 0.0)), 0.0) for i in heads]
    qk = [lax.dot_general(jnp.concatenate([q[i], kb[i]], axis=0).astype(BF16), k[i].astype(BF16), nt,
                          preferred_element_type=F32) for i in heads]
    a_qk = [(qk[i][:T] * decay[i]).astype(BF16) for i in heads]
    lm = [jnp.where(stril, qk[i][T:] * decay[i], 0.0) for i in heads]

    inv = [jnp.where(ti == tj, 1.0, 0.0) - jnp.where(ti // 2 == tj // 2, lm[i], 0.0) for i in heads]
    s = 2
    while s < C:
        off = (ti // (2 * s) == tj // (2 * s)) & (ti // s != tj // s)
        xb = [inv[i].astype(BF16) for i in heads]
        xc = [dot(xb[i], jnp.where(off, lm[i], 0.0).astype(BF16)) for i in heads]
        inv = [inv[i] - dot(xc[i].astype(BF16), xb[i]) for i in heads]
        s *= 2

    uw = [dot(inv[i].astype(BF16), jnp.concatenate([v_all[:, cs[i]] * beta[i], kb[i] * eg[i]], axis=1).astype(BF16))
          for i in heads]
    qg = [q[i] * eg[i] for i in heads]

    state = [st_ref[i] for i in heads]
    for c in range(T // C):
        r = slice(c * C, (c + 1) * C)
        ws = [dot(jnp.concatenate([uw[i][r, D:], qg[i][r]], axis=0).astype(BF16), state[i].astype(BF16))
              for i in heads]
        vnew = [(uw[i][r, :D] - ws[i][:C]).astype(BF16) for i in heads]
        glast = [gcol[i][c * C + C - 1:(c + 1) * C, :] for i in heads]
        kdec = [(k[i][r] * jnp.exp(glast[i] - gcol[i][r])).T.astype(BF16) for i in heads]
        state = [state[i] * jnp.exp(glast[i]) + dot(kdec[i], vnew[i]) for i in heads]
        for i in heads:
            o = ws[i][C:] + dot(a_qk[i][r, c * C:(c + 1) * C], vnew[i])
            zc = z_ref[pl.ds(c * C, C), cs[i]].astype(F32)
            o_ref[pl.ds(c * C, C), cs[i]] = (_rms(o, on_ref[...]) * _silu(zc)).astype(o_ref.dtype)
    for i in heads:
        st_ref[i] = state[i]


def gdn_heads(y, ab, conv_qkv, a_log, dt_bias, o_norm, batch, seq_len, T=256, heads_per_step=8):
    H, D = GDN_HEADS, GDN_HD
    HP = heads_per_step
    W = HP * D
    nG = H // HP
    nT = seq_len // T
    lanes = ab.shape[1]
    pad = lambda p: jnp.pad(p.astype(F32), (0, lanes - p.shape[0])).reshape(1, lanes)
    row = lambda b, h, t: b * nT + t
    col = lambda off: (lambda b, h, t: (row(b, h, t), off * nG + h))
    cw = lambda off: (lambda b, h, t: (0, off * nG + h))
    const = lambda b, h, t: (0, 0)
    K = conv_qkv.shape[0]
    return pl.pallas_call(
        _gdn_kernel,
        grid=(batch, nG, nT),
        in_specs=[pl.BlockSpec((T, W), col(0)), pl.BlockSpec((T, W), col(1)),
                  pl.BlockSpec((T, W), col(2)), pl.BlockSpec((T, W), col(3)),
                  pl.BlockSpec((T, lanes), lambda b, h, t: (row(b, h, t), 0)),
                  pl.BlockSpec((K, W), cw(0)), pl.BlockSpec((K, W), cw(1)), pl.BlockSpec((K, W), cw(2)),
                  pl.BlockSpec((1, lanes), const), pl.BlockSpec((1, lanes), const),
                  pl.BlockSpec((1, D), const)],
        out_specs=pl.BlockSpec((T, W), col(0)),
        out_shape=jax.ShapeDtypeStruct((batch * seq_len, H * D), BF16),
        scratch_shapes=[pltpu.VMEM((T + CONV_HALO, W), F32), pltpu.VMEM((T + CONV_HALO, W), F32),
                        pltpu.VMEM((T + CONV_HALO, W), F32), pltpu.VMEM((HP, D, D), F32)],
        compiler_params=_cparams("parallel", "parallel", "arbitrary"),
        name="gdn_heads",
    )(y, y, y, y, ab, conv_qkv, conv_qkv, conv_qkv, pad(a_log), pad(dt_bias), o_norm.reshape(1, D))


def _shortconv_kernel(hb_ref, gb_ref, gc_ref, cw_ref, o_ref, ext_ref):
    first = pl.program_id(2) == 0
    x = gc_ref[...].astype(F32) * hb_ref[...].astype(F32)
    y = _causal_conv_ext(ext_ref, x, cw_ref[...], first)
    o_ref[...] = (gb_ref[...].astype(F32) * y).astype(o_ref.dtype)


def short_conv(y, col_off, conv_sc, batch, seq_len, T=512, tc=512):
    W = conv_sc.shape[1]
    nT = seq_len // T
    nC = W // tc
    spec = lambda off: pl.BlockSpec((T, tc), lambda b, c, t: (b * nT + t, (col_off + off) // tc + c))
    return pl.pallas_call(
        _shortconv_kernel,
        grid=(batch, nC, nT),
        in_specs=[spec(0), spec(W), spec(2 * W),
                  pl.BlockSpec((conv_sc.shape[0], tc), lambda b, c, t: (0, c))],
        out_specs=pl.BlockSpec((T, tc), lambda b, c, t: (b * nT + t, c)),
        out_shape=jax.ShapeDtypeStruct((batch * seq_len, W), BF16),
        scratch_shapes=[pltpu.VMEM((T + CONV_HALO, tc), F32)],
        compiler_params=_cparams("parallel", "parallel", "arbitrary"),
        name="short_conv",
    )(y, y, y, conv_sc)


def _nsa_prep_kernel(y_ref, cos_ref, sin_ref, q_ref, kc_ref, vc_ref, ks_ref, vs_ref, kw_ref, vw_ref):
    T = y_ref.shape[0]
    H, G, D = NSA_HEADS, NSA_KV, NSA_HD
    cos = cos_ref[...]
    sin = sin_ref[...]

    def head(c):
        return y_ref[:, c * D:(c + 1) * D]

    def rope(x):
        x = x.astype(F32)
        return x * cos + pltpu.roll(x, D // 2, 1) * sin

    for hh in range(H):
        q_ref[0, hh] = (rope(head(hh)) * (D ** -0.5 * LOG2E)).astype(BF16)
    t0 = pl.program_id(1) * T
    tok = t0 + lax.broadcasted_iota(jnp.int32, (T, D), 0)
    lane = lax.broadcasted_iota(jnp.int32, (T, D), 1)
    onehot = jnp.where(tok // SEL_BLOCK == lane, 1.0, 0.0).astype(BF16)
    for g in range(G):
        kc_ref[0, g] = rope(head(H + g)).astype(BF16)
        vc_ref[0, g] = head(H + G + g)
        ks_ref[0, g, :, 0:D] = rope(head(H + 2 * G + g)).astype(BF16)
        ks_ref[0, g, :, D:2 * D] = onehot
        vs_ref[0, g] = head(H + 3 * G + g).astype(F32).T.astype(BF16)
        kw_ref[0, g] = rope(head(H + 4 * G + g)).astype(BF16)
        vw_ref[0, g] = head(H + 5 * G + g).astype(F32).T.astype(BF16)


def nsa_prep(y, cos2, sin2, batch, seq_len, T=512):
    H, G, D = NSA_HEADS, NSA_KV, NSA_HD
    nT = seq_len // T
    assert seq_len // SEL_BLOCK <= D
    grp = lambda w: jax.ShapeDtypeStruct((batch, G, seq_len, w), BF16)
    gspec = lambda w: pl.BlockSpec((1, G, T, w), lambda b, t: (b, 0, t, 0))
    grp_t = jax.ShapeDtypeStruct((batch, G, D, seq_len), BF16)
    tspec = pl.BlockSpec((1, G, D, T), lambda b, t: (b, 0, 0, t))
    return pl.pallas_call(
        _nsa_prep_kernel,
        grid=(batch, nT),
        in_specs=[pl.BlockSpec((T, y.shape[1]), lambda b, t: (b * nT + t, 0)),
                  pl.BlockSpec((T, D), lambda b, t: (t, 0)),
                  pl.BlockSpec((T, D), lambda b, t: (t, 0))],
        out_specs=[pl.BlockSpec((1, H, T, D), lambda b, t: (b, 0, t, 0)),
                   gspec(D), gspec(D), gspec(2 * D), tspec, gspec(D), tspec],
        out_shape=[jax.ShapeDtypeStruct((batch, H, seq_len, D), BF16),
                   grp(D), grp(D), grp(2 * D), grp_t, grp(D), grp_t],
        compiler_params=_cparams("parallel", "parallel"),
        name="nsa_prep",
    )(y, cos2, sin2)


def _gelu_tanh(x):
    return 0.5 * x * (1.0 + jnp.tanh(math.sqrt(2.0 / math.pi) * (x + 0.044715 * (x * x * x))))


def _compress_kernel(x_ref, pos_ref, w1_ref, w2_ref, o_ref):
    x = x_ref[0].astype(F32)
    half = x.shape[1]
    pos = pos_ref[...]
    top = jnp.dot((x + pos[:, :half]).astype(BF16), w1_ref[0:half, :], preferred_element_type=F32)
    bot = jnp.dot((x + pos[:, half:]).astype(BF16), w1_ref[half:2 * half, :], preferred_element_type=F32)
    n = x.shape[0]
    hid = top + pltpu.roll(bot, n - 1, 0)
    o_ref[0] = jnp.dot(_gelu_tanh(hid).astype(BF16), w2_ref[...], preferred_element_type=F32).astype(BF16)


def nsa_compress_blocks(t, pos, w1, w2):
    B, G, S, D = t.shape
    assert CMP_LEN == 2 * CMP_STRIDE
    n = S // CMP_STRIDE
    x = t.reshape(B * G, n, CMP_STRIDE * D)
    hidden = w1.shape[1]
    return pl.pallas_call(
        _compress_kernel,
        grid=(B * G,),
        in_specs=[pl.BlockSpec((1, n, CMP_STRIDE * D), lambda i: (i, 0, 0)),
                  pl.BlockSpec((1, CMP_LEN * D), lambda i: (0, 0)),
                  pl.BlockSpec((CMP_LEN * D, hidden), lambda i: (0, 0)),
                  pl.BlockSpec((hidden, D), lambda i: (0, 0))],
        out_specs=pl.BlockSpec((1, n, D), lambda i: (i, 0, 0)),
        out_shape=jax.ShapeDtypeStruct((B * G, n, D), BF16),
        compiler_params=_cparams("parallel"),
        name="nsa_compress",
    )(x, pos.reshape(1, CMP_LEN * D).astype(F32), w1.astype(BF16), w2.astype(BF16))


def _gate_col(sig, idx):
    lane = lax.broadcasted_iota(jnp.int32, sig.shape, 1)
    return jnp.sum(jnp.where(lane == idx, sig, 0.0), axis=-1, keepdims=True)


def _cmp_select_kernel(q_ref, kc_ref, vc_ref, gl_ref, ov_ref, o_ref, sb_ref):
    g = pl.program_id(1)
    i = pl.program_id(2)
    R, tq, D = q_ref.shape[1], q_ref.shape[2], q_ref.shape[3]
    nc = kc_ref.shape[1]
    ns = ov_ref.shape[0]
    heads = range(R)
    nt = (((1,), (1,)), ((), ()))
    dot = functools.partial(jnp.dot, preferred_element_type=F32)
    kc = kc_ref[0]
    vct = vc_ref[0].astype(F32).T.astype(BF16)
    c_col = lax.broadcasted_iota(jnp.int32, (nc, tq), 0)
    t_row = i * tq + lax.broadcasted_iota(jnp.int32, (nc, tq), 1)
    valid = c_col * CMP_STRIDE + (CMP_LEN - 1) <= t_row
    s = [jnp.where(valid, lax.dot_general(kc, q_ref[0, r], nt, preferred_element_type=F32), NEG) for r in heads]
    e = [jnp.where(valid, jnp.exp2(x - jnp.max(x, axis=0, keepdims=True)), 0.0) for x in s]
    den = [jnp.sum(x, axis=0, keepdims=True) for x in e]
    p = [x / jnp.where(d > 0.0, d, 1.0) for x, d in zip(e, den)]
    o_t = [dot(vct, x.astype(BF16)) for x in p]
    sig = jax.nn.sigmoid(gl_ref[...])
    pick = jnp.where(lax.broadcasted_iota(jnp.int32, (8, sig.shape[1]), 1)
                     == g * R + lax.broadcasted_iota(jnp.int32, (8, sig.shape[1]), 0), 1.0, 0.0)
    gates = lax.dot_general(pick, sig, nt, preferred_element_type=F32,
                            precision=lax.Precision.HIGHEST)
    for r in heads:
        o_ref[:, r * D:(r + 1) * D] = (o_t[r] * gates[r:r + 1, :]).T.astype(o_ref.dtype)

    psum = functools.reduce(lambda a, b: a + b, p)
    ov = ov_ref[...]
    hi = psum.astype(BF16)
    r1 = psum - hi.astype(F32)
    mid = r1.astype(BF16)
    lo = (r1 - mid.astype(F32)).astype(BF16)
    score = dot(ov, hi) + (dot(ov, mid) + dot(ov, lo))
    n_col = lax.broadcasted_iota(jnp.int32, (ns, tq), 0)
    t_row = i * tq + lax.broadcasted_iota(jnp.int32, (ns, tq), 1)
    cur = t_row // SEL_BLOCK
    forced = (n_col == 0) | (n_col == cur) | (n_col == cur - 1)
    score = jnp.where(n_col * SEL_BLOCK <= t_row, score + jnp.where(forced, FORCE_BONUS, 0.0), NEG)
    SUB = 8
    rows = [score[a:a + SUB, :] for a in range(0, ns, SUB)]
    ranks = [jnp.zeros((SUB, tq), F32) for _ in rows]
    sub_row = lax.broadcasted_iota(jnp.int32, (SUB, tq), 0)
    for m in range(ns):
        sm = jnp.broadcast_to(score[m:m + 1, :], (SUB, tq))
        for a, sc in enumerate(rows):
            if a * SUB + SUB - 1 < m:
                ahead = sm > sc
            elif a * SUB > m:
                ahead = sm >= sc
            else:
                ahead = (sm > sc) | ((sm == sc) & (sub_row > m - a * SUB))
            ranks[a] = ranks[a] + jnp.where(ahead, 1.0, 0.0)
    rank = jnp.concatenate(ranks, axis=0)
    bias = jnp.where(rank < float(min(SEL_N, ns)), 0.0, NEG)
    bias = jnp.concatenate([bias, jnp.zeros((sb_ref.shape[3] - ns, tq), F32)], axis=0)
    sb_ref[0, 0] = bias.T.astype(sb_ref.dtype)


def _overlap_matrix(nc, ns):
    cs = np.arange(nc) * CMP_STRIDE
    ss = np.arange(ns) * SEL_BLOCK
    ov = np.minimum(cs[None, :] + CMP_LEN, ss[:, None] + SEL_BLOCK) - np.maximum(cs[None, :], ss[:, None])
    ov = np.clip(ov, 0, None) / CMP_STRIDE
    ov[:, nc - CMP_LEN // CMP_STRIDE + 1:] = 0.0
    return jnp.asarray(ov, BF16)


def nsa_cmp_select(q, kcmp, vcmp, gl, batch, seq_len, tq=256):
    H, G, D = NSA_HEADS, NSA_KV, NSA_HD
    R = H // G
    nq = seq_len // tq
    nc = kcmp.shape[1]
    ns = seq_len // SEL_BLOCK
    ov = _overlap_matrix(nc, ns)
    return pl.pallas_call(
        _cmp_select_kernel,
        grid=(batch, G, nq),
        in_specs=[pl.BlockSpec((1, R, tq, D), lambda b, g, i: (b, g, i, 0)),
                  pl.BlockSpec((1, nc, D), lambda b, g, i: (b * G + g, 0, 0)),
                  pl.BlockSpec((1, nc, D), lambda b, g, i: (b * G + g, 0, 0)),
                  pl.BlockSpec((tq, gl.shape[1]), lambda b, g, i: (b * nq + i, 0)),
                  pl.BlockSpec((ns, nc), lambda b, g, i: (0, 0))],
        out_specs=[pl.BlockSpec((tq, R * D), lambda b, g, i: (b * nq + i, g)),
                   pl.BlockSpec((1, 1, tq, D), lambda b, g, i: (b, g, i, 0))],
        out_shape=[jax.ShapeDtypeStruct((batch * seq_len, H * D), BF16),
                   jax.ShapeDtypeStruct((batch, G, seq_len, D), BF16)],
        compiler_params=_cparams("parallel", "parallel", "parallel"),
        name="nsa_cmp_select",
    )(q, kcmp, vcmp, gl, ov)


def _sel_win_kernel(q_ref, sb_ref, ks_ref, vs_ref, kw_ref, vw_ref, gl_ref, o_ref, m_ref, l_ref, acc_ref,
                    s_ref, *, tk):
    g = pl.program_id(1)
    i = pl.program_id(2)
    R, tq, D = q_ref.shape[1], q_ref.shape[2], q_ref.shape[3]
    cols = R * tq
    H = NSA_HEADS
    nt = (((1,), (1,)), ((), ()))
    q = jnp.concatenate([q_ref[0, r] for r in range(R)], axis=0)
    sb = sb_ref[0, 0]
    q_aug = jnp.concatenate([q, jnp.concatenate([sb] * R, axis=0)], axis=1)
    t_cols = i * tq + (lax.broadcasted_iota(jnp.int32, (1, cols), 1) & (tq - 1))

    m_ref[...] = jnp.full_like(m_ref, NEG)
    l_ref[...] = jnp.zeros_like(l_ref)
    acc_ref[...] = jnp.zeros_like(acc_ref)

    gw = cols // COL_GROUPS
    groups = [slice(c * gw, (c + 1) * gw) for c in range(COL_GROUPS)]
    q_grp = [q_aug[c] for c in groups]
    dot = functools.partial(jnp.dot, preferred_element_type=F32)

    def scores(j):
        k = ks_ref[0, 0, pl.ds(pl.multiple_of(j * tk, tk), tk), :]
        return [lax.dot_general(k, qa, nt, preferred_element_type=F32) for qa in q_grp]

    def accumulate(j, s):
        vt = vs_ref[0, 0, :, pl.ds(pl.multiple_of(j * tk, tk), tk)]
        m_old = [m_ref[0:1, c] for c in groups]
        m_new = [jnp.maximum(mo, jnp.max(x, axis=0, keepdims=True)) for mo, x in zip(m_old, s)]
        alpha = [jnp.exp2(mo - mn) for mo, mn in zip(m_old, m_new)]
        p = [jnp.exp2(x - mn) for x, mn in zip(s, m_new)]
        pv = [dot(vt, x.astype(BF16)) for x in p]
        for n, c in enumerate(groups):
            l_ref[0:1, c] = l_ref[0:1, c] * alpha[n] + jnp.sum(p[n], axis=0, keepdims=True)
            acc_ref[:, c] = acc_ref[:, c] * alpha[n] + pv[n]
            m_ref[0:1, c] = m_new[n]

    last = (i * tq) // tk
    for n, c in enumerate(groups):
        s_ref[:, c] = scores(0)[n]

    def body(j, carry):
        s_next = scores(j + 1)
        accumulate(j, [s_ref[:, c] for c in groups])
        for n, c in enumerate(groups):
            s_ref[:, c] = s_next[n]
        return carry

    lax.fori_loop(0, last, body, 0)
    kpos = last * tk + lax.broadcasted_iota(jnp.int32, (tk, gw), 0)
    accumulate(last, [jnp.where(kpos <= t_cols[:, c], s_ref[:, c], NEG) for c in groups])

    sig = jax.nn.sigmoid(gl_ref[...])
    row = lax.broadcasted_iota(jnp.int32, (2 * R, sig.shape[1]), 0)
    lane = lax.broadcasted_iota(jnp.int32, (2 * R, sig.shape[1]), 1)
    pick = jnp.where(lane == H + (row // R) * H + g * R + (row % R), 1.0, 0.0)
    gates = lax.dot_general(pick, sig, nt, preferred_element_type=F32,
                            precision=lax.Precision.HIGHEST)
    g_slc = jnp.concatenate([gates[r:r + 1, :] for r in range(R)], axis=1)
    g_win = jnp.concatenate([gates[R + r:R + r + 1, :] for r in range(R)], axis=1)
    o_t = acc_ref[...] * (g_slc / l_ref[0:1, :])

    wlen = WIN + tq
    start = pl.multiple_of(jnp.maximum(i * tq - WIN, 0), tq)
    kw = kw_ref[0, 0, pl.ds(start, wlen), :]
    vwt = vw_ref[0, 0, :, pl.ds(start, wlen)]
    kpos = start + lax.broadcasted_iota(jnp.int32, (wlen, gw), 0)
    s = [lax.dot_general(kw, q[c], nt, preferred_element_type=F32) for c in groups]
    s = [jnp.where((kpos <= t_cols[:, c]) & (kpos > t_cols[:, c] - WIN), x, NEG) for x, c in zip(s, groups)]
    p = [jnp.exp2(x - jnp.max(x, axis=0, keepdims=True)) for x in s]
    o_w = [dot(vwt, x.astype(BF16)) * (g_win[:, c] / jnp.sum(x, axis=0, keepdims=True))
           for x, c in zip(p, groups)]
    o_t = o_t + jnp.concatenate(o_w, axis=1)

    for r in range(R):
        o_ref[:, r * D:(r + 1) * D] = o_t[:, r * tq:(r + 1) * tq].T.astype(o_ref.dtype)


def nsa_selected_window(q, sb, ks_aug, vs, kw, vw, gl, batch, seq_len, tq=256, tk=512):
    H, G, D = NSA_HEADS, NSA_KV, NSA_HD
    R = H // G
    nq = seq_len // tq
    assert tk % tq == 0 and WIN % tq == 0 and tq & (tq - 1) == 0 and seq_len >= WIN + tq and tq % 128 == 0
    full = lambda w: pl.BlockSpec((1, 1, seq_len, w), lambda b, g, i: (b, g, 0, 0))
    full_t = pl.BlockSpec((1, 1, D, seq_len), lambda b, g, i: (b, g, 0, 0))
    return pl.pallas_call(
        functools.partial(_sel_win_kernel, tk=tk),
        grid=(batch, G, nq),
        in_specs=[pl.BlockSpec((1, R, tq, D), lambda b, g, i: (b, g, i, 0)),
                  pl.BlockSpec((1, 1, tq, D), lambda b, g, i: (b, g, i, 0)),
                  full(2 * D), full_t, full(D), full_t,
                  pl.BlockSpec((tq, gl.shape[1]), lambda b, g, i: (b * nq + i, 0))],
        out_specs=pl.BlockSpec((tq, R * D), lambda b, g, i: (b * nq + i, g)),
        out_shape=jax.ShapeDtypeStruct((batch * seq_len, H * D), BF16),
        scratch_shapes=[pltpu.VMEM((8, R * tq), F32), pltpu.VMEM((8, R * tq), F32),
                        pltpu.VMEM((D, R * tq), F32), pltpu.VMEM((tk, R * tq), F32)],
        compiler_params=_cparams("parallel", "parallel", "arbitrary"),
        name="nsa_selected_window",
    )(q, sb, ks_aug, vs, kw, vw, gl)


def nsa_mixer_parts(y, gl, p, i, cos, sin, batch, seq_len):
    D = NSA_HD
    cos2 = jnp.concatenate([cos, cos], axis=1)
    sin2 = jnp.concatenate([-sin, sin], axis=1)
    q, kc, vc, ks_aug, vs, kw, vw = nsa_prep(y, cos2, sin2, batch, seq_len)
    kcmp = nsa_compress_blocks(kc, p["cmp_pos_k"][i], p["cmp_w1_k"][i], p["cmp_w2_k"][i])
    vcmp = nsa_compress_blocks(vc, p["cmp_pos_v"][i], p["cmp_w1_v"][i], p["cmp_w2_v"][i])
    o_cmp, sb = nsa_cmp_select(q, kcmp, vcmp, gl, batch, seq_len)
    o_sw = nsa_selected_window(q, sb, ks_aug, vs, kw, vw, gl, batch, seq_len)
    return o_cmp, o_sw


def split_cols(t, widths):
    offs = np.cumsum(widths)[:-1].tolist()
    return jnp.split(t, offs, axis=-1)


def rmsnorm(x, w):
    xf = x.astype(jnp.float32)
    y = xf * lax.rsqrt(jnp.mean(xf * xf, axis=-1, keepdims=True) + EPS)
    return (y * w.astype(jnp.float32)).astype(x.dtype)


def l2norm(x):
    return x * lax.rsqrt(jnp.sum(x * x, axis=-1, keepdims=True) + EPS)


def causal_dwconv(x, w):
    K = w.shape[0]
    S = x.shape[1]
    xp = jnp.pad(x, ((0, 0), (K - 1, 0), (0, 0)))
    return sum(xp[:, k:k + S] * w[k] for k in range(K))


def rope_tables(S, dim):
    inv = 1.0 / (ROPE_THETA ** (jnp.arange(0, dim, 2, dtype=jnp.float32) / dim))
    ang = jnp.arange(S, dtype=jnp.float32)[:, None] * inv[None, :]
    return jnp.cos(ang), jnp.sin(ang)


def apply_rope(x, cos, sin):
    x1, x2 = jnp.split(x, 2, axis=-1)
    c = cos[None, :, None, :]
    s = sin[None, :, None, :]
    return jnp.concatenate([x1 * c - x2 * s, x2 * c + x1 * s], axis=-1)


def gated_delta_chunked(q, k, v, g, beta):
    B, H, S, D = q.shape
    C = GDN_CHUNK
    N = S // C
    q = q * (D ** -0.5)
    rs = lambda t: t.reshape((B, H, N, C) + t.shape[3:])
    q, k, v, beta = rs(q), rs(k), rs(v), rs(beta)
    g = jnp.cumsum(rs(g), axis=-1)
    kb = k * beta[..., None]
    vb = v * beta[..., None]
    tril = jnp.tril(jnp.ones((C, C), dtype=bool))
    stril = jnp.tril(jnp.ones((C, C), dtype=bool), -1)
    gdiff = g[..., :, None] - g[..., None, :]
    decay = jnp.where(tril, jnp.exp(jnp.where(tril, gdiff, 0.0)), 0.0)
    Lm = jnp.where(stril, jnp.einsum('bhnid,bhnjd->bhnij', kb, k) * decay, 0.0)
    eye = jnp.eye(C, dtype=jnp.float32)
    T = lax.linalg.triangular_solve(eye + Lm, jnp.broadcast_to(eye, Lm.shape),
                                    left_side=True, lower=True)
    u = T @ vb
    w = T @ (kb * jnp.exp(g)[..., None])
    qk = jnp.where(tril, jnp.einsum('bhnid,bhnjd->bhnij', q, k) * decay, 0.0)

    def step(state, xs):
        qc, kc, uc, wc, gc, ac = xs
        vnew = uc - wc @ state
        o = (qc * jnp.exp(gc)[..., None]) @ state + ac @ vnew
        glast = gc[..., -1]
        kdec = kc * jnp.exp(glast[..., None] - gc)[..., None]
        state = state * jnp.exp(glast)[..., None, None] + jnp.einsum('bhcd,bhce->bhde', kdec, vnew)
        return state, o

    xs = tuple(jnp.moveaxis(t, 2, 0) for t in (q, k, u, w, g, qk))
    s0 = jnp.zeros((B, H, D, D), jnp.float32)
    _, o = lax.scan(step, s0, xs)
    return jnp.moveaxis(o, 0, 2).reshape(B, H, S, D)


def delta_shortconv_core(y, conv_qkv, a_log, dt_bias, o_norm, conv_sc):
    B, S, _ = y.shape
    H, D = GDN_HEADS, GDN_HD
    SCW = H * D
    f32 = jnp.float32
    q, k, v, z, a, b, hb, gb, gc = split_cols(y, [H * D] * 4 + [H] * 2 + [SCW] * 3)
    qkv = jax.nn.silu(causal_dwconv(jnp.concatenate([q, k, v], axis=-1), conv_qkv)).astype(f32)
    q, k, v = [t.reshape(B, S, H, D).transpose(0, 2, 1, 3) for t in jnp.split(qkv, 3, axis=-1)]
    g = -jnp.exp(a_log.astype(f32)) * jax.nn.softplus(a.astype(f32) + dt_bias.astype(f32))
    beta = jax.nn.sigmoid(b.astype(f32))
    o = gated_delta_chunked(l2norm(q), l2norm(k), v, g.transpose(0, 2, 1), beta.transpose(0, 2, 1))
    o = rmsnorm(o.transpose(0, 2, 1, 3), o_norm) * jax.nn.silu(z.astype(f32).reshape(B, S, H, D))
    y_a = o.reshape(B, S, H * D)
    y_b = gb * causal_dwconv(gc * hb, conv_sc)
    return jnp.concatenate([y_a, y_b], axis=-1)


def nsa_compress(t, pos, w1, w2):
    B, G, S, D = t.shape
    r = CMP_LEN // CMP_STRIDE
    nc = S // CMP_STRIDE - r + 1
    tr = t.reshape(B, G, S // CMP_STRIDE, CMP_STRIDE, D)
    blocks = jnp.concatenate([tr[:, :, j:j + nc] for j in range(r)], axis=3)
    blocks = (blocks + pos).reshape(B, G, nc, CMP_LEN * D)
    return jax.nn.gelu(blocks @ w1) @ w2


def nsa_compressed_branch(q, kc, vc):
    S = q.shape[3]
    nc = kc.shape[2]
    t = jnp.arange(S)
    valid = (jnp.arange(nc) * CMP_STRIDE + CMP_LEN - 1)[None, :] <= t[:, None]
    s = jnp.einsum('bgrsd,bgcd->bgrsc', q, kc)
    p = jnp.where(valid, jax.nn.softmax(jnp.where(valid, s, NEG), axis=-1), 0.0)
    return jnp.einsum('bgrsc,bgcd->bgrsd', p, vc), p


def nsa_selected_branch(q, k, v, p_cmp):
    B, G, R, S, D = q.shape
    nc = p_cmp.shape[-1]
    ns = S // SEL_BLOCK
    n_sel = min(SEL_N, ns)
    cs = jnp.arange(nc) * CMP_STRIDE
    ss = jnp.arange(ns) * SEL_BLOCK
    ov = jnp.minimum(cs[:, None] + CMP_LEN, ss[None, :] + SEL_BLOCK) - jnp.maximum(cs[:, None], ss[None, :])
    M = (jnp.clip(ov, 0, None) / CMP_STRIDE).astype(jnp.float32)
    score = jnp.einsum('bgrsc,cn->bgsn', p_cmp, M)
    t = jnp.arange(S)
    blk = jnp.arange(ns)
    cur = (t // SEL_BLOCK)[:, None]
    valid = blk[None, :] * SEL_BLOCK <= t[:, None]
    forced = (blk[None, :] == 0) | (blk[None, :] == cur) | (blk[None, :] == cur - 1)
    score = jnp.where(valid, score + jnp.where(forced, FORCE_BONUS, 0.0), NEG)
    _, idx = lax.top_k(score, n_sel)
    sel = jnp.sum(jax.nn.one_hot(idx, ns, dtype=jnp.float32), axis=-2) > 0
    s = jnp.einsum('bgrsd,bgkd->bgrsk', q, k)
    mask = jnp.repeat(sel, SEL_BLOCK, axis=-1) & (t[None, :] <= t[:, None])
    p = jax.nn.softmax(jnp.where(mask[:, :, None], s, NEG), axis=-1)
    return jnp.einsum('bgrsk,bgkd->bgrsd', p, v)


def nsa_window_branch(q, k, v):
    S = q.shape[3]
    t = jnp.arange(S)
    mask = (t[None, :] <= t[:, None]) & (t[None, :] > t[:, None] - WIN)
    s = jnp.einsum('bgrsd,bgkd->bgrsk', q, k)
    p = jax.nn.softmax(jnp.where(mask, s, NEG), axis=-1)
    return jnp.einsum('bgrsk,bgkd->bgrsd', p, v)


def nsa_core(y, pos_k, w1_k, w2_k, pos_v, w1_v, w2_v, cos, sin):
    B, S, _ = y.shape
    H, G, D = NSA_HEADS, NSA_KV, NSA_HD
    R = H // G
    f32 = jnp.float32
    q, kc, vc, ks, vs, kw, vw, gt = split_cols(y, [H * D] + [G * D] * 6 + [3 * H])
    q = apply_rope(q.reshape(B, S, H, D), cos, sin) * (D ** -0.5)
    q = q.reshape(B, S, G, R, D).transpose(0, 2, 3, 1, 4)
    rk = lambda t: apply_rope(t.reshape(B, S, G, D), cos, sin).transpose(0, 2, 1, 3)
    rv = lambda t: t.reshape(B, S, G, D).transpose(0, 2, 1, 3)
    kc, ks, kw = rk(kc), rk(ks), rk(kw)
    vc, vs, vw = rv(vc), rv(vs), rv(vw)
    gates = jax.nn.sigmoid(gt).reshape(B, S, 3, G, R).transpose(2, 0, 3, 4, 1)[..., None]
    kcmp = nsa_compress(kc, pos_k.astype(f32), w1_k.astype(f32), w2_k.astype(f32))
    vcmp = nsa_compress(vc, pos_v.astype(f32), w1_v.astype(f32), w2_v.astype(f32))
    o_cmp, p_cmp = nsa_compressed_branch(q, kcmp, vcmp)
    o_slc = nsa_selected_branch(q, ks, vs, p_cmp)
    o_win = nsa_window_branch(q, kw, vw)
    o = gates[0] * o_cmp + gates[1] * o_slc + gates[2] * o_win
    return o.transpose(0, 3, 1, 2, 4).reshape(B, S, H * D)


def kernel(x, norm_w, ab_w_in, ab_w_out, gdn_conv, gdn_a_log, gdn_dt_bias, gdn_norm, sc_conv, nsa_w_in, nsa_w_out, cmp_pos_k, cmp_w1_k, cmp_w2_k, cmp_pos_v, cmp_w1_v, cmp_w2_v, ffn_w_gu, ffn_conv, ffn_w_down):
    B, S, D = x.shape
    p = dict(norm_w=norm_w, ab_w_in=ab_w_in, ab_w_out=ab_w_out, gdn_conv=gdn_conv, gdn_a_log=gdn_a_log,
             gdn_dt_bias=gdn_dt_bias, gdn_norm=gdn_norm, sc_conv=sc_conv, nsa_w_in=nsa_w_in,
             nsa_w_out=nsa_w_out, cmp_pos_k=cmp_pos_k, cmp_w1_k=cmp_w1_k, cmp_w2_k=cmp_w2_k,
             cmp_pos_v=cmp_pos_v, cmp_w1_v=cmp_w1_v, cmp_w2_v=cmp_w2_v, ffn_w_gu=ffn_w_gu,
             ffn_conv=ffn_conv, ffn_w_down=ffn_w_down)
    cos, sin = rope_tables(S, NSA_HD)
    h = x.reshape(B * S, D)
    for l in range(norm_w.shape[0]):
        _, h = run_layer(h, l, B, S, cos, sin, p)
    return h.reshape(B, S, D)


def run_layer(h, l, B, S, cos, sin, p):
    D = h.shape[1]
    i = l // 2
    norm_w = p["norm_w"]
    if l % 2 == 0:
        gw = GDN_HEADS * GDN_HD
        w = p["ab_w_in"][i]
        w_main = jnp.concatenate([w[:, :4 * gw], w[:, 4 * gw + 2 * GDN_HEADS:]], axis=1).astype(BF16)
        w_ab = jnp.pad(w[:, 4 * gw:4 * gw + 2 * GDN_HEADS], ((0, 0), (0, 128 - 2 * GDN_HEADS))).astype(BF16)
        y, ab = norm_matmul(h, norm_w[l, 0], w_main, w_ab)
        y_a = gdn_heads(y, ab, p["gdn_conv"][i], p["gdn_a_log"][i], p["gdn_dt_bias"][i], p["gdn_norm"][i], B, S)
        y_b = short_conv(y, 4 * gw, p["sc_conv"][i], B, S)
        h1 = out_proj([y_a, y_b], p["ab_w_out"][i].astype(BF16), h, norm_w[l, 1], concat=True)
    else:
        w = p["nsa_w_in"][i]
        nm = NSA_HEADS * NSA_HD + 6 * NSA_KV * NSA_HD
        w_gl = jnp.pad(w[:, nm:], ((0, 0), (0, 128 - (w.shape[1] - nm)))).astype(BF16)
        y, gl = norm_matmul(h, norm_w[l, 0], w[:, :nm].astype(BF16), w_gl)
        o_cmp, o_sw = nsa_mixer_parts(y, gl, p, i, cos, sin, B, S)
        h1 = out_proj([o_cmp, o_sw], p["nsa_w_out"][i].astype(BF16), h, norm_w[l, 1], concat=False)
    h2 = conv_ffn_block(h1, S, norm_w[l, 2], p["ffn_w_gu"][l].astype(BF16), p["ffn_conv"][l],
                        p["ffn_w_down"][l].astype(BF16), norm_w[l, 3])
    return h1, h2
```

```python
import functools
import math

import jax
import jax.numpy as jnp
import numpy as np
from jax import lax
from jax.experimental import pallas as pl
from jax.experimental.pallas import tpu as pltpu

F32 = jnp.float32
BF16 = jnp.bfloat16

EPS = 1e-6
NEG = -1e30

GDN_HEADS = 8
GDN_HD = 128
GDN_CHUNK = 64
NSA_HEADS = 16
NSA_KV = 4
NSA_HD = 128
CMP_LEN = 32
CMP_STRIDE = 16
SEL_BLOCK = 64
SEL_N = 16
WIN = 512
FORCE_BONUS = 1000.0
ROPE_THETA = 10000.0
LOG2E = math.log2(math.e)
COL_GROUPS = 4
V_ONES_ROWS = 16

VMEM_LIMIT_BYTES = 56 * 1024 * 1024
BF16_ROWS = 16


def _cparams(*sem, vmem=VMEM_LIMIT_BYTES):
    return pltpu.CompilerParams(dimension_semantics=sem, vmem_limit_bytes=vmem)


def _rms(x, w):
    return x * lax.rsqrt(jnp.mean(x * x, axis=-1, keepdims=True) + EPS) * w


def _norm_mm_kernel(h_ref, nw_ref, w_ref, ws_ref, o_ref, os_ref, xn_ref):
    @pl.when(pl.program_id(1) == 0)
    def _():
        xn_ref[...] = _rms(h_ref[...], nw_ref[...]).astype(BF16)
        os_ref[...] = jnp.dot(xn_ref[...], ws_ref[...], preferred_element_type=F32)

    o_ref[...] = jnp.dot(xn_ref[...], w_ref[...], preferred_element_type=F32).astype(o_ref.dtype)


def norm_matmul(h, nw, w, w_small, tm=1024, tn=1024):
    M, D = h.shape
    N = w.shape[1]
    Ns = w_small.shape[1]
    tn = min(tn, N)
    return pl.pallas_call(
        _norm_mm_kernel,
        grid=(M // tm, N // tn),
        in_specs=[pl.BlockSpec((tm, D), lambda i, j: (i, 0)),
                  pl.BlockSpec((1, D), lambda i, j: (0, 0)),
                  pl.BlockSpec((D, tn), lambda i, j: (0, j)),
                  pl.BlockSpec((D, Ns), lambda i, j: (0, 0))],
        out_specs=[pl.BlockSpec((tm, tn), lambda i, j: (i, j)),
                   pl.BlockSpec((tm, Ns), lambda i, j: (i, 0))],
        out_shape=[jax.ShapeDtypeStruct((M, N), BF16), jax.ShapeDtypeStruct((M, Ns), F32)],
        scratch_shapes=[pltpu.VMEM((tm, D), BF16)],
        compiler_params=_cparams("parallel", "arbitrary"),
        name="norm_matmul",
    )(h, nw.reshape(1, D), w, w_small)


def _out_proj_kernel(*refs, n_parts, concat):
    parts = refs[:n_parts]
    w_ref, h_ref, nw_ref, o_ref = refs[n_parts:]
    if concat:
        x = jnp.concatenate([p[...] for p in parts], axis=1)
    else:
        x = parts[0][...].astype(F32)
        for p in parts[1:]:
            x = x + p[...].astype(F32)
        x = x.astype(BF16)
    m = jnp.dot(x, w_ref[...], preferred_element_type=F32)
    o_ref[...] = h_ref[...] + _rms(m, nw_ref[...])


def out_proj(parts, w, h, nw, concat, tm=512):
    M, D = h.shape
    K = w.shape[0]
    in_specs = [pl.BlockSpec((tm, p.shape[1]), lambda i: (i, 0)) for p in parts]
    in_specs += [pl.BlockSpec((K, D), lambda i: (0, 0)),
                 pl.BlockSpec((tm, D), lambda i: (i, 0)),
                 pl.BlockSpec((1, D), lambda i: (0, 0))]
    return pl.pallas_call(
        functools.partial(_out_proj_kernel, n_parts=len(parts), concat=concat),
        grid=(M // tm,),
        in_specs=in_specs,
        out_specs=pl.BlockSpec((tm, D), lambda i: (i, 0)),
        out_shape=jax.ShapeDtypeStruct((M, D), F32),
        compiler_params=_cparams("parallel"),
        name="out_proj",
    )(*parts, w, h, nw.reshape(1, D))


def _ffn_kernel(h_ref, halo_ref, nw2_ref, wg_ref, wu_ref, cw_ref, wd_ref, nw3_ref, o_ref,
                xn_ref, g_ref, acc_ref, *, tiles_per_seq):
    i = pl.program_id(0)
    j = pl.program_id(1)
    tm = h_ref.shape[0]
    H = BF16_ROWS

    @pl.when(j == 0)
    def _():
        xn_ref[pl.ds(H, tm), :] = _rms(h_ref[...], nw2_ref[...]).astype(BF16)
        halo = _rms(halo_ref[...], nw2_ref[...])
        halo = jnp.where(i % tiles_per_seq == 0, 0.0, halo)
        xn_ref[pl.ds(0, H), :] = halo.astype(BF16)
        acc_ref[...] = jnp.zeros_like(acc_ref)

    g_ref[...] = jnp.dot(xn_ref[...], wg_ref[...], preferred_element_type=F32)
    u = jnp.dot(xn_ref[pl.ds(H, tm), :], wu_ref[...], preferred_element_type=F32)
    cw = cw_ref[...]
    c = (g_ref[pl.ds(H, tm), :] * cw[2:3, :] + g_ref[pl.ds(H - 1, tm), :] * cw[1:2, :]
         + g_ref[pl.ds(H - 2, tm), :] * cw[0:1, :])
    hid = (c * jax.nn.sigmoid(c) * u).astype(BF16)
    acc_ref[...] += jnp.dot(hid, wd_ref[...], preferred_element_type=F32)

    @pl.when(j == pl.num_programs(1) - 1)
    def _():
        o_ref[...] = h_ref[...] + _rms(acc_ref[...], nw3_ref[...])


def conv_ffn_block(h, seq_len, nw2, w_gu, conv_w, w_down, nw3, layer, tm=512, tf=512):
    M, D = h.shape
    FF = w_down.shape[1]
    nf = FF // tf
    H = BF16_ROWS
    kern = functools.partial(_ffn_kernel, tiles_per_seq=seq_len // tm)
    return pl.pallas_call(
        kern,
        grid=(M // tm, nf),
        in_specs=[pl.BlockSpec((tm, D), lambda i, j: (i, 0)),
                  pl.BlockSpec((H, D), lambda i, j: (jnp.maximum(i * (tm // H) - 1, 0), 0)),
                  pl.BlockSpec((1, D), lambda i, j: (0, 0)),
                  pl.BlockSpec((None, D, tf), lambda i, j: (layer, 0, j)),
                  pl.BlockSpec((None, D, tf), lambda i, j: (layer, 0, nf + j)),
                  pl.BlockSpec((None, conv_w.shape[1], tf), lambda i, j: (layer, 0, j)),
                  pl.BlockSpec((None, tf, D), lambda i, j: (layer, j, 0)),
                  pl.BlockSpec((1, D), lambda i, j: (0, 0))],
        out_specs=pl.BlockSpec((tm, D), lambda i, j: (i, 0)),
        out_shape=jax.ShapeDtypeStruct((M, D), F32),
        scratch_shapes=[pltpu.VMEM((tm + H, D), BF16),
                        pltpu.VMEM((tm + H, tf), F32),
                        pltpu.VMEM((tm, D), F32)],
        compiler_params=_cparams("parallel", "arbitrary"),
        name="conv_ffn",
    )(h, h, nw2.reshape(1, D), w_gu, w_gu, conv_w, w_down, nw3.reshape(1, D))


CONV_HALO = 8


def _silu(x):
    return x * jax.nn.sigmoid(x)


def _causal_conv_ext(ext_ref, x, cw, first):
    T = x.shape[0]
    K = cw.shape[0]

    @pl.when(first)
    def _():
        ext_ref[pl.ds(0, CONV_HALO), :] = jnp.zeros((CONV_HALO, x.shape[1]), F32)

    @pl.when(jnp.logical_not(first))
    def _():
        ext_ref[pl.ds(0, CONV_HALO), :] = ext_ref[pl.ds(T, CONV_HALO), :]

    ext_ref[pl.ds(CONV_HALO, T), :] = x
    y = ext_ref[pl.ds(CONV_HALO, T), :] * cw[K - 1:K, :]
    for k in range(K - 1):
        y = y + ext_ref[pl.ds(CONV_HALO - (K - 1) + k, T), :] * cw[k:k + 1, :]
    return y


def _gdn_kernel(q_ref, k_ref, v_ref, z_ref, ab_ref, cq_ref, ck_ref, cv_ref, alog_ref, dtb_ref, on_ref,
                o_ref, qe_ref, ke_ref, ve_ref, st_ref):
    t = pl.program_id(2)
    T = q_ref.shape[0]
    D = GDN_HD
    HP = q_ref.shape[1] // D
    C = GDN_CHUNK
    first = t == 0
    nt = (((1,), (1,)), ((), ()))

    @pl.when(first)
    def _():
        st_ref[...] = jnp.zeros_like(st_ref)

    q_all = _silu(_causal_conv_ext(qe_ref, q_ref[...].astype(F32), cq_ref[...], first))
    k_all = _silu(_causal_conv_ext(ke_ref, k_ref[...].astype(F32), ck_ref[...], first))
    v_all = _silu(_causal_conv_ext(ve_ref, v_ref[...].astype(F32), cv_ref[...], first))

    ab = ab_ref[...]
    lane = lax.broadcasted_iota(jnp.int32, ab.shape, 1)
    x = ab + dtb_ref[...]
    g_all = -jnp.exp(alog_ref[...]) * (jnp.maximum(x, 0.0) + jnp.log1p(jnp.exp(-jnp.abs(x))))
    g_all = jnp.where(lane < GDN_HEADS, g_all, 0.0)
    sig_ab = jax.nn.sigmoid(ab)
    ti = lax.broadcasted_iota(jnp.int32, (T, T), 0)
    tj = lax.broadcasted_iota(jnp.int32, (T, T), 1)
    tril = (ti >= tj) & (ti // C == tj // C)
    stril = tril & (ti > tj)
    gc_all = jnp.dot(jnp.where(tril, 1.0, 0.0), g_all, preferred_element_type=F32,
                     precision=lax.Precision.HIGHEST)
    lane8 = lax.broadcasted_iota(jnp.int32, (8, ab.shape[1]), 1)

    heads = range(HP)
    dot = functools.partial(jnp.dot, preferred_element_type=F32)
    hid = [pl.program_id(1) * HP + hh for hh in heads]
    cs = [slice(hh * D, (hh + 1) * D) for hh in heads]
    q = [q_all[:, c] for c in cs]
    k = [k_all[:, c] for c in cs]
    q = [x * lax.rsqrt(jnp.sum(x * x, axis=-1, keepdims=True) + EPS) * (D ** -0.5) for x in q]
    k = [x * lax.rsqrt(jnp.sum(x * x, axis=-1, keepdims=True) + EPS) for x in k]
    gcol = [jnp.sum(jnp.where(lane == h, gc_all, 0.0), axis=-1, keepdims=True) for h in hid]
    beta = [jnp.sum(jnp.where(lane == GDN_HEADS + h, sig_ab, 0.0), axis=-1, keepdims=True) for h in hid]
    grow = [lax.dot_general(jnp.where(lane8 == h, 1.0, 0.0), gc_all, nt, preferred_element_type=F32,
                            precision=lax.Precision.HIGHEST)[0:1, :] for h in hid]

    kb = [k[i] * beta[i] for i in heads]
    eg = [jnp.exp(gcol[i]) for i in heads]
    decay = [jnp.where(tril, jnp.exp(jnp.where(tril, gcol[i] - grow[i], 0.0)), 0.0) for i in heads]
    qk = [lax.dot_general(jnp.concatenate([q[i], kb[i]], axis=0).astype(BF16), k[i].astype(BF16), nt,
                          preferred_element_type=F32) for i in heads]
    a_qk = [(qk[i][:T] * decay[i]).astype(BF16) for i in heads]
    lm = [jnp.where(stril, qk[i][T:] * decay[i], 0.0) for i in heads]

    inv = [jnp.where(ti == tj, 1.0, 0.0) - jnp.where(ti // 2 == tj // 2, lm[i], 0.0) for i in heads]
    s = 2
    while s < C:
        off = (ti // (2 * s) == tj // (2 * s)) & (ti // s != tj // s)
        xb = [inv[i].astype(BF16) for i in heads]
        xc = [dot(xb[i], jnp.where(off, lm[i], 0.0).astype(BF16)) for i in heads]
        inv = [inv[i] - dot(xc[i].astype(BF16), xb[i]) for i in heads]
        s *= 2

    uw = [dot(inv[i].astype(BF16), jnp.concatenate([v_all[:, cs[i]] * beta[i], kb[i] * eg[i]], axis=1).astype(BF16))
          for i in heads]
    qg = [q[i] * eg[i] for i in heads]

    state = [st_ref[i] for i in heads]
    for c in range(T // C):
        r = slice(c * C, (c + 1) * C)
        ws = [dot(jnp.concatenate([uw[i][r, D:], qg[i][r]], axis=0).astype(BF16), state[i].astype(BF16))
              for i in heads]
        vnew = [(uw[i][r, :D] - ws[i][:C]).astype(BF16) for i in heads]
        glast = [gcol[i][c * C + C - 1:(c + 1) * C, :] for i in heads]
        kdec = [(k[i][r] * jnp.exp(glast[i] - gcol[i][r])).T.astype(BF16) for i in heads]
        state = [state[i] * jnp.exp(glast[i]) + dot(kdec[i], vnew[i]) for i in heads]
        for i in heads:
            o = ws[i][C:] + dot(a_qk[i][r, c * C:(c + 1) * C], vnew[i])
            zc = z_ref[pl.ds(c * C, C), cs[i]].astype(F32)
            o_ref[pl.ds(c * C, C), cs[i]] = (_rms(o, on_ref[...]) * _silu(zc)).astype(o_ref.dtype)
    for i in heads:
        st_ref[i] = state[i]


def gdn_heads(y, ab, conv_qkv, a_log, dt_bias, o_norm, batch, seq_len, T=256, heads_per_step=8):
    H, D = GDN_HEADS, GDN_HD
    HP = heads_per_step
    W = HP * D
    nG = H // HP
    nT = seq_len // T
    lanes = ab.shape[1]
    pad = lambda p: jnp.pad(p.astype(F32), (0, lanes - p.shape[0])).reshape(1, lanes)
    row = lambda b, h, t: b * nT + t
    col = lambda off: (lambda b, h, t: (row(b, h, t), off * nG + h))
    cw = lambda off: (lambda b, h, t: (0, off * nG + h))
    const = lambda b, h, t: (0, 0)
    K = conv_qkv.shape[0]
    return pl.pallas_call(
        _gdn_kernel,
        grid=(batch, nG, nT),
        in_specs=[pl.BlockSpec((T, W), col(0)), pl.BlockSpec((T, W), col(1)),
                  pl.BlockSpec((T, W), col(2)), pl.BlockSpec((T, W), col(3)),
                  pl.BlockSpec((T, lanes), lambda b, h, t: (row(b, h, t), 0)),
                  pl.BlockSpec((K, W), cw(0)), pl.BlockSpec((K, W), cw(1)), pl.BlockSpec((K, W), cw(2)),
                  pl.BlockSpec((1, lanes), const), pl.BlockSpec((1, lanes), const),
                  pl.BlockSpec((1, D), const)],
        out_specs=pl.BlockSpec((T, W), col(0)),
        out_shape=jax.ShapeDtypeStruct((batch * seq_len, H * D), BF16),
        scratch_shapes=[pltpu.VMEM((T + CONV_HALO, W), F32), pltpu.VMEM((T + CONV_HALO, W), F32),
                        pltpu.VMEM((T + CONV_HALO, W), F32), pltpu.VMEM((HP, D, D), F32)],
        compiler_params=_cparams("parallel", "parallel", "arbitrary"),
        name="gdn_heads",
    )(y, y, y, y, ab, conv_qkv, conv_qkv, conv_qkv, pad(a_log), pad(dt_bias), o_norm.reshape(1, D))


def _shortconv_kernel(hb_ref, gb_ref, gc_ref, cw_ref, o_ref, ext_ref):
    first = pl.program_id(2) == 0
    x = gc_ref[...].astype(F32) * hb_ref[...].astype(F32)
    y = _causal_conv_ext(ext_ref, x, cw_ref[...], first)
    o_ref[...] = (gb_ref[...].astype(F32) * y).astype(o_ref.dtype)


def short_conv(y, col_off, conv_sc, batch, seq_len, T=512, tc=512):
    W = conv_sc.shape[1]
    nT = seq_len // T
    nC = W // tc
    spec = lambda off: pl.BlockSpec((T, tc), lambda b, c, t: (b * nT + t, (col_off + off) // tc + c))
    return pl.pallas_call(
        _shortconv_kernel,
        grid=(batch, nC, nT),
        in_specs=[spec(0), spec(W), spec(2 * W),
                  pl.BlockSpec((conv_sc.shape[0], tc), lambda b, c, t: (0, c))],
        out_specs=pl.BlockSpec((T, tc), lambda b, c, t: (b * nT + t, c)),
        out_shape=jax.ShapeDtypeStruct((batch * seq_len, W), BF16),
        scratch_shapes=[pltpu.VMEM((T + CONV_HALO, tc), F32)],
        compiler_params=_cparams("parallel", "parallel", "arbitrary"),
        name="short_conv",
    )(y, y, y, conv_sc)


def _nsa_prep_kernel(y_ref, cos_ref, sin_ref, q_ref, kc_ref, vc_ref, ks_ref, vs_ref, kw_ref, vw_ref):
    T = y_ref.shape[0]
    H, G, D = NSA_HEADS, NSA_KV, NSA_HD
    cos = cos_ref[...]
    sin = sin_ref[...]

    def head(c):
        return y_ref[:, c * D:(c + 1) * D]

    def rope(x):
        x = x.astype(F32)
        return x * cos + pltpu.roll(x, D // 2, 1) * sin

    for hh in range(H):
        q_ref[0, hh] = (rope(head(hh)) * (D ** -0.5 * LOG2E)).astype(BF16)
    t0 = pl.program_id(1) * T
    tok = t0 + lax.broadcasted_iota(jnp.int32, (T, D), 0)
    lane = lax.broadcasted_iota(jnp.int32, (T, D), 1)
    onehot = jnp.where(tok // SEL_BLOCK == lane, 1.0, 0.0).astype(BF16)
    for g in range(G):
        kc_ref[0, g] = rope(head(H + g)).astype(BF16)
        vc_ref[0, g] = head(H + G + g)
        ks_ref[0, g, :, 0:D] = rope(head(H + 2 * G + g)).astype(BF16)
        ks_ref[0, g, :, D:2 * D] = onehot
        ones = jnp.ones((V_ONES_ROWS, T), BF16)
        vs_ref[0, g, 0:D, :] = head(H + 3 * G + g).astype(F32).T.astype(BF16)
        vs_ref[0, g, D:D + V_ONES_ROWS, :] = ones
        kw_ref[0, g] = rope(head(H + 4 * G + g)).astype(BF16)
        vw_ref[0, g, 0:D, :] = head(H + 5 * G + g).astype(F32).T.astype(BF16)
        vw_ref[0, g, D:D + V_ONES_ROWS, :] = ones


def nsa_prep(y, cos2, sin2, batch, seq_len, T=512):
    H, G, D = NSA_HEADS, NSA_KV, NSA_HD
    nT = seq_len // T
    assert seq_len // SEL_BLOCK <= D
    grp = lambda w: jax.ShapeDtypeStruct((batch, G, seq_len, w), BF16)
    gspec = lambda w: pl.BlockSpec((1, G, T, w), lambda b, t: (b, 0, t, 0))
    grp_t = jax.ShapeDtypeStruct((batch, G, D + V_ONES_ROWS, seq_len), BF16)
    tspec = pl.BlockSpec((1, G, D + V_ONES_ROWS, T), lambda b, t: (b, 0, 0, t))
    return pl.pallas_call(
        _nsa_prep_kernel,
        grid=(batch, nT),
        in_specs=[pl.BlockSpec((T, y.shape[1]), lambda b, t: (b * nT + t, 0)),
                  pl.BlockSpec((T, D), lambda b, t: (t, 0)),
                  pl.BlockSpec((T, D), lambda b, t: (t, 0))],
        out_specs=[pl.BlockSpec((1, H, T, D), lambda b, t: (b, 0, t, 0)),
                   gspec(D), gspec(D), gspec(2 * D), tspec, gspec(D), tspec],
        out_shape=[jax.ShapeDtypeStruct((batch, H, seq_len, D), BF16),
                   grp(D), grp(D), grp(2 * D), grp_t, grp(D), grp_t],
        compiler_params=_cparams("parallel", "parallel"),
        name="nsa_prep",
    )(y, cos2, sin2)


def _gelu_tanh(x):
    return 0.5 * x * (1.0 + jnp.tanh(math.sqrt(2.0 / math.pi) * (x + 0.044715 * (x * x * x))))


def _compress_kernel(x_ref, pos_ref, w1_ref, w2_ref, o_ref):
    x = x_ref[0].astype(F32)
    half = x.shape[1]
    pos = pos_ref[...]
    top = jnp.dot((x + pos[:, :half]).astype(BF16), w1_ref[0:half, :], preferred_element_type=F32)
    bot = jnp.dot((x + pos[:, half:]).astype(BF16), w1_ref[half:2 * half, :], preferred_element_type=F32)
    n = x.shape[0]
    hid = top + pltpu.roll(bot, n - 1, 0)
    o_ref[0] = jnp.dot(_gelu_tanh(hid).astype(BF16), w2_ref[...], preferred_element_type=F32).astype(BF16)


def nsa_compress_blocks(t, pos, w1, w2):
    B, G, S, D = t.shape
    assert CMP_LEN == 2 * CMP_STRIDE
    n = S // CMP_STRIDE
    x = t.reshape(B * G, n, CMP_STRIDE * D)
    hidden = w1.shape[1]
    return pl.pallas_call(
        _compress_kernel,
        grid=(B * G,),
        in_specs=[pl.BlockSpec((1, n, CMP_STRIDE * D), lambda i: (i, 0, 0)),
                  pl.BlockSpec((1, CMP_LEN * D), lambda i: (0, 0)),
                  pl.BlockSpec((CMP_LEN * D, hidden), lambda i: (0, 0)),
                  pl.BlockSpec((hidden, D), lambda i: (0, 0))],
        out_specs=pl.BlockSpec((1, n, D), lambda i: (i, 0, 0)),
        out_shape=jax.ShapeDtypeStruct((B * G, n, D), BF16),
        compiler_params=_cparams("parallel"),
        name="nsa_compress",
    )(x, pos.reshape(1, CMP_LEN * D).astype(F32), w1.astype(BF16), w2.astype(BF16))


def _cmp_select_kernel(q_ref, kc_ref, vc_ref, gl_ref, ov_ref, o_ref, sb_ref):
    g = pl.program_id(1)
    i = pl.program_id(2)
    R, tq, D = q_ref.shape[1], q_ref.shape[2], q_ref.shape[3]
    nc = kc_ref.shape[1]
    ns = ov_ref.shape[0]
    heads = range(R)
    nt = (((1,), (1,)), ((), ()))
    dot = functools.partial(jnp.dot, preferred_element_type=F32)
    kc = kc_ref[0]
    vct = vc_ref[0].astype(F32).T.astype(BF16)
    c_col = lax.broadcasted_iota(jnp.int32, (nc, tq), 0)
    t_row = i * tq + lax.broadcasted_iota(jnp.int32, (nc, tq), 1)
    valid = c_col * CMP_STRIDE + (CMP_LEN - 1) <= t_row
    s = [jnp.where(valid, lax.dot_general(kc, q_ref[0, r], nt, preferred_element_type=F32), NEG) for r in heads]
    e = [jnp.where(valid, jnp.exp2(x - jnp.max(x, axis=0, keepdims=True)), 0.0) for x in s]
    den = [jnp.sum(x, axis=0, keepdims=True) for x in e]
    p = [x / jnp.where(d > 0.0, d, 1.0) for x, d in zip(e, den)]
    o_t = [dot(vct, x.astype(BF16)) for x in p]
    sig = jax.nn.sigmoid(gl_ref[...])
    pick = jnp.where(lax.broadcasted_iota(jnp.int32, (8, sig.shape[1]), 1)
                     == g * R + lax.broadcasted_iota(jnp.int32, (8, sig.shape[1]), 0), 1.0, 0.0)
    gates = lax.dot_general(pick, sig, nt, preferred_element_type=F32,
                            precision=lax.Precision.HIGHEST)
    for r in heads:
        o_ref[:, r * D:(r + 1) * D] = (o_t[r] * gates[r:r + 1, :]).T.astype(o_ref.dtype)

    psum = functools.reduce(lambda a, b: a + b, p)
    ov = ov_ref[...]
    hi = psum.astype(BF16)
    r1 = psum - hi.astype(F32)
    mid = r1.astype(BF16)
    lo = (r1 - mid.astype(F32)).astype(BF16)
    score = dot(ov, hi) + (dot(ov, mid) + dot(ov, lo))
    n_col = lax.broadcasted_iota(jnp.int32, (ns, tq), 0)
    t_row = i * tq + lax.broadcasted_iota(jnp.int32, (ns, tq), 1)
    cur = t_row // SEL_BLOCK
    forced = (n_col == 0) | (n_col == cur) | (n_col == cur - 1)
    score = jnp.where(n_col * SEL_BLOCK <= t_row, score + jnp.where(forced, FORCE_BONUS, 0.0), NEG)
    SUB = 8
    rows = [score[a:a + SUB, :] for a in range(0, ns, SUB)]
    ranks = [jnp.zeros((SUB, tq), F32) for _ in rows]
    sub_row = lax.broadcasted_iota(jnp.int32, (SUB, tq), 0)
    for m in range(ns):
        sm = jnp.broadcast_to(score[m:m + 1, :], (SUB, tq))
        for a, sc in enumerate(rows):
            if a * SUB + SUB - 1 < m:
                ahead = sm > sc
            elif a * SUB > m:
                ahead = sm >= sc
            else:
                ahead = (sm > sc) | ((sm == sc) & (sub_row > m - a * SUB))
            ranks[a] = ranks[a] + jnp.where(ahead, 1.0, 0.0)
    rank = jnp.concatenate(ranks, axis=0)
    bias = jnp.where(rank < float(min(SEL_N, ns)), 0.0, NEG)
    bias = jnp.concatenate([bias, jnp.zeros((sb_ref.shape[3] - ns, tq), F32)], axis=0)
    sb_ref[0, 0] = bias.T.astype(sb_ref.dtype)


def _overlap_matrix(nc, ns):
    cs = np.arange(nc) * CMP_STRIDE
    ss = np.arange(ns) * SEL_BLOCK
    ov = np.minimum(cs[None, :] + CMP_LEN, ss[:, None] + SEL_BLOCK) - np.maximum(cs[None, :], ss[:, None])
    ov = np.clip(ov, 0, None) / CMP_STRIDE
    ov[:, nc - CMP_LEN // CMP_STRIDE + 1:] = 0.0
    return jnp.asarray(ov, BF16)


def nsa_cmp_select(q, kcmp, vcmp, gl, batch, seq_len, tq=256):
    H, G, D = NSA_HEADS, NSA_KV, NSA_HD
    R = H // G
    nq = seq_len // tq
    nc = kcmp.shape[1]
    ns = seq_len // SEL_BLOCK
    ov = _overlap_matrix(nc, ns)
    return pl.pallas_call(
        _cmp_select_kernel,
        grid=(batch, G, nq),
        in_specs=[pl.BlockSpec((1, R, tq, D), lambda b, g, i: (b, g, i, 0)),
                  pl.BlockSpec((1, nc, D), lambda b, g, i: (b * G + g, 0, 0)),
                  pl.BlockSpec((1, nc, D), lambda b, g, i: (b * G + g, 0, 0)),
                  pl.BlockSpec((tq, gl.shape[1]), lambda b, g, i: (b * nq + i, 0)),
                  pl.BlockSpec((ns, nc), lambda b, g, i: (0, 0))],
        out_specs=[pl.BlockSpec((tq, R * D), lambda b, g, i: (b * nq + i, g)),
                   pl.BlockSpec((1, 1, tq, D), lambda b, g, i: (b, g, i, 0))],
        out_shape=[jax.ShapeDtypeStruct((batch * seq_len, H * D), BF16),
                   jax.ShapeDtypeStruct((batch, G, seq_len, D), BF16)],
        compiler_params=_cparams("parallel", "parallel", "parallel"),
        name="nsa_cmp_select",
    )(q, kcmp, vcmp, gl, ov)


def _sel_win_kernel(q_ref, sb_ref, ks_ref, vs_ref, kw_ref, vw_ref, gl_ref, o_ref, m_ref, acc_ref, s_ref,
                    *, tk):
    g = pl.program_id(1)
    i = pl.program_id(2)
    R, tq, D = q_ref.shape[1], q_ref.shape[2], q_ref.shape[3]
    cols = R * tq
    H = NSA_HEADS
    nt = (((1,), (1,)), ((), ()))
    q = jnp.concatenate([q_ref[0, r] for r in range(R)], axis=0)
    sb = sb_ref[0, 0]
    q_aug = jnp.concatenate([q, jnp.concatenate([sb] * R, axis=0)], axis=1)
    t_cols = i * tq + (lax.broadcasted_iota(jnp.int32, (1, cols), 1) & (tq - 1))

    m_ref[...] = jnp.full_like(m_ref, NEG)
    acc_ref[...] = jnp.zeros_like(acc_ref)

    gw = cols // COL_GROUPS
    groups = [slice(c * gw, (c + 1) * gw) for c in range(COL_GROUPS)]
    q_grp = [q_aug[c] for c in groups]
    dot = functools.partial(jnp.dot, preferred_element_type=F32)

    def scores(j):
        k = ks_ref[0, 0, pl.ds(pl.multiple_of(j * tk, tk), tk), :]
        return [lax.dot_general(k, qa, nt, preferred_element_type=F32) for qa in q_grp]

    def accumulate(j, s):
        vt = vs_ref[0, 0, :, pl.ds(pl.multiple_of(j * tk, tk), tk)]
        m_old = [m_ref[0:1, c] for c in groups]
        m_new = [jnp.maximum(mo, jnp.max(x, axis=0, keepdims=True)) for mo, x in zip(m_old, s)]
        alpha = [jnp.exp2(mo - mn) for mo, mn in zip(m_old, m_new)]
        p = [jnp.exp2(x - mn) for x, mn in zip(s, m_new)]
        pv = [dot(vt, x.astype(BF16)) for x in p]
        for n, c in enumerate(groups):
            acc_ref[:, c] = acc_ref[:, c] * alpha[n] + pv[n]
            m_ref[0:1, c] = m_new[n]

    last = (i * tq) // tk
    for n, c in enumerate(groups):
        s_ref[:, c] = scores(0)[n]

    def body(j, carry):
        s_next = scores(j + 1)
        accumulate(j, [s_ref[:, c] for c in groups])
        for n, c in enumerate(groups):
            s_ref[:, c] = s_next[n]
        return carry

    lax.fori_loop(0, last, body, 0)
    kpos = last * tk + lax.broadcasted_iota(jnp.int32, (tk, gw), 0)
    accumulate(last, [jnp.where(kpos <= t_cols[:, c], s_ref[:, c], NEG) for c in groups])

    sig = jax.nn.sigmoid(gl_ref[...])
    row = lax.broadcasted_iota(jnp.int32, (2 * R, sig.shape[1]), 0)
    lane = lax.broadcasted_iota(jnp.int32, (2 * R, sig.shape[1]), 1)
    pick = jnp.where(lane == H + (row // R) * H + g * R + (row % R), 1.0, 0.0)
    gates = lax.dot_general(pick, sig, nt, preferred_element_type=F32,
                            precision=lax.Precision.HIGHEST)
    g_slc = jnp.concatenate([gates[r:r + 1, :] for r in range(R)], axis=1)
    g_win = jnp.concatenate([gates[R + r:R + r + 1, :] for r in range(R)], axis=1)
    o_t = acc_ref[0:D, :] * (g_slc / acc_ref[D:D + 1, :])

    wlen = WIN + tq
    start = pl.multiple_of(jnp.maximum(i * tq - WIN, 0), tq)
    kw = kw_ref[0, 0, pl.ds(start, wlen), :]
    vwt = vw_ref[0, 0, :, pl.ds(start, wlen)]
    kpos = start + lax.broadcasted_iota(jnp.int32, (wlen, gw), 0)
    s = [lax.dot_general(kw, q[c], nt, preferred_element_type=F32) for c in groups]
    s = [jnp.where((kpos <= t_cols[:, c]) & (kpos > t_cols[:, c] - WIN), x, NEG) for x, c in zip(s, groups)]
    p = [jnp.exp2(x - jnp.max(x, axis=0, keepdims=True)) for x in s]
    o_w = [dot(vwt, x.astype(BF16)) for x in p]
    o_w = [x[0:D] * (g_win[:, c] / x[D:D + 1]) for x, c in zip(o_w, groups)]
    o_t = o_t + jnp.concatenate(o_w, axis=1)

    for r in range(R):
        o_ref[:, r * D:(r + 1) * D] = o_t[:, r * tq:(r + 1) * tq].T.astype(o_ref.dtype)


def nsa_selected_window(q, sb, ks_aug, vs, kw, vw, gl, batch, seq_len, tq=256, tk=512):
    H, G, D = NSA_HEADS, NSA_KV, NSA_HD
    R = H // G
    nq = seq_len // tq
    assert tk % tq == 0 and WIN % tq == 0 and tq & (tq - 1) == 0 and seq_len >= WIN + tq and tq % 128 == 0
    full = lambda w: pl.BlockSpec((1, 1, seq_len, w), lambda b, g, i: (b, g, 0, 0))
    full_t = pl.BlockSpec((1, 1, D + V_ONES_ROWS, seq_len), lambda b, g, i: (b, g, 0, 0))
    return pl.pallas_call(
        functools.partial(_sel_win_kernel, tk=tk),
        grid=(batch, G, nq),
        in_specs=[pl.BlockSpec((1, R, tq, D), lambda b, g, i: (b, g, i, 0)),
                  pl.BlockSpec((1, 1, tq, D), lambda b, g, i: (b, g, i, 0)),
                  full(2 * D), full_t, full(D), full_t,
                  pl.BlockSpec((tq, gl.shape[1]), lambda b, g, i: (b * nq + i, 0))],
        out_specs=pl.BlockSpec((tq, R * D), lambda b, g, i: (b * nq + i, g)),
        out_shape=jax.ShapeDtypeStruct((batch * seq_len, H * D), BF16),
        scratch_shapes=[pltpu.VMEM((8, R * tq), F32), pltpu.VMEM((D + V_ONES_ROWS, R * tq), F32),
                        pltpu.VMEM((tk, R * tq), F32)],
        compiler_params=_cparams("parallel", "parallel", "arbitrary"),
        name="nsa_selected_window",
    )(q, sb, ks_aug, vs, kw, vw, gl)


def nsa_mixer_parts(y, gl, p, i, cos, sin, batch, seq_len):
    D = NSA_HD
    cos2 = jnp.concatenate([cos, cos], axis=1)
    sin2 = jnp.concatenate([-sin, sin], axis=1)
    q, kc, vc, ks_aug, vs, kw, vw = nsa_prep(y, cos2, sin2, batch, seq_len)
    kcmp = nsa_compress_blocks(kc, p["cmp_pos_k"][i], p["cmp_w1_k"][i], p["cmp_w2_k"][i])
    vcmp = nsa_compress_blocks(vc, p["cmp_pos_v"][i], p["cmp_w1_v"][i], p["cmp_w2_v"][i])
    o_cmp, sb = nsa_cmp_select(q, kcmp, vcmp, gl, batch, seq_len)
    o_sw = nsa_selected_window(q, sb, ks_aug, vs, kw, vw, gl, batch, seq_len)
    return o_cmp, o_sw


def rope_tables(S, dim):
    inv = 1.0 / (ROPE_THETA ** (jnp.arange(0, dim, 2, dtype=jnp.float32) / dim))
    ang = jnp.arange(S, dtype=jnp.float32)[:, None] * inv[None, :]
    return jnp.cos(ang), jnp.sin(ang)


def kernel(x, norm_w, ab_w_in, ab_w_out, gdn_conv, gdn_a_log, gdn_dt_bias, gdn_norm, sc_conv, nsa_w_in, nsa_w_out, cmp_pos_k, cmp_w1_k, cmp_w2_k, cmp_pos_v, cmp_w1_v, cmp_w2_v, ffn_w_gu, ffn_conv, ffn_w_down):
    B, S, D = x.shape
    p = dict(norm_w=norm_w, ab_w_in=ab_w_in, ab_w_out=ab_w_out, gdn_conv=gdn_conv, gdn_a_log=gdn_a_log,
             gdn_dt_bias=gdn_dt_bias, gdn_norm=gdn_norm, sc_conv=sc_conv, nsa_w_in=nsa_w_in,
             nsa_w_out=nsa_w_out, cmp_pos_k=cmp_pos_k, cmp_w1_k=cmp_w1_k, cmp_w2_k=cmp_w2_k,
             cmp_pos_v=cmp_pos_v, cmp_w1_v=cmp_w1_v, cmp_w2_v=cmp_w2_v, ffn_w_gu=ffn_w_gu,
             ffn_conv=ffn_conv, ffn_w_down=ffn_w_down)
    cos, sin = rope_tables(S, NSA_HD)
    h = x.reshape(B * S, D)
    for l in range(norm_w.shape[0]):
        _, h = run_layer(h, l, B, S, cos, sin, p)
    return h.reshape(B, S, D)


def run_layer(h, l, B, S, cos, sin, p):
    D = h.shape[1]
    i = l // 2
    norm_w = p["norm_w"]
    if l % 2 == 0:
        gw = GDN_HEADS * GDN_HD
        w = p["ab_w_in"][i].astype(BF16)
        w_main = jnp.concatenate([w[:, :4 * gw], w[:, 4 * gw + 2 * GDN_HEADS:]], axis=1)
        w_ab = jnp.pad(w[:, 4 * gw:4 * gw + 2 * GDN_HEADS], ((0, 0), (0, 128 - 2 * GDN_HEADS)))
        y, ab = norm_matmul(h, norm_w[l, 0], w_main, w_ab)
        y_a = gdn_heads(y, ab, p["gdn_conv"][i], p["gdn_a_log"][i], p["gdn_dt_bias"][i], p["gdn_norm"][i], B, S)
        y_b = short_conv(y, 4 * gw, p["sc_conv"][i], B, S)
        h1 = out_proj([y_a, y_b], p["ab_w_out"][i].astype(BF16), h, norm_w[l, 1], concat=True)
    else:
        w = p["nsa_w_in"][i].astype(BF16)
        nm = NSA_HEADS * NSA_HD + 6 * NSA_KV * NSA_HD
        w_gl = jnp.pad(w[:, nm:], ((0, 0), (0, 128 - (w.shape[1] - nm))))
        y, gl = norm_matmul(h, norm_w[l, 0], w[:, :nm], w_gl)
        o_cmp, o_sw = nsa_mixer_parts(y, gl, p, i, cos, sin, B, S)
        h1 = out_proj([o_cmp, o_sw], p["nsa_w_out"][i].astype(BF16), h, norm_w[l, 1], concat=False)
    h2 = conv_ffn_block(h1, S, norm_w[l, 2], p["ffn_w_gu"].astype(BF16), p["ffn_conv"],
                        p["ffn_w_down"].astype(BF16), norm_w[l, 3], l)
    return h1, h2
```

```python
import functools
import math

import jax
import jax.numpy as jnp
import numpy as np
from jax import lax
from jax.experimental import pallas as pl
from jax.experimental.pallas import tpu as pltpu

F32 = jnp.float32
BF16 = jnp.bfloat16

EPS = 1e-6
NEG = -1e30

GDN_HEADS = 8
GDN_HD = 128
GDN_CHUNK = 64
NSA_HEADS = 16
NSA_KV = 4
NSA_HD = 128
CMP_LEN = 32
CMP_STRIDE = 16
SEL_BLOCK = 64
SEL_N = 16
WIN = 512
FORCE_BONUS = 1000.0
ROPE_THETA = 10000.0
LOG2E = math.log2(math.e)
COL_GROUPS = 4
V_ONES_ROWS = 16

VMEM_LIMIT_BYTES = 56 * 1024 * 1024
BF16_ROWS = 16
LANES = 128


def _cparams(*sem, vmem=VMEM_LIMIT_BYTES):
    return pltpu.CompilerParams(dimension_semantics=sem, vmem_limit_bytes=vmem)


def _rms(x, w):
    return x * lax.rsqrt(jnp.mean(x * x, axis=-1, keepdims=True) + EPS) * w


def _norm_mm_kernel(h_ref, nw_ref, w_ref, ws_ref, o_ref, os_ref, xn_ref):
    @pl.when(pl.program_id(1) == 0)
    def _():
        xn_ref[...] = _rms(h_ref[...], nw_ref[...]).astype(BF16)
        os_ref[...] = jnp.dot(xn_ref[...], ws_ref[...], preferred_element_type=F32)

    o_ref[...] = jnp.dot(xn_ref[...], w_ref[...], preferred_element_type=F32).astype(o_ref.dtype)


def norm_matmul(h, nw, w, w_small, tm=1024, tn=1024):
    M, D = h.shape
    N = w.shape[1]
    Ns = w_small.shape[1]
    tn = min(tn, N)
    return pl.pallas_call(
        _norm_mm_kernel,
        grid=(M // tm, N // tn),
        in_specs=[pl.BlockSpec((tm, D), lambda i, j: (i, 0)),
                  pl.BlockSpec((1, D), lambda i, j: (0, 0)),
                  pl.BlockSpec((D, tn), lambda i, j: (0, j)),
                  pl.BlockSpec((D, Ns), lambda i, j: (0, 0))],
        out_specs=[pl.BlockSpec((tm, tn), lambda i, j: (i, j)),
                   pl.BlockSpec((tm, Ns), lambda i, j: (i, 0))],
        out_shape=[jax.ShapeDtypeStruct((M, N), BF16), jax.ShapeDtypeStruct((M, Ns), F32)],
        scratch_shapes=[pltpu.VMEM((tm, D), BF16)],
        compiler_params=_cparams("parallel", "arbitrary"),
        name="norm_matmul",
    )(h, nw.reshape(1, D), w, w_small)


def _out_proj_kernel(*refs, n_parts, concat):
    parts = refs[:n_parts]
    w_ref, h_ref, nw_ref, o_ref = refs[n_parts:]
    if concat:
        x = jnp.concatenate([p[...] for p in parts], axis=1)
    else:
        x = parts[0][...].astype(F32)
        for p in parts[1:]:
            x = x + p[...].astype(F32)
        x = x.astype(BF16)
    m = jnp.dot(x, w_ref[...], preferred_element_type=F32)
    o_ref[...] = h_ref[...] + _rms(m, nw_ref[...])


def out_proj(parts, w, h, nw, concat, tm=512):
    M, D = h.shape
    K = w.shape[0]
    in_specs = [pl.BlockSpec((tm, p.shape[1]), lambda i: (i, 0)) for p in parts]
    in_specs += [pl.BlockSpec((K, D), lambda i: (0, 0)),
                 pl.BlockSpec((tm, D), lambda i: (i, 0)),
                 pl.BlockSpec((1, D), lambda i: (0, 0))]
    return pl.pallas_call(
        functools.partial(_out_proj_kernel, n_parts=len(parts), concat=concat),
        grid=(M // tm,),
        in_specs=in_specs,
        out_specs=pl.BlockSpec((tm, D), lambda i: (i, 0)),
        out_shape=jax.ShapeDtypeStruct((M, D), F32),
        compiler_params=_cparams("parallel"),
        name="out_proj",
    )(*parts, w, h, nw.reshape(1, D))


def _ffn_kernel(h_ref, halo_ref, nw2_ref, wg_ref, wu_ref, cw_ref, wd_ref, nw3_ref, o_ref,
                xn_ref, g_ref, acc_ref, *, tiles_per_seq):
    i = pl.program_id(0)
    j = pl.program_id(1)
    tm = h_ref.shape[0]
    H = BF16_ROWS

    @pl.when(j == 0)
    def _():
        xn_ref[pl.ds(H, tm), :] = _rms(h_ref[...], nw2_ref[...]).astype(BF16)
        halo = _rms(halo_ref[...], nw2_ref[...])
        halo = jnp.where(i % tiles_per_seq == 0, 0.0, halo)
        xn_ref[pl.ds(0, H), :] = halo.astype(BF16)
        acc_ref[...] = jnp.zeros_like(acc_ref)

    g_ref[...] = jnp.dot(xn_ref[...], wg_ref[...], preferred_element_type=F32)
    u = jnp.dot(xn_ref[pl.ds(H, tm), :], wu_ref[...], preferred_element_type=F32)
    cw = cw_ref[...]
    c = (g_ref[pl.ds(H, tm), :] * cw[2:3, :] + g_ref[pl.ds(H - 1, tm), :] * cw[1:2, :]
         + g_ref[pl.ds(H - 2, tm), :] * cw[0:1, :])
    hid = (c * jax.nn.sigmoid(c) * u).astype(BF16)
    acc_ref[...] += jnp.dot(hid, wd_ref[...], preferred_element_type=F32)

    @pl.when(j == pl.num_programs(1) - 1)
    def _():
        o_ref[...] = h_ref[...] + _rms(acc_ref[...], nw3_ref[...])


def conv_ffn_block(h, seq_len, nw2, w_gu, conv_w, w_down, nw3, layer, tm=512, tf=512):
    M, D = h.shape
    FF = w_down.shape[1]
    nf = FF // tf
    H = BF16_ROWS
    kern = functools.partial(_ffn_kernel, tiles_per_seq=seq_len // tm)
    return pl.pallas_call(
        kern,
        grid=(M // tm, nf),
        in_specs=[pl.BlockSpec((tm, D), lambda i, j: (i, 0)),
                  pl.BlockSpec((H, D), lambda i, j: (jnp.maximum(i * (tm // H) - 1, 0), 0)),
                  pl.BlockSpec((1, D), lambda i, j: (0, 0)),
                  pl.BlockSpec((None, D, tf), lambda i, j: (layer, 0, j)),
                  pl.BlockSpec((None, D, tf), lambda i, j: (layer, 0, nf + j)),
                  pl.BlockSpec((None, conv_w.shape[1], tf), lambda i, j: (layer, 0, j)),
                  pl.BlockSpec((None, tf, D), lambda i, j: (layer, j, 0)),
                  pl.BlockSpec((1, D), lambda i, j: (0, 0))],
        out_specs=pl.BlockSpec((tm, D), lambda i, j: (i, 0)),
        out_shape=jax.ShapeDtypeStruct((M, D), F32),
        scratch_shapes=[pltpu.VMEM((tm + H, D), BF16),
                        pltpu.VMEM((tm + H, tf), F32),
                        pltpu.VMEM((tm, D), F32)],
        compiler_params=_cparams("parallel", "arbitrary"),
        name="conv_ffn",
    )(h, h, nw2.reshape(1, D), w_gu, w_gu, conv_w, w_down, nw3.reshape(1, D))


CONV_HALO = 8


def _silu(x):
    return x * jax.nn.sigmoid(x)


def _causal_conv_ext(ext_ref, x, cw, first):
    T = x.shape[0]
    K = cw.shape[0]

    @pl.when(first)
    def _():
        ext_ref[pl.ds(0, CONV_HALO), :] = jnp.zeros((CONV_HALO, x.shape[1]), F32)

    @pl.when(jnp.logical_not(first))
    def _():
        ext_ref[pl.ds(0, CONV_HALO), :] = ext_ref[pl.ds(T, CONV_HALO), :]

    ext_ref[pl.ds(CONV_HALO, T), :] = x
    y = ext_ref[pl.ds(CONV_HALO, T), :] * cw[K - 1:K, :]
    for k in range(K - 1):
        y = y + ext_ref[pl.ds(CONV_HALO - (K - 1) + k, T), :] * cw[k:k + 1, :]
    return y


def _gdn_kernel(q_ref, k_ref, v_ref, z_ref, ab_ref, cq_ref, ck_ref, cv_ref, alog_ref, dtb_ref, on_ref,
                o_ref, qe_ref, ke_ref, ve_ref, st_ref):
    t = pl.program_id(2)
    T = q_ref.shape[0]
    D = GDN_HD
    HP = q_ref.shape[1] // D
    C = GDN_CHUNK
    first = t == 0
    nt = (((1,), (1,)), ((), ()))

    @pl.when(first)
    def _():
        st_ref[...] = jnp.zeros_like(st_ref)

    q_all = _silu(_causal_conv_ext(qe_ref, q_ref[...].astype(F32), cq_ref[...], first))
    k_all = _silu(_causal_conv_ext(ke_ref, k_ref[...].astype(F32), ck_ref[...], first))
    v_all = _silu(_causal_conv_ext(ve_ref, v_ref[...].astype(F32), cv_ref[...], first))

    ab = ab_ref[...]
    lane = lax.broadcasted_iota(jnp.int32, ab.shape, 1)
    x = ab + dtb_ref[...]
    g_all = -jnp.exp(alog_ref[...]) * (jnp.maximum(x, 0.0) + jnp.log1p(jnp.exp(-jnp.abs(x))))
    g_all = jnp.where(lane < GDN_HEADS, g_all, 0.0)
    sig_ab = jax.nn.sigmoid(ab)
    ti = lax.broadcasted_iota(jnp.int32, (T, T), 0)
    tj = lax.broadcasted_iota(jnp.int32, (T, T), 1)
    tril = (ti >= tj) & (ti // C == tj // C)
    stril = tril & (ti > tj)
    gc_all = jnp.dot(jnp.where(tril, 1.0, 0.0), g_all, preferred_element_type=F32,
                     precision=lax.Precision.HIGHEST)
    lane8 = lax.broadcasted_iota(jnp.int32, (8, ab.shape[1]), 1)

    heads = range(HP)
    dot = functools.partial(jnp.dot, preferred_element_type=F32)
    hid = [pl.program_id(1) * HP + hh for hh in heads]
    cs = [slice(hh * D, (hh + 1) * D) for hh in heads]
    q = [q_all[:, c] for c in cs]
    k = [k_all[:, c] for c in cs]
    q = [x * lax.rsqrt(jnp.sum(x * x, axis=-1, keepdims=True) + EPS) * (D ** -0.5) for x in q]
    k = [x * lax.rsqrt(jnp.sum(x * x, axis=-1, keepdims=True) + EPS) for x in k]
    gcol = [jnp.sum(jnp.where(lane == h, gc_all, 0.0), axis=-1, keepdims=True) for h in hid]
    beta = [jnp.sum(jnp.where(lane == GDN_HEADS + h, sig_ab, 0.0), axis=-1, keepdims=True) for h in hid]
    grow = [lax.dot_general(jnp.where(lane8 == h, 1.0, 0.0), gc_all, nt, preferred_element_type=F32,
                            precision=lax.Precision.HIGHEST)[0:1, :] for h in hid]

    kb = [k[i] * beta[i] for i in heads]
    eg = [jnp.exp(gcol[i]) for i in heads]
    decay = [jnp.where(tril, jnp.exp(jnp.where(tril, gcol[i] - grow[i], 0.0)), 0.0) for i in heads]
    qk = [lax.dot_general(jnp.concatenate([q[i], kb[i]], axis=0).astype(BF16), k[i].astype(BF16), nt,
                          preferred_element_type=F32) for i in heads]
    a_qk = [(qk[i][:T] * decay[i]).astype(BF16) for i in heads]
    lm = [jnp.where(stril, qk[i][T:] * decay[i], 0.0) for i in heads]

    inv = [jnp.where(ti == tj, 1.0, 0.0) - jnp.where(ti // 2 == tj // 2, lm[i], 0.0) for i in heads]
    s = 2
    while s < C:
        off = (ti // (2 * s) == tj // (2 * s)) & (ti // s != tj // s)
        xb = [inv[i].astype(BF16) for i in heads]
        xc = [dot(xb[i], jnp.where(off, lm[i], 0.0).astype(BF16)) for i in heads]
        inv = [inv[i] - dot(xc[i].astype(BF16), xb[i]) for i in heads]
        s *= 2

    uw = [dot(inv[i].astype(BF16), jnp.concatenate([v_all[:, cs[i]] * beta[i], kb[i] * eg[i]], axis=1).astype(BF16))
          for i in heads]
    qg = [q[i] * eg[i] for i in heads]

    state = [st_ref[i] for i in heads]
    for c in range(T // C):
        r = slice(c * C, (c + 1) * C)
        ws = [dot(jnp.concatenate([uw[i][r, D:], qg[i][r]], axis=0).astype(BF16), state[i].astype(BF16))
              for i in heads]
        vnew = [(uw[i][r, :D] - ws[i][:C]).astype(BF16) for i in heads]
        glast = [gcol[i][c * C + C - 1:(c + 1) * C, :] for i in heads]
        kdec = [(k[i][r] * jnp.exp(glast[i] - gcol[i][r])).T.astype(BF16) for i in heads]
        state = [state[i] * jnp.exp(glast[i]) + dot(kdec[i], vnew[i]) for i in heads]
        for i in heads:
            o = ws[i][C:] + dot(a_qk[i][r, c * C:(c + 1) * C], vnew[i])
            zc = z_ref[pl.ds(c * C, C), cs[i]].astype(F32)
            o_ref[pl.ds(c * C, C), cs[i]] = (_rms(o, on_ref[...]) * _silu(zc)).astype(o_ref.dtype)
    for i in heads:
        st_ref[i] = state[i]


def gdn_heads(y, ab, conv_qkv, a_log, dt_bias, o_norm, batch, seq_len, T=256, heads_per_step=8):
    H, D = GDN_HEADS, GDN_HD
    HP = heads_per_step
    W = HP * D
    nG = H // HP
    nT = seq_len // T
    lanes = ab.shape[1]
    pad = lambda p: jnp.pad(p.astype(F32), (0, lanes - p.shape[0])).reshape(1, lanes)
    row = lambda b, h, t: b * nT + t
    col = lambda off: (lambda b, h, t: (row(b, h, t), off * nG + h))
    cw = lambda off: (lambda b, h, t: (0, off * nG + h))
    const = lambda b, h, t: (0, 0)
    K = conv_qkv.shape[0]
    return pl.pallas_call(
        _gdn_kernel,
        grid=(batch, nG, nT),
        in_specs=[pl.BlockSpec((T, W), col(0)), pl.BlockSpec((T, W), col(1)),
                  pl.BlockSpec((T, W), col(2)), pl.BlockSpec((T, W), col(3)),
                  pl.BlockSpec((T, lanes), lambda b, h, t: (row(b, h, t), 0)),
                  pl.BlockSpec((K, W), cw(0)), pl.BlockSpec((K, W), cw(1)), pl.BlockSpec((K, W), cw(2)),
                  pl.BlockSpec((1, lanes), const), pl.BlockSpec((1, lanes), const),
                  pl.BlockSpec((1, D), const)],
        out_specs=pl.BlockSpec((T, W), col(0)),
        out_shape=jax.ShapeDtypeStruct((batch * seq_len, H * D), BF16),
        scratch_shapes=[pltpu.VMEM((T + CONV_HALO, W), F32), pltpu.VMEM((T + CONV_HALO, W), F32),
                        pltpu.VMEM((T + CONV_HALO, W), F32), pltpu.VMEM((HP, D, D), F32)],
        compiler_params=_cparams("parallel", "parallel", "arbitrary"),
        name="gdn_heads",
    )(y, y, y, y, ab, conv_qkv, conv_qkv, conv_qkv, pad(a_log), pad(dt_bias), o_norm.reshape(1, D))


def _shortconv_kernel(hb_ref, gb_ref, gc_ref, cw_ref, o_ref, ext_ref):
    first = pl.program_id(2) == 0
    x = gc_ref[...].astype(F32) * hb_ref[...].astype(F32)
    y = _causal_conv_ext(ext_ref, x, cw_ref[...], first)
    o_ref[...] = (gb_ref[...].astype(F32) * y).astype(o_ref.dtype)


def short_conv(y, col_off, conv_sc, batch, seq_len, T=512, tc=512):
    W = conv_sc.shape[1]
    nT = seq_len // T
    nC = W // tc
    spec = lambda off: pl.BlockSpec((T, tc), lambda b, c, t: (b * nT + t, (col_off + off) // tc + c))
    return pl.pallas_call(
        _shortconv_kernel,
        grid=(batch, nC, nT),
        in_specs=[spec(0), spec(W), spec(2 * W),
                  pl.BlockSpec((conv_sc.shape[0], tc), lambda b, c, t: (0, c))],
        out_specs=pl.BlockSpec((T, tc), lambda b, c, t: (b * nT + t, c)),
        out_shape=jax.ShapeDtypeStruct((batch * seq_len, W), BF16),
        scratch_shapes=[pltpu.VMEM((T + CONV_HALO, tc), F32)],
        compiler_params=_cparams("parallel", "parallel", "arbitrary"),
        name="short_conv",
    )(y, y, y, conv_sc)


def _nsa_prep_kernel(y_ref, cos_ref, sin_ref, q_ref, kc_ref, vc_ref, ks_ref, vs_ref, kw_ref, vw_ref):
    T = y_ref.shape[0]
    H, G, D = NSA_HEADS, NSA_KV, NSA_HD
    cos = cos_ref[...]
    sin = sin_ref[...]

    def head(c):
        return y_ref[:, c * D:(c + 1) * D]

    def rope(x):
        x = x.astype(F32)
        return x * cos + pltpu.roll(x, D // 2, 1) * sin

    for hh in range(H):
        q_ref[0, hh] = (rope(head(hh)) * (D ** -0.5 * LOG2E)).astype(BF16)
    t0 = pl.program_id(1) * T
    tok = t0 + lax.broadcasted_iota(jnp.int32, (T, D), 0)
    lane = lax.broadcasted_iota(jnp.int32, (T, D), 1)
    onehot = jnp.where(tok // SEL_BLOCK == lane, 1.0, 0.0).astype(BF16)
    for g in range(G):
        kc_ref[0, g] = rope(head(H + g)).astype(BF16)
        vc_ref[0, g] = head(H + G + g)
        ks_ref[0, g, :, 0:D] = rope(head(H + 2 * G + g)).astype(BF16)
        ks_ref[0, g, :, D:2 * D] = onehot
        ones = jnp.ones((V_ONES_ROWS, T), BF16)
        vs_ref[0, g, 0:D, :] = head(H + 3 * G + g).astype(F32).T.astype(BF16)
        vs_ref[0, g, D:D + V_ONES_ROWS, :] = ones
        kw_ref[0, g] = rope(head(H + 4 * G + g)).astype(BF16)
        vw_ref[0, g, 0:D, :] = head(H + 5 * G + g).astype(F32).T.astype(BF16)
        vw_ref[0, g, D:D + V_ONES_ROWS, :] = ones


def nsa_prep(y, cos2, sin2, batch, seq_len, T=512):
    H, G, D = NSA_HEADS, NSA_KV, NSA_HD
    nT = seq_len // T
    assert seq_len // SEL_BLOCK <= D
    grp = lambda w: jax.ShapeDtypeStruct((batch, G, seq_len, w), BF16)
    gspec = lambda w: pl.BlockSpec((1, G, T, w), lambda b, t: (b, 0, t, 0))
    grp_t = jax.ShapeDtypeStruct((batch, G, D + V_ONES_ROWS, seq_len), BF16)
    tspec = pl.BlockSpec((1, G, D + V_ONES_ROWS, T), lambda b, t: (b, 0, 0, t))
    return pl.pallas_call(
        _nsa_prep_kernel,
        grid=(batch, nT),
        in_specs=[pl.BlockSpec((T, y.shape[1]), lambda b, t: (b * nT + t, 0)),
                  pl.BlockSpec((T, D), lambda b, t: (t, 0)),
                  pl.BlockSpec((T, D), lambda b, t: (t, 0))],
        out_specs=[pl.BlockSpec((1, H, T, D), lambda b, t: (b, 0, t, 0)),
                   gspec(D), gspec(D), gspec(2 * D), tspec, gspec(D), tspec],
        out_shape=[jax.ShapeDtypeStruct((batch, H, seq_len, D), BF16),
                   grp(D), grp(D), grp(2 * D), grp_t, grp(D), grp_t],
        compiler_params=_cparams("parallel", "parallel"),
        name="nsa_prep",
    )(y, cos2, sin2)


def _gelu_tanh(x):
    return 0.5 * x * (1.0 + jnp.tanh(math.sqrt(2.0 / math.pi) * (x + 0.044715 * (x * x * x))))


def _compress_kernel(x_ref, pos_ref, w1_ref, w2_ref, o_ref):
    x = x_ref[0].astype(F32)
    half = x.shape[1]
    pos = pos_ref[...]
    top = jnp.dot((x + pos[:, :half]).astype(BF16), w1_ref[0:half, :], preferred_element_type=F32)
    bot = jnp.dot((x + pos[:, half:]).astype(BF16), w1_ref[half:2 * half, :], preferred_element_type=F32)
    n = x.shape[0]
    hid = top + pltpu.roll(bot, n - 1, 0)
    o_ref[0] = jnp.dot(_gelu_tanh(hid).astype(BF16), w2_ref[...], preferred_element_type=F32).astype(BF16)


def nsa_compress_blocks(t, pos, w1, w2):
    B, G, S, D = t.shape
    assert CMP_LEN == 2 * CMP_STRIDE
    n = S // CMP_STRIDE
    x = t.reshape(B * G, n, CMP_STRIDE * D)
    hidden = w1.shape[1]
    return pl.pallas_call(
        _compress_kernel,
        grid=(B * G,),
        in_specs=[pl.BlockSpec((1, n, CMP_STRIDE * D), lambda i: (i, 0, 0)),
                  pl.BlockSpec((1, CMP_LEN * D), lambda i: (0, 0)),
                  pl.BlockSpec((CMP_LEN * D, hidden), lambda i: (0, 0)),
                  pl.BlockSpec((hidden, D), lambda i: (0, 0))],
        out_specs=pl.BlockSpec((1, n, D), lambda i: (i, 0, 0)),
        out_shape=jax.ShapeDtypeStruct((B * G, n, D), BF16),
        compiler_params=_cparams("parallel"),
        name="nsa_compress",
    )(x, pos.reshape(1, CMP_LEN * D).astype(F32), w1.astype(BF16), w2.astype(BF16))


def _cmp_select_kernel(q_ref, kc_ref, vc_ref, gl_ref, ov_ref, o_ref, sb_ref):
    g = pl.program_id(1)
    i = pl.program_id(2)
    R, tq, D = q_ref.shape[1], q_ref.shape[2], q_ref.shape[3]
    nc = kc_ref.shape[1]
    ns = ov_ref.shape[0]
    heads = range(R)
    nt = (((1,), (1,)), ((), ()))
    dot = functools.partial(jnp.dot, preferred_element_type=F32)
    kc = kc_ref[0]
    vct = vc_ref[0].astype(F32).T.astype(BF16)
    c_col = lax.broadcasted_iota(jnp.int32, (nc, tq), 0)
    t_row = i * tq + lax.broadcasted_iota(jnp.int32, (nc, tq), 1)
    valid = c_col * CMP_STRIDE + (CMP_LEN - 1) <= t_row
    s = [jnp.where(valid, lax.dot_general(kc, q_ref[0, r], nt, preferred_element_type=F32), NEG) for r in heads]
    e = [jnp.where(valid, jnp.exp2(x - jnp.max(x, axis=0, keepdims=True)), 0.0) for x in s]
    den = [jnp.sum(x, axis=0, keepdims=True) for x in e]
    p = [x / jnp.where(d > 0.0, d, 1.0) for x, d in zip(e, den)]
    o_t = [dot(vct, x.astype(BF16)) for x in p]
    sig = jax.nn.sigmoid(gl_ref[...])
    pick = jnp.where(lax.broadcasted_iota(jnp.int32, (8, sig.shape[1]), 1)
                     == g * R + lax.broadcasted_iota(jnp.int32, (8, sig.shape[1]), 0), 1.0, 0.0)
    gates = lax.dot_general(pick, sig, nt, preferred_element_type=F32,
                            precision=lax.Precision.HIGHEST)
    for r in heads:
        o_ref[:, r * D:(r + 1) * D] = (o_t[r] * gates[r:r + 1, :]).T.astype(o_ref.dtype)

    psum = functools.reduce(lambda a, b: a + b, p)
    ov = ov_ref[...]
    hi = psum.astype(BF16)
    r1 = psum - hi.astype(F32)
    mid = r1.astype(BF16)
    lo = (r1 - mid.astype(F32)).astype(BF16)
    score = dot(ov, hi) + (dot(ov, mid) + dot(ov, lo))
    n_col = lax.broadcasted_iota(jnp.int32, (ns, tq), 0)
    t_row = i * tq + lax.broadcasted_iota(jnp.int32, (ns, tq), 1)
    cur = t_row // SEL_BLOCK
    forced = (n_col == 0) | (n_col == cur) | (n_col == cur - 1)
    score = jnp.where(n_col * SEL_BLOCK <= t_row, score + jnp.where(forced, FORCE_BONUS, 0.0), NEG)
    SUB = 8
    rows = [score[a:a + SUB, :] for a in range(0, ns, SUB)]
    ranks = [jnp.zeros((SUB, tq), F32) for _ in rows]
    sub_row = lax.broadcasted_iota(jnp.int32, (SUB, tq), 0)
    for m in range(ns):
        sm = jnp.broadcast_to(score[m:m + 1, :], (SUB, tq))
        for a, sc in enumerate(rows):
            if a * SUB + SUB - 1 < m:
                ahead = sm > sc
            elif a * SUB > m:
                ahead = sm >= sc
            else:
                ahead = (sm > sc) | ((sm == sc) & (sub_row > m - a * SUB))
            ranks[a] = ranks[a] + jnp.where(ahead, 1.0, 0.0)
    rank = jnp.concatenate(ranks, axis=0)
    bias = jnp.where(rank < float(min(SEL_N, ns)), 0.0, NEG)
    bias = jnp.concatenate([bias, jnp.zeros((sb_ref.shape[3] - ns, tq), F32)], axis=0)
    sb_ref[0, 0] = bias.T.astype(sb_ref.dtype)


def _overlap_matrix(nc, ns):
    cs = np.arange(nc) * CMP_STRIDE
    ss = np.arange(ns) * SEL_BLOCK
    ov = np.minimum(cs[None, :] + CMP_LEN, ss[:, None] + SEL_BLOCK) - np.maximum(cs[None, :], ss[:, None])
    ov = np.clip(ov, 0, None) / CMP_STRIDE
    ov[:, nc - CMP_LEN // CMP_STRIDE + 1:] = 0.0
    return jnp.asarray(ov, BF16)


def nsa_cmp_select(q, kcmp, vcmp, gl, batch, seq_len, tq=512):
    H, G, D = NSA_HEADS, NSA_KV, NSA_HD
    R = H // G
    nq = seq_len // tq
    nc = kcmp.shape[1]
    ns = seq_len // SEL_BLOCK
    ov = _overlap_matrix(nc, ns)
    return pl.pallas_call(
        _cmp_select_kernel,
        grid=(batch, G, nq),
        in_specs=[pl.BlockSpec((1, R, tq, D), lambda b, g, i: (b, g, i, 0)),
                  pl.BlockSpec((1, nc, D), lambda b, g, i: (b * G + g, 0, 0)),
                  pl.BlockSpec((1, nc, D), lambda b, g, i: (b * G + g, 0, 0)),
                  pl.BlockSpec((tq, gl.shape[1]), lambda b, g, i: (b * nq + i, 0)),
                  pl.BlockSpec((ns, nc), lambda b, g, i: (0, 0))],
        out_specs=[pl.BlockSpec((tq, R * D), lambda b, g, i: (b * nq + i, g)),
                   pl.BlockSpec((1, 1, tq, D), lambda b, g, i: (b, g, i, 0))],
        out_shape=[jax.ShapeDtypeStruct((batch * seq_len, H * D), BF16),
                   jax.ShapeDtypeStruct((batch, G, seq_len, D), BF16)],
        compiler_params=_cparams("parallel", "parallel", "parallel"),
        name="nsa_cmp_select",
    )(q, kcmp, vcmp, gl, ov)


def _sel_win_kernel(q_ref, sb_ref, ks_ref, vs_ref, kw_ref, vw_ref, gl_ref, o_ref, m_ref, acc_ref, s_ref,
                    *, tk):
    g = pl.program_id(1)
    i = pl.program_id(2)
    R, tq, D = q_ref.shape[1], q_ref.shape[2], q_ref.shape[3]
    cols = R * tq
    H = NSA_HEADS
    nt = (((1,), (1,)), ((), ()))
    q = jnp.concatenate([q_ref[0, r] for r in range(R)], axis=0)
    sb = sb_ref[0, 0]
    q_aug = jnp.concatenate([q, jnp.concatenate([sb] * R, axis=0)], axis=1)
    t_grp = i * tq + lax.broadcasted_iota(jnp.int32, (1, tq), 1)

    m_ref[...] = jnp.full_like(m_ref, NEG)
    acc_ref[...] = jnp.zeros_like(acc_ref)

    gw = cols // COL_GROUPS
    groups = [slice(c * gw, (c + 1) * gw) for c in range(COL_GROUPS)]
    q_grp = [q_aug[c] for c in groups]
    dot = functools.partial(jnp.dot, preferred_element_type=F32)

    def scores(j):
        k = ks_ref[0, 0, pl.ds(pl.multiple_of(j * tk, tk), tk), :]
        return [lax.dot_general(k, qa, nt, preferred_element_type=F32) for qa in q_grp]

    def softmax_stage(s):
        m_old = [m_ref[0:1, c] for c in groups]
        m_new = [jnp.maximum(mo, jnp.max(x, axis=0, keepdims=True)) for mo, x in zip(m_old, s)]
        alpha = [jnp.exp2(mo - mn) for mo, mn in zip(m_old, m_new)]
        p = [jnp.exp2(x - mn).astype(BF16) for x, mn in zip(s, m_new)]
        return m_new, alpha, p

    def pv_stage(j, m_new, alpha, p):
        vt = vs_ref[0, 0, :, pl.ds(pl.multiple_of(j * tk, tk), tk)]
        pv = [dot(vt, x) for x in p]
        for n, c in enumerate(groups):
            acc_ref[:, c] = acc_ref[:, c] * alpha[n] + pv[n]
            m_ref[0:1, c] = m_new[n]

    def accumulate(j, s):
        pv_stage(j, *softmax_stage(s))

    last = (i * tq) // tk
    for n, c in enumerate(groups):
        s_ref[:, c] = scores(0)[n]

    def body(j, carry):
        s_next = scores(j + 1)
        accumulate(j, [s_ref[:, c] for c in groups])
        for n, c in enumerate(groups):
            s_ref[:, c] = s_next[n]
        return carry

    lax.fori_loop(0, last, body, 0)

    wlen = WIN + tq
    start = pl.multiple_of(jnp.maximum(i * tq - WIN, 0), tq)
    kw = kw_ref[0, 0, pl.ds(start, wlen), :]
    s_win = [lax.dot_general(kw, q[c], nt, preferred_element_type=F32) for c in groups]
    causal = last * tk + lax.broadcasted_iota(jnp.int32, (tk, gw), 0) <= t_grp
    sel_stage = softmax_stage([jnp.where(causal, s_ref[:, c], NEG) for c in groups])
    kpos = start + lax.broadcasted_iota(jnp.int32, (wlen, gw), 0)
    in_window = (kpos <= t_grp) & (kpos > t_grp - WIN)
    s_win = [jnp.where(in_window, x, NEG) for x in s_win]
    p_win = [jnp.exp2(x - jnp.max(x, axis=0, keepdims=True)).astype(BF16) for x in s_win]
    pv_stage(last, *sel_stage)
    vwt = vw_ref[0, 0, :, pl.ds(start, wlen)]
    o_w = [dot(vwt, x) for x in p_win]

    sig = jax.nn.sigmoid(gl_ref[...])
    row = lax.broadcasted_iota(jnp.int32, (2 * R, sig.shape[1]), 0)
    lane = lax.broadcasted_iota(jnp.int32, (2 * R, sig.shape[1]), 1)
    pick = jnp.where(lane == H + (row // R) * H + g * R + (row % R), 1.0, 0.0)
    gates = lax.dot_general(pick, sig, nt, preferred_element_type=F32,
                            precision=lax.Precision.HIGHEST)
    g_slc = jnp.concatenate([gates[r:r + 1, :] for r in range(R)], axis=1)
    g_win = jnp.concatenate([gates[R + r:R + r + 1, :] for r in range(R)], axis=1)
    o_t = acc_ref[0:D, :] * (g_slc / acc_ref[D:D + 1, :])

    o_w = [x[0:D] * (g_win[:, c] / x[D:D + 1]) for x, c in zip(o_w, groups)]
    o_t = o_t + jnp.concatenate(o_w, axis=1)

    for r in range(R):
        o_ref[:, r * D:(r + 1) * D] = o_t[:, r * tq:(r + 1) * tq].T.astype(o_ref.dtype)


def nsa_selected_window(q, sb, ks_aug, vs, kw, vw, gl, batch, seq_len, tq=256, tk=512):
    H, G, D = NSA_HEADS, NSA_KV, NSA_HD
    R = H // G
    nq = seq_len // tq
    assert tk % tq == 0 and WIN % tq == 0 and seq_len >= WIN + tq and tq % LANES == 0 and COL_GROUPS == R
    full = lambda w: pl.BlockSpec((1, 1, seq_len, w), lambda b, g, i: (b, g, 0, 0))
    full_t = pl.BlockSpec((1, 1, D + V_ONES_ROWS, seq_len), lambda b, g, i: (b, g, 0, 0))
    return pl.pallas_call(
        functools.partial(_sel_win_kernel, tk=tk),
        grid=(batch, G, nq),
        in_specs=[pl.BlockSpec((1, R, tq, D), lambda b, g, i: (b, g, i, 0)),
                  pl.BlockSpec((1, 1, tq, D), lambda b, g, i: (b, g, i, 0)),
                  full(2 * D), full_t, full(D), full_t,
                  pl.BlockSpec((tq, gl.shape[1]), lambda b, g, i: (b * nq + i, 0))],
        out_specs=pl.BlockSpec((tq, R * D), lambda b, g, i: (b * nq + i, g)),
        out_shape=jax.ShapeDtypeStruct((batch * seq_len, H * D), BF16),
        scratch_shapes=[pltpu.VMEM((8, R * tq), F32), pltpu.VMEM((D + V_ONES_ROWS, R * tq), F32),
                        pltpu.VMEM((tk, R * tq), F32)],
        compiler_params=_cparams("parallel", "parallel", "arbitrary"),
        name="nsa_selected_window",
    )(q, sb, ks_aug, vs, kw, vw, gl)


def nsa_mixer_parts(y, gl, p, i, cos, sin, batch, seq_len):
    D = NSA_HD
    cos2 = jnp.concatenate([cos, cos], axis=1)
    sin2 = jnp.concatenate([-sin, sin], axis=1)
    q, kc, vc, ks_aug, vs, kw, vw = nsa_prep(y, cos2, sin2, batch, seq_len)
    kcmp = nsa_compress_blocks(kc, p["cmp_pos_k"][i], p["cmp_w1_k"][i], p["cmp_w2_k"][i])
    vcmp = nsa_compress_blocks(vc, p["cmp_pos_v"][i], p["cmp_w1_v"][i], p["cmp_w2_v"][i])
    o_cmp, sb = nsa_cmp_select(q, kcmp, vcmp, gl, batch, seq_len)
    o_sw = nsa_selected_window(q, sb, ks_aug, vs, kw, vw, gl, batch, seq_len)
    return o_cmp, o_sw


def rope_tables(S, dim):
    inv = 1.0 / (ROPE_THETA ** (jnp.arange(0, dim, 2, dtype=jnp.float32) / dim))
    ang = jnp.arange(S, dtype=jnp.float32)[:, None] * inv[None, :]
    return jnp.cos(ang), jnp.sin(ang)


def kernel(x, norm_w, ab_w_in, ab_w_out, gdn_conv, gdn_a_log, gdn_dt_bias, gdn_norm, sc_conv, nsa_w_in, nsa_w_out, cmp_pos_k, cmp_w1_k, cmp_w2_k, cmp_pos_v, cmp_w1_v, cmp_w2_v, ffn_w_gu, ffn_conv, ffn_w_down):
    B, S, D = x.shape
    p = dict(norm_w=norm_w, ab_w_in=ab_w_in, ab_w_out=ab_w_out, gdn_conv=gdn_conv, gdn_a_log=gdn_a_log,
             gdn_dt_bias=gdn_dt_bias, gdn_norm=gdn_norm, sc_conv=sc_conv, nsa_w_in=nsa_w_in,
             nsa_w_out=nsa_w_out, cmp_pos_k=cmp_pos_k, cmp_w1_k=cmp_w1_k, cmp_w2_k=cmp_w2_k,
             cmp_pos_v=cmp_pos_v, cmp_w1_v=cmp_w1_v, cmp_w2_v=cmp_w2_v, ffn_w_gu=ffn_w_gu,
             ffn_conv=ffn_conv, ffn_w_down=ffn_w_down)
    cos, sin = rope_tables(S, NSA_HD)
    h = x.reshape(B * S, D)
    for l in range(norm_w.shape[0]):
        _, h = run_layer(h, l, B, S, cos, sin, p)
    return h.reshape(B, S, D)


def run_layer(h, l, B, S, cos, sin, p):
    D = h.shape[1]
    i = l // 2
    norm_w = p["norm_w"]
    if l % 2 == 0:
        gw = GDN_HEADS * GDN_HD
        w = p["ab_w_in"][i].astype(BF16)
        w_main = jnp.concatenate([w[:, :4 * gw], w[:, 4 * gw + 2 * GDN_HEADS:]], axis=1)
        w_ab = jnp.pad(w[:, 4 * gw:4 * gw + 2 * GDN_HEADS], ((0, 0), (0, LANES - 2 * GDN_HEADS)))
        y, ab = norm_matmul(h, norm_w[l, 0], w_main, w_ab)
        y_a = gdn_heads(y, ab, p["gdn_conv"][i], p["gdn_a_log"][i], p["gdn_dt_bias"][i], p["gdn_norm"][i], B, S)
        y_b = short_conv(y, 4 * gw, p["sc_conv"][i], B, S)
        h1 = out_proj([y_a, y_b], p["ab_w_out"][i].astype(BF16), h, norm_w[l, 1], concat=True)
    else:
        w = p["nsa_w_in"][i].astype(BF16)
        nm = NSA_HEADS * NSA_HD + 6 * NSA_KV * NSA_HD
        w_gl = jnp.pad(w[:, nm:], ((0, 0), (0, LANES - (w.shape[1] - nm))))
        y, gl = norm_matmul(h, norm_w[l, 0], w[:, :nm], w_gl)
        o_cmp, o_sw = nsa_mixer_parts(y, gl, p, i, cos, sin, B, S)
        h1 = out_proj([o_cmp, o_sw], p["nsa_w_out"][i].astype(BF16), h, norm_w[l, 1], concat=False)
    h2 = conv_ffn_block(h1, S, norm_w[l, 2], p["ffn_w_gu"].astype(BF16), p["ffn_conv"],
                        p["ffn_w_down"].astype(BF16), norm_w[l, 3], l)
    return h1, h2
```

```python
import functools
import math

import jax
import jax.numpy as jnp
import numpy as np
from jax import lax
from jax.experimental import pallas as pl
from jax.experimental.pallas import tpu as pltpu

F32 = jnp.float32
BF16 = jnp.bfloat16

EPS = 1e-6
NEG = -1e30

GDN_HEADS = 8
GDN_HD = 128
GDN_CHUNK = 64
NSA_HEADS = 16
NSA_KV = 4
NSA_HD = 128
CMP_LEN = 32
CMP_STRIDE = 16
SEL_BLOCK = 64
SEL_N = 16
WIN = 512
FORCE_BONUS = 1000.0
ROPE_THETA = 10000.0
LOG2E = math.log2(math.e)
COL_GROUPS = 4
V_ONES_ROWS = 16

VMEM_LIMIT_BYTES = 56 * 1024 * 1024
BF16_ROWS = 16
LANES = 128


def _cparams(*sem, vmem=VMEM_LIMIT_BYTES):
    return pltpu.CompilerParams(dimension_semantics=sem, vmem_limit_bytes=vmem)


def _rms(x, w):
    return x * lax.rsqrt(jnp.mean(x * x, axis=-1, keepdims=True) + EPS) * w


def _norm_mm_kernel(h_ref, nw_ref, w_ref, ws_ref, o_ref, os_ref, xn_ref):
    @pl.when(pl.program_id(1) == 0)
    def _():
        xn_ref[...] = _rms(h_ref[...], nw_ref[...]).astype(BF16)
        os_ref[...] = jnp.dot(xn_ref[...], ws_ref[...], preferred_element_type=F32)

    o_ref[...] = jnp.dot(xn_ref[...], w_ref[...], preferred_element_type=F32).astype(o_ref.dtype)


def norm_matmul(h, nw, w, w_small, tm=1024, tn=1024):
    M, D = h.shape
    N = w.shape[1]
    Ns = w_small.shape[1]
    tn = min(tn, N)
    return pl.pallas_call(
        _norm_mm_kernel,
        grid=(M // tm, N // tn),
        in_specs=[pl.BlockSpec((tm, D), lambda i, j: (i, 0)),
                  pl.BlockSpec((1, D), lambda i, j: (0, 0)),
                  pl.BlockSpec((D, tn), lambda i, j: (0, j)),
                  pl.BlockSpec((D, Ns), lambda i, j: (0, 0))],
        out_specs=[pl.BlockSpec((tm, tn), lambda i, j: (i, j)),
                   pl.BlockSpec((tm, Ns), lambda i, j: (i, 0))],
        out_shape=[jax.ShapeDtypeStruct((M, N), BF16), jax.ShapeDtypeStruct((M, Ns), F32)],
        scratch_shapes=[pltpu.VMEM((tm, D), BF16)],
        compiler_params=_cparams("parallel", "arbitrary"),
        name="norm_matmul",
    )(h, nw.reshape(1, D), w, w_small)


def _out_proj_kernel(*refs, n_parts, concat):
    parts = refs[:n_parts]
    w_ref, h_ref, nw_ref, o_ref = refs[n_parts:]
    if concat:
        x = jnp.concatenate([p[...] for p in parts], axis=1)
    else:
        x = parts[0][...].astype(F32)
        for p in parts[1:]:
            x = x + p[...].astype(F32)
        x = x.astype(BF16)
    m = jnp.dot(x, w_ref[...], preferred_element_type=F32)
    o_ref[...] = h_ref[...] + _rms(m, nw_ref[...])


def out_proj(parts, w, h, nw, concat, tm=512):
    M, D = h.shape
    K = w.shape[0]
    in_specs = [pl.BlockSpec((tm, p.shape[1]), lambda i: (i, 0)) for p in parts]
    in_specs += [pl.BlockSpec((K, D), lambda i: (0, 0)),
                 pl.BlockSpec((tm, D), lambda i: (i, 0)),
                 pl.BlockSpec((1, D), lambda i: (0, 0))]
    return pl.pallas_call(
        functools.partial(_out_proj_kernel, n_parts=len(parts), concat=concat),
        grid=(M // tm,),
        in_specs=in_specs,
        out_specs=pl.BlockSpec((tm, D), lambda i: (i, 0)),
        out_shape=jax.ShapeDtypeStruct((M, D), F32),
        compiler_params=_cparams("parallel"),
        name="out_proj",
    )(*parts, w, h, nw.reshape(1, D))


def _ffn_kernel(h_ref, halo_ref, nw2_ref, wg_ref, wu_ref, cw_ref, wd_ref, nw3_ref, o_ref,
                xn_ref, g_ref, acc_ref, *, tiles_per_seq):
    i = pl.program_id(0)
    j = pl.program_id(1)
    tm = h_ref.shape[0]
    H = BF16_ROWS

    @pl.when(j == 0)
    def _():
        xn_ref[pl.ds(H, tm), :] = _rms(h_ref[...], nw2_ref[...]).astype(BF16)
        halo = _rms(halo_ref[...], nw2_ref[...])
        halo = jnp.where(i % tiles_per_seq == 0, 0.0, halo)
        xn_ref[pl.ds(0, H), :] = halo.astype(BF16)
        acc_ref[...] = jnp.zeros_like(acc_ref)

    g_ref[...] = jnp.dot(xn_ref[...], wg_ref[...], preferred_element_type=F32)
    u = jnp.dot(xn_ref[pl.ds(H, tm), :], wu_ref[...], preferred_element_type=F32)
    cw = cw_ref[...]
    c = (g_ref[pl.ds(H, tm), :] * cw[2:3, :] + g_ref[pl.ds(H - 1, tm), :] * cw[1:2, :]
         + g_ref[pl.ds(H - 2, tm), :] * cw[0:1, :])
    hid = (c * jax.nn.sigmoid(c) * u).astype(BF16)
    acc_ref[...] += jnp.dot(hid, wd_ref[...], preferred_element_type=F32)

    @pl.when(j == pl.num_programs(1) - 1)
    def _():
        o_ref[...] = h_ref[...] + _rms(acc_ref[...], nw3_ref[...])


def conv_ffn_block(h, seq_len, nw2, w_gu, conv_w, w_down, nw3, layer, tm=512, tf=512):
    M, D = h.shape
    FF = w_down.shape[1]
    nf = FF // tf
    H = BF16_ROWS
    kern = functools.partial(_ffn_kernel, tiles_per_seq=seq_len // tm)
    return pl.pallas_call(
        kern,
        grid=(M // tm, nf),
        in_specs=[pl.BlockSpec((tm, D), lambda i, j: (i, 0)),
                  pl.BlockSpec((H, D), lambda i, j: (jnp.maximum(i * (tm // H) - 1, 0), 0)),
                  pl.BlockSpec((1, D), lambda i, j: (0, 0)),
                  pl.BlockSpec((None, D, tf), lambda i, j: (layer, 0, j)),
                  pl.BlockSpec((None, D, tf), lambda i, j: (layer, 0, nf + j)),
                  pl.BlockSpec((None, conv_w.shape[1], tf), lambda i, j: (layer, 0, j)),
                  pl.BlockSpec((None, tf, D), lambda i, j: (layer, j, 0)),
                  pl.BlockSpec((1, D), lambda i, j: (0, 0))],
        out_specs=pl.BlockSpec((tm, D), lambda i, j: (i, 0)),
        out_shape=jax.ShapeDtypeStruct((M, D), F32),
        scratch_shapes=[pltpu.VMEM((tm + H, D), BF16),
                        pltpu.VMEM((tm + H, tf), F32),
                        pltpu.VMEM((tm, D), F32)],
        compiler_params=_cparams("parallel", "arbitrary"),
        name="conv_ffn",
    )(h, h, nw2.reshape(1, D), w_gu, w_gu, conv_w, w_down, nw3.reshape(1, D))


CONV_HALO = 8


def _silu(x):
    return x * jax.nn.sigmoid(x)


def _causal_conv_ext(ext_ref, x, cw, first):
    T = x.shape[0]
    K = cw.shape[0]

    @pl.when(first)
    def _():
        ext_ref[pl.ds(0, CONV_HALO), :] = jnp.zeros((CONV_HALO, x.shape[1]), F32)

    @pl.when(jnp.logical_not(first))
    def _():
        ext_ref[pl.ds(0, CONV_HALO), :] = ext_ref[pl.ds(T, CONV_HALO), :]

    ext_ref[pl.ds(CONV_HALO, T), :] = x
    y = ext_ref[pl.ds(CONV_HALO, T), :] * cw[K - 1:K, :]
    for k in range(K - 1):
        y = y + ext_ref[pl.ds(CONV_HALO - (K - 1) + k, T), :] * cw[k:k + 1, :]
    return y


def _gdn_kernel(q_ref, k_ref, v_ref, z_ref, ab_ref, cq_ref, ck_ref, cv_ref, alog_ref, dtb_ref, on_ref,
                o_ref, qe_ref, ke_ref, ve_ref, st_ref):
    t = pl.program_id(2)
    T = q_ref.shape[0]
    D = GDN_HD
    HP = q_ref.shape[1] // D
    C = GDN_CHUNK
    first = t == 0
    nt = (((1,), (1,)), ((), ()))

    @pl.when(first)
    def _():
        st_ref[...] = jnp.zeros_like(st_ref)

    q_all = _silu(_causal_conv_ext(qe_ref, q_ref[...].astype(F32), cq_ref[...], first))
    k_all = _silu(_causal_conv_ext(ke_ref, k_ref[...].astype(F32), ck_ref[...], first))
    v_all = _silu(_causal_conv_ext(ve_ref, v_ref[...].astype(F32), cv_ref[...], first))

    ab = ab_ref[...]
    lane = lax.broadcasted_iota(jnp.int32, ab.shape, 1)
    x = ab + dtb_ref[...]
    g_all = -jnp.exp(alog_ref[...]) * (jnp.maximum(x, 0.0) + jnp.log1p(jnp.exp(-jnp.abs(x))))
    g_all = jnp.where(lane < GDN_HEADS, g_all, 0.0)
    sig_ab = jax.nn.sigmoid(ab)
    ti = lax.broadcasted_iota(jnp.int32, (T, T), 0)
    tj = lax.broadcasted_iota(jnp.int32, (T, T), 1)
    tril = (ti >= tj) & (ti // C == tj // C)
    stril = tril & (ti > tj)
    gc_all = jnp.dot(jnp.where(tril, 1.0, 0.0), g_all, preferred_element_type=F32,
                     precision=lax.Precision.HIGHEST)
    lane8 = lax.broadcasted_iota(jnp.int32, (8, ab.shape[1]), 1)

    heads = range(HP)
    dot = functools.partial(jnp.dot, preferred_element_type=F32)
    hid = [pl.program_id(1) * HP + hh for hh in heads]
    cs = [slice(hh * D, (hh + 1) * D) for hh in heads]
    q = [q_all[:, c] for c in cs]
    k = [k_all[:, c] for c in cs]
    q = [x * lax.rsqrt(jnp.sum(x * x, axis=-1, keepdims=True) + EPS) * (D ** -0.5) for x in q]
    k = [x * lax.rsqrt(jnp.sum(x * x, axis=-1, keepdims=True) + EPS) for x in k]
    gcol = [jnp.sum(jnp.where(lane == h, gc_all, 0.0), axis=-1, keepdims=True) for h in hid]
    beta = [jnp.sum(jnp.where(lane == GDN_HEADS + h, sig_ab, 0.0), axis=-1, keepdims=True) for h in hid]
    grow = [lax.dot_general(jnp.where(lane8 == h, 1.0, 0.0), gc_all, nt, preferred_element_type=F32,
                            precision=lax.Precision.HIGHEST)[0:1, :] for h in hid]

    kb = [k[i] * beta[i] for i in heads]
    eg = [jnp.exp(gcol[i]) for i in heads]
    decay = [jnp.where(tril, jnp.exp(jnp.where(tril, gcol[i] - grow[i], 0.0)), 0.0) for i in heads]
    qk = [lax.dot_general(jnp.concatenate([q[i], kb[i]], axis=0).astype(BF16), k[i].astype(BF16), nt,
                          preferred_element_type=F32) for i in heads]
    a_qk = [(qk[i][:T] * decay[i]).astype(BF16) for i in heads]
    lm = [jnp.where(stril, qk[i][T:] * decay[i], 0.0) for i in heads]

    inv = [jnp.where(ti == tj, 1.0, 0.0) - jnp.where(ti // 2 == tj // 2, lm[i], 0.0) for i in heads]
    s = 2
    while s < C:
        off = (ti // (2 * s) == tj // (2 * s)) & (ti // s != tj // s)
        xb = [inv[i].astype(BF16) for i in heads]
        xc = [dot(xb[i], jnp.where(off, lm[i], 0.0).astype(BF16)) for i in heads]
        inv = [inv[i] - dot(xc[i].astype(BF16), xb[i]) for i in heads]
        s *= 2

    uw = [dot(inv[i].astype(BF16), jnp.concatenate([v_all[:, cs[i]] * beta[i], kb[i] * eg[i]], axis=1).astype(BF16))
          for i in heads]
    qg = [q[i] * eg[i] for i in heads]

    state = [st_ref[i] for i in heads]
    for c in range(T // C):
        r = slice(c * C, (c + 1) * C)
        ws = [dot(jnp.concatenate([uw[i][r, D:], qg[i][r]], axis=0).astype(BF16), state[i].astype(BF16))
              for i in heads]
        vnew = [(uw[i][r, :D] - ws[i][:C]).astype(BF16) for i in heads]
        glast = [gcol[i][c * C + C - 1:(c + 1) * C, :] for i in heads]
        kdec = [(k[i][r] * jnp.exp(glast[i] - gcol[i][r])).T.astype(BF16) for i in heads]
        state = [state[i] * jnp.exp(glast[i]) + dot(kdec[i], vnew[i]) for i in heads]
        for i in heads:
            o = ws[i][C:] + dot(a_qk[i][r, c * C:(c + 1) * C], vnew[i])
            zc = z_ref[pl.ds(c * C, C), cs[i]].astype(F32)
            o_ref[pl.ds(c * C, C), cs[i]] = (_rms(o, on_ref[...]) * _silu(zc)).astype(o_ref.dtype)
    for i in heads:
        st_ref[i] = state[i]


def gdn_heads(y, ab, conv_qkv, a_log, dt_bias, o_norm, batch, seq_len, T=256, heads_per_step=8):
    H, D = GDN_HEADS, GDN_HD
    HP = heads_per_step
    W = HP * D
    nG = H // HP
    nT = seq_len // T
    lanes = ab.shape[1]
    pad = lambda p: jnp.pad(p.astype(F32), (0, lanes - p.shape[0])).reshape(1, lanes)
    row = lambda b, h, t: b * nT + t
    col = lambda off: (lambda b, h, t: (row(b, h, t), off * nG + h))
    cw = lambda off: (lambda b, h, t: (0, off * nG + h))
    const = lambda b, h, t: (0, 0)
    K = conv_qkv.shape[0]
    return pl.pallas_call(
        _gdn_kernel,
        grid=(batch, nG, nT),
        in_specs=[pl.BlockSpec((T, W), col(0)), pl.BlockSpec((T, W), col(1)),
                  pl.BlockSpec((T, W), col(2)), pl.BlockSpec((T, W), col(3)),
                  pl.BlockSpec((T, lanes), lambda b, h, t: (row(b, h, t), 0)),
                  pl.BlockSpec((K, W), cw(0)), pl.BlockSpec((K, W), cw(1)), pl.BlockSpec((K, W), cw(2)),
                  pl.BlockSpec((1, lanes), const), pl.BlockSpec((1, lanes), const),
                  pl.BlockSpec((1, D), const)],
        out_specs=pl.BlockSpec((T, W), col(0)),
        out_shape=jax.ShapeDtypeStruct((batch * seq_len, H * D), BF16),
        scratch_shapes=[pltpu.VMEM((T + CONV_HALO, W), F32), pltpu.VMEM((T + CONV_HALO, W), F32),
                        pltpu.VMEM((T + CONV_HALO, W), F32), pltpu.VMEM((HP, D, D), F32)],
        compiler_params=_cparams("parallel", "parallel", "arbitrary"),
        name="gdn_heads",
    )(y, y, y, y, ab, conv_qkv, conv_qkv, conv_qkv, pad(a_log), pad(dt_bias), o_norm.reshape(1, D))


def _shortconv_kernel(hb_ref, gb_ref, gc_ref, cw_ref, o_ref, ext_ref):
    first = pl.program_id(2) == 0
    x = gc_ref[...].astype(F32) * hb_ref[...].astype(F32)
    y = _causal_conv_ext(ext_ref, x, cw_ref[...], first)
    o_ref[...] = (gb_ref[...].astype(F32) * y).astype(o_ref.dtype)


def short_conv(y, col_off, conv_sc, batch, seq_len, T=512, tc=512):
    W = conv_sc.shape[1]
    nT = seq_len // T
    nC = W // tc
    spec = lambda off: pl.BlockSpec((T, tc), lambda b, c, t: (b * nT + t, (col_off + off) // tc + c))
    return pl.pallas_call(
        _shortconv_kernel,
        grid=(batch, nC, nT),
        in_specs=[spec(0), spec(W), spec(2 * W),
                  pl.BlockSpec((conv_sc.shape[0], tc), lambda b, c, t: (0, c))],
        out_specs=pl.BlockSpec((T, tc), lambda b, c, t: (b * nT + t, c)),
        out_shape=jax.ShapeDtypeStruct((batch * seq_len, W), BF16),
        scratch_shapes=[pltpu.VMEM((T + CONV_HALO, tc), F32)],
        compiler_params=_cparams("parallel", "parallel", "arbitrary"),
        name="short_conv",
    )(y, y, y, conv_sc)


def _nsa_prep_kernel(y_ref, cos_ref, sin_ref, q_ref, kc_ref, vc_ref, ks_ref, vs_ref, kw_ref, vw_ref):
    T = y_ref.shape[0]
    H, G, D = NSA_HEADS, NSA_KV, NSA_HD
    cos = cos_ref[...]
    sin = sin_ref[...]

    def head(c):
        return y_ref[:, c * D:(c + 1) * D]

    def rope(x):
        x = x.astype(F32)
        return x * cos + pltpu.roll(x, D // 2, 1) * sin

    for hh in range(H):
        q_ref[0, hh] = (rope(head(hh)) * (D ** -0.5 * LOG2E)).astype(BF16)
    t0 = pl.program_id(1) * T
    tok = t0 + lax.broadcasted_iota(jnp.int32, (T, D), 0)
    lane = lax.broadcasted_iota(jnp.int32, (T, D), 1)
    onehot = jnp.where(tok // SEL_BLOCK == lane, 1.0, 0.0).astype(BF16)
    for g in range(G):
        kc_ref[0, g] = rope(head(H + g)).astype(BF16)
        vc_ref[0, g] = head(H + G + g)
        ks_ref[0, g, :, 0:D] = rope(head(H + 2 * G + g)).astype(BF16)
        ks_ref[0, g, :, D:2 * D] = onehot
        ones = jnp.ones((V_ONES_ROWS, T), BF16)
        vs_ref[0, g, 0:D, :] = head(H + 3 * G + g).astype(F32).T.astype(BF16)
        vs_ref[0, g, D:D + V_ONES_ROWS, :] = ones
        kw_ref[0, g] = rope(head(H + 4 * G + g)).astype(BF16)
        vw_ref[0, g, 0:D, :] = head(H + 5 * G + g).astype(F32).T.astype(BF16)
        vw_ref[0, g, D:D + V_ONES_ROWS, :] = ones


def nsa_prep(y, cos2, sin2, batch, seq_len, T=512):
    H, G, D = NSA_HEADS, NSA_KV, NSA_HD
    nT = seq_len // T
    assert seq_len // SEL_BLOCK <= D
    grp = lambda w: jax.ShapeDtypeStruct((batch, G, seq_len, w), BF16)
    gspec = lambda w: pl.BlockSpec((1, G, T, w), lambda b, t: (b, 0, t, 0))
    grp_t = jax.ShapeDtypeStruct((batch, G, D + V_ONES_ROWS, seq_len), BF16)
    tspec = pl.BlockSpec((1, G, D + V_ONES_ROWS, T), lambda b, t: (b, 0, 0, t))
    return pl.pallas_call(
        _nsa_prep_kernel,
        grid=(batch, nT),
        in_specs=[pl.BlockSpec((T, y.shape[1]), lambda b, t: (b * nT + t, 0)),
                  pl.BlockSpec((T, D), lambda b, t: (t, 0)),
                  pl.BlockSpec((T, D), lambda b, t: (t, 0))],
        out_specs=[pl.BlockSpec((1, H, T, D), lambda b, t: (b, 0, t, 0)),
                   gspec(D), gspec(D), gspec(2 * D), tspec, gspec(D), tspec],
        out_shape=[jax.ShapeDtypeStruct((batch, H, seq_len, D), BF16),
                   grp(D), grp(D), grp(2 * D), grp_t, grp(D), grp_t],
        compiler_params=_cparams("parallel", "parallel"),
        name="nsa_prep",
    )(y, cos2, sin2)


def _gelu_tanh(x):
    return 0.5 * x * (1.0 + jnp.tanh(math.sqrt(2.0 / math.pi) * (x + 0.044715 * (x * x * x))))


def _compress_kernel(x_ref, pos_ref, w1_ref, w2_ref, o_ref):
    x = x_ref[0].astype(F32)
    half = x.shape[1]
    pos = pos_ref[...]
    top = jnp.dot((x + pos[:, :half]).astype(BF16), w1_ref[0:half, :], preferred_element_type=F32)
    bot = jnp.dot((x + pos[:, half:]).astype(BF16), w1_ref[half:2 * half, :], preferred_element_type=F32)
    n = x.shape[0]
    hid = top + pltpu.roll(bot, n - 1, 0)
    o_ref[0] = jnp.dot(_gelu_tanh(hid).astype(BF16), w2_ref[...], preferred_element_type=F32).astype(BF16)


def nsa_compress_blocks(t, pos, w1, w2):
    B, G, S, D = t.shape
    assert CMP_LEN == 2 * CMP_STRIDE
    n = S // CMP_STRIDE
    x = t.reshape(B * G, n, CMP_STRIDE * D)
    hidden = w1.shape[1]
    return pl.pallas_call(
        _compress_kernel,
        grid=(B * G,),
        in_specs=[pl.BlockSpec((1, n, CMP_STRIDE * D), lambda i: (i, 0, 0)),
                  pl.BlockSpec((1, CMP_LEN * D), lambda i: (0, 0)),
                  pl.BlockSpec((CMP_LEN * D, hidden), lambda i: (0, 0)),
                  pl.BlockSpec((hidden, D), lambda i: (0, 0))],
        out_specs=pl.BlockSpec((1, n, D), lambda i: (i, 0, 0)),
        out_shape=jax.ShapeDtypeStruct((B * G, n, D), BF16),
        compiler_params=_cparams("parallel"),
        name="nsa_compress",
    )(x, pos.reshape(1, CMP_LEN * D).astype(F32), w1.astype(BF16), w2.astype(BF16))


def _cmp_select_kernel(q_ref, kc_ref, vc_ref, gl_ref, ov_ref, o_ref, sb_ref):
    g = pl.program_id(1)
    i = pl.program_id(2)
    R, tq, D = q_ref.shape[1], q_ref.shape[2], q_ref.shape[3]
    nc = kc_ref.shape[1]
    ns = ov_ref.shape[0]
    heads = range(R)
    nt = (((1,), (1,)), ((), ()))
    dot = functools.partial(jnp.dot, preferred_element_type=F32)
    kc = kc_ref[0]
    vct = vc_ref[0].astype(F32).T.astype(BF16)
    c_col = lax.broadcasted_iota(jnp.int32, (nc, tq), 0)
    t_row = i * tq + lax.broadcasted_iota(jnp.int32, (nc, tq), 1)
    valid = c_col * CMP_STRIDE + (CMP_LEN - 1) <= t_row
    s = [jnp.where(valid, lax.dot_general(kc, q_ref[0, r], nt, preferred_element_type=F32), NEG) for r in heads]
    e = [jnp.where(valid, jnp.exp2(x - jnp.max(x, axis=0, keepdims=True)), 0.0) for x in s]
    den = [jnp.sum(x, axis=0, keepdims=True) for x in e]
    p = [x / jnp.where(d > 0.0, d, 1.0) for x, d in zip(e, den)]
    o_t = [dot(vct, x.astype(BF16)) for x in p]
    sig = jax.nn.sigmoid(gl_ref[...])
    pick = jnp.where(lax.broadcasted_iota(jnp.int32, (8, sig.shape[1]), 1)
                     == g * R + lax.broadcasted_iota(jnp.int32, (8, sig.shape[1]), 0), 1.0, 0.0)
    gates = lax.dot_general(pick, sig, nt, preferred_element_type=F32,
                            precision=lax.Precision.HIGHEST)
    for r in heads:
        o_ref[:, r * D:(r + 1) * D] = (o_t[r] * gates[r:r + 1, :]).T.astype(o_ref.dtype)

    psum = functools.reduce(lambda a, b: a + b, p)
    ov = ov_ref[...]
    hi = psum.astype(BF16)
    r1 = psum - hi.astype(F32)
    mid = r1.astype(BF16)
    lo = (r1 - mid.astype(F32)).astype(BF16)
    score = dot(ov, hi) + (dot(ov, mid) + dot(ov, lo))
    n_col = lax.broadcasted_iota(jnp.int32, (ns, tq), 0)
    t_row = i * tq + lax.broadcasted_iota(jnp.int32, (ns, tq), 1)
    cur = t_row // SEL_BLOCK
    forced = (n_col == 0) | (n_col == cur) | (n_col == cur - 1)
    score = jnp.where(n_col * SEL_BLOCK <= t_row, score + jnp.where(forced, FORCE_BONUS, 0.0), NEG)
    SUB = 8
    rows = [score[a:a + SUB, :] for a in range(0, ns, SUB)]
    ranks = [jnp.zeros((SUB, tq), F32) for _ in rows]
    sub_row = lax.broadcasted_iota(jnp.int32, (SUB, tq), 0)
    for m in range(ns):
        sm = jnp.broadcast_to(score[m:m + 1, :], (SUB, tq))
        for a, sc in enumerate(rows):
            if a * SUB + SUB - 1 < m:
                ahead = sm > sc
            elif a * SUB > m:
                ahead = sm >= sc
            else:
                ahead = (sm > sc) | ((sm == sc) & (sub_row > m - a * SUB))
            ranks[a] = ranks[a] + jnp.where(ahead, 1.0, 0.0)
    rank = jnp.concatenate(ranks, axis=0)
    bias = jnp.where(rank < float(min(SEL_N, ns)), 0.0, NEG)
    bias = jnp.concatenate([bias, jnp.zeros((sb_ref.shape[3] - ns, tq), F32)], axis=0)
    sb_ref[0, 0] = bias.T.astype(sb_ref.dtype)


def _overlap_matrix(nc, ns):
    cs = np.arange(nc) * CMP_STRIDE
    ss = np.arange(ns) * SEL_BLOCK
    ov = np.minimum(cs[None, :] + CMP_LEN, ss[:, None] + SEL_BLOCK) - np.maximum(cs[None, :], ss[:, None])
    ov = np.clip(ov, 0, None) / CMP_STRIDE
    ov[:, nc - CMP_LEN // CMP_STRIDE + 1:] = 0.0
    return jnp.asarray(ov, BF16)


def nsa_cmp_select(q, kcmp, vcmp, gl, batch, seq_len, tq=512):
    H, G, D = NSA_HEADS, NSA_KV, NSA_HD
    R = H // G
    nq = seq_len // tq
    nc = kcmp.shape[1]
    ns = seq_len // SEL_BLOCK
    ov = _overlap_matrix(nc, ns)
    return pl.pallas_call(
        _cmp_select_kernel,
        grid=(batch, G, nq),
        in_specs=[pl.BlockSpec((1, R, tq, D), lambda b, g, i: (b, g, i, 0)),
                  pl.BlockSpec((1, nc, D), lambda b, g, i: (b * G + g, 0, 0)),
                  pl.BlockSpec((1, nc, D), lambda b, g, i: (b * G + g, 0, 0)),
                  pl.BlockSpec((tq, gl.shape[1]), lambda b, g, i: (b * nq + i, 0)),
                  pl.BlockSpec((ns, nc), lambda b, g, i: (0, 0))],
        out_specs=[pl.BlockSpec((tq, R * D), lambda b, g, i: (b * nq + i, g)),
                   pl.BlockSpec((1, 1, tq, D), lambda b, g, i: (b, g, i, 0))],
        out_shape=[jax.ShapeDtypeStruct((batch * seq_len, H * D), BF16),
                   jax.ShapeDtypeStruct((batch, G, seq_len, D), BF16)],
        compiler_params=_cparams("parallel", "parallel", "parallel"),
        name="nsa_cmp_select",
    )(q, kcmp, vcmp, gl, ov)


def _sel_win_kernel(q_ref, sb_ref, ks_ref, vs_ref, gl_ref, o_ref, m_ref, acc_ref, s_ref, *, tk):
    g = pl.program_id(1)
    i = pl.program_id(2)
    R, tq, D = q_ref.shape[1], q_ref.shape[2], q_ref.shape[3]
    cols = R * tq
    H = NSA_HEADS
    nt = (((1,), (1,)), ((), ()))
    q = jnp.concatenate([q_ref[0, r] for r in range(R)], axis=0)
    sb = sb_ref[0, 0]
    q_aug = jnp.concatenate([q, jnp.concatenate([sb] * R, axis=0)], axis=1)
    t_grp = i * tq + lax.broadcasted_iota(jnp.int32, (1, tq), 1)

    m_ref[...] = jnp.full_like(m_ref, NEG)
    acc_ref[...] = jnp.zeros_like(acc_ref)

    gw = cols // COL_GROUPS
    groups = [slice(c * gw, (c + 1) * gw) for c in range(COL_GROUPS)]
    q_grp = [q_aug[c] for c in groups]
    dot = functools.partial(jnp.dot, preferred_element_type=F32)

    def scores(j):
        k = ks_ref[0, 0, pl.ds(pl.multiple_of(j * tk, tk), tk), :]
        return [lax.dot_general(k, qa, nt, preferred_element_type=F32) for qa in q_grp]

    def softmax_stage(s):
        m_old = [m_ref[0:1, c] for c in groups]
        m_new = [jnp.maximum(mo, jnp.max(x, axis=0, keepdims=True)) for mo, x in zip(m_old, s)]
        alpha = [jnp.exp2(mo - mn) for mo, mn in zip(m_old, m_new)]
        p = [jnp.exp2(x - mn).astype(BF16) for x, mn in zip(s, m_new)]
        return m_new, alpha, p

    def pv_stage(j, m_new, alpha, p):
        vt = vs_ref[0, 0, :, pl.ds(pl.multiple_of(j * tk, tk), tk)]
        pv = [dot(vt, x) for x in p]
        for n, c in enumerate(groups):
            acc_ref[:, c] = acc_ref[:, c] * alpha[n] + pv[n]
            m_ref[0:1, c] = m_new[n]

    def accumulate(j, s):
        pv_stage(j, *softmax_stage(s))

    last = (i * tq) // tk
    for n, c in enumerate(groups):
        s_ref[:, c] = scores(0)[n]

    def body(j, carry):
        s_next = scores(j + 1)
        accumulate(j, [s_ref[:, c] for c in groups])
        for n, c in enumerate(groups):
            s_ref[:, c] = s_next[n]
        return carry

    lax.fori_loop(0, last, body, 0)

    causal = last * tk + lax.broadcasted_iota(jnp.int32, (tk, gw), 0) <= t_grp
    accumulate(last, [jnp.where(causal, s_ref[:, c], NEG) for c in groups])

    g_slc = _gate_rows(gl_ref, H + g * R, R)
    o_t = acc_ref[0:D, :] * (g_slc / acc_ref[D:D + 1, :])
    for r in range(R):
        o_ref[:, r * D:(r + 1) * D] = o_t[:, r * tq:(r + 1) * tq].T.astype(o_ref.dtype)


def _gate_rows(gl_ref, first_lane, n):
    sig = jax.nn.sigmoid(gl_ref[...])
    row = lax.broadcasted_iota(jnp.int32, (8, sig.shape[1]), 0)
    lane = lax.broadcasted_iota(jnp.int32, (8, sig.shape[1]), 1)
    pick = jnp.where(lane == first_lane + row, 1.0, 0.0)
    gates = lax.dot_general(pick, sig, (((1,), (1,)), ((), ())), preferred_element_type=F32,
                            precision=lax.Precision.HIGHEST)
    return jnp.concatenate([gates[r:r + 1, :] for r in range(n)], axis=1)


def _window_kernel(q_ref, kw_ref, vw_ref, gl_ref, o_ref):
    g = pl.program_id(1)
    i = pl.program_id(2)
    R, tq, D = q_ref.shape[1], q_ref.shape[2], q_ref.shape[3]
    nt = (((1,), (1,)), ((), ()))
    dot = functools.partial(jnp.dot, preferred_element_type=F32)
    t_grp = i * tq + lax.broadcasted_iota(jnp.int32, (1, tq), 1)
    wlen = WIN + tq
    start = pl.multiple_of(jnp.maximum(i * tq - WIN, 0), tq)
    kw = kw_ref[0, 0, pl.ds(start, wlen), :]
    vwt = vw_ref[0, 0, :, pl.ds(start, wlen)]
    kpos = start + lax.broadcasted_iota(jnp.int32, (wlen, tq), 0)
    in_window = (kpos <= t_grp) & (kpos > t_grp - WIN)
    s = [jnp.where(in_window, lax.dot_general(kw, q_ref[0, r], nt, preferred_element_type=F32), NEG)
         for r in range(R)]
    p = [jnp.exp2(x - jnp.max(x, axis=0, keepdims=True)).astype(BF16) for x in s]
    o_w = [dot(vwt, x) for x in p]
    g_win = _gate_rows(gl_ref, 2 * NSA_HEADS + g * R, R)
    for r in range(R):
        o = o_w[r][0:D] * (g_win[:, r * tq:(r + 1) * tq] / o_w[r][D:D + 1])
        o_ref[:, r * D:(r + 1) * D] = o.T.astype(o_ref.dtype)


def nsa_selected_window(q, sb, ks_aug, vs, kw, vw, gl, batch, seq_len, tq=256, tk=512):
    H, G, D = NSA_HEADS, NSA_KV, NSA_HD
    R = H // G
    nq = seq_len // tq
    assert tk % tq == 0 and WIN % tq == 0 and seq_len >= WIN + tq and tq % LANES == 0 and COL_GROUPS == R
    full = lambda w: pl.BlockSpec((1, 1, seq_len, w), lambda b, g, i: (b, g, 0, 0))
    full_t = pl.BlockSpec((1, 1, D + V_ONES_ROWS, seq_len), lambda b, g, i: (b, g, 0, 0))
    q_spec = pl.BlockSpec((1, R, tq, D), lambda b, g, i: (b, g, i, 0))
    gl_spec = pl.BlockSpec((tq, gl.shape[1]), lambda b, g, i: (b * nq + i, 0))
    out_spec = pl.BlockSpec((tq, R * D), lambda b, g, i: (b * nq + i, g))
    out_shape = jax.ShapeDtypeStruct((batch * seq_len, H * D), BF16)
    o_slc = pl.pallas_call(
        functools.partial(_sel_win_kernel, tk=tk),
        grid=(batch, G, nq),
        in_specs=[q_spec, pl.BlockSpec((1, 1, tq, D), lambda b, g, i: (b, g, i, 0)),
                  full(2 * D), full_t, gl_spec],
        out_specs=out_spec,
        out_shape=out_shape,
        scratch_shapes=[pltpu.VMEM((8, R * tq), F32), pltpu.VMEM((D + V_ONES_ROWS, R * tq), F32),
                        pltpu.VMEM((tk, R * tq), F32)],
        compiler_params=_cparams("parallel", "parallel", "arbitrary"),
        name="nsa_selected",
    )(q, sb, ks_aug, vs, gl)
    o_win = pl.pallas_call(
        _window_kernel,
        grid=(batch, G, nq),
        in_specs=[q_spec, full(D), full_t, gl_spec],
        out_specs=out_spec,
        out_shape=out_shape,
        compiler_params=_cparams("parallel", "parallel", "arbitrary"),
        name="nsa_window",
    )(q, kw, vw, gl)
    return o_slc, o_win


def nsa_mixer_parts(y, gl, p, i, cos, sin, batch, seq_len):
    D = NSA_HD
    cos2 = jnp.concatenate([cos, cos], axis=1)
    sin2 = jnp.concatenate([-sin, sin], axis=1)
    q, kc, vc, ks_aug, vs, kw, vw = nsa_prep(y, cos2, sin2, batch, seq_len)
    kcmp = nsa_compress_blocks(kc, p["cmp_pos_k"][i], p["cmp_w1_k"][i], p["cmp_w2_k"][i])
    vcmp = nsa_compress_blocks(vc, p["cmp_pos_v"][i], p["cmp_w1_v"][i], p["cmp_w2_v"][i])
    o_cmp, sb = nsa_cmp_select(q, kcmp, vcmp, gl, batch, seq_len)
    o_slc, o_win = nsa_selected_window(q, sb, ks_aug, vs, kw, vw, gl, batch, seq_len)
    return o_cmp, o_slc, o_win


def rope_tables(S, dim):
    inv = 1.0 / (ROPE_THETA ** (jnp.arange(0, dim, 2, dtype=jnp.float32) / dim))
    ang = jnp.arange(S, dtype=jnp.float32)[:, None] * inv[None, :]
    return jnp.cos(ang), jnp.sin(ang)


def kernel(x, norm_w, ab_w_in, ab_w_out, gdn_conv, gdn_a_log, gdn_dt_bias, gdn_norm, sc_conv, nsa_w_in, nsa_w_out, cmp_pos_k, cmp_w1_k, cmp_w2_k, cmp_pos_v, cmp_w1_v, cmp_w2_v, ffn_w_gu, ffn_conv, ffn_w_down):
    B, S, D = x.shape
    p = dict(norm_w=norm_w, ab_w_in=ab_w_in, ab_w_out=ab_w_out, gdn_conv=gdn_conv, gdn_a_log=gdn_a_log,
             gdn_dt_bias=gdn_dt_bias, gdn_norm=gdn_norm, sc_conv=sc_conv, nsa_w_in=nsa_w_in,
             nsa_w_out=nsa_w_out, cmp_pos_k=cmp_pos_k, cmp_w1_k=cmp_w1_k, cmp_w2_k=cmp_w2_k,
             cmp_pos_v=cmp_pos_v, cmp_w1_v=cmp_w1_v, cmp_w2_v=cmp_w2_v, ffn_w_gu=ffn_w_gu,
             ffn_conv=ffn_conv, ffn_w_down=ffn_w_down)
    cos, sin = rope_tables(S, NSA_HD)
    h = x.reshape(B * S, D)
    for l in range(norm_w.shape[0]):
        _, h = run_layer(h, l, B, S, cos, sin, p)
    return h.reshape(B, S, D)


def run_layer(h, l, B, S, cos, sin, p):
    D = h.shape[1]
    i = l // 2
    norm_w = p["norm_w"]
    if l % 2 == 0:
        gw = GDN_HEADS * GDN_HD
        w = p["ab_w_in"][i].astype(BF16)
        w_main = jnp.concatenate([w[:, :4 * gw], w[:, 4 * gw + 2 * GDN_HEADS:]], axis=1)
        w_ab = jnp.pad(w[:, 4 * gw:4 * gw + 2 * GDN_HEADS], ((0, 0), (0, LANES - 2 * GDN_HEADS)))
        y, ab = norm_matmul(h, norm_w[l, 0], w_main, w_ab)
        y_a = gdn_heads(y, ab, p["gdn_conv"][i], p["gdn_a_log"][i], p["gdn_dt_bias"][i], p["gdn_norm"][i], B, S)
        y_b = short_conv(y, 4 * gw, p["sc_conv"][i], B, S)
        h1 = out_proj([y_a, y_b], p["ab_w_out"][i].astype(BF16), h, norm_w[l, 1], concat=True)
    else:
        w = p["nsa_w_in"][i].astype(BF16)
        nm = NSA_HEADS * NSA_HD + 6 * NSA_KV * NSA_HD
        w_gl = jnp.pad(w[:, nm:], ((0, 0), (0, LANES - (w.shape[1] - nm))))
        y, gl = norm_matmul(h, norm_w[l, 0], w[:, :nm], w_gl)
        parts = nsa_mixer_parts(y, gl, p, i, cos, sin, B, S)
        h1 = out_proj(list(parts), p["nsa_w_out"][i].astype(BF16), h, norm_w[l, 1], concat=False)
    h2 = conv_ffn_block(h1, S, norm_w[l, 2], p["ffn_w_gu"].astype(BF16), p["ffn_conv"],
                        p["ffn_w_down"].astype(BF16), norm_w[l, 3], l)
    return h1, h2
```
